```python
import math
import jax, jax.numpy as jnp
from jax import lax
import numpy as np

D_MODEL = 1024
BATCH = 2
SEQ = 8192
DEPTH = 4

NORM_EPS = 1e-6
CONV_A_WIDTH = D_MODEL
CONV_A_K = 3
DN_HEAD_DIM = 128
DN_HEADS = D_MODEL // DN_HEAD_DIM
DN_WIDTH = DN_HEADS * DN_HEAD_DIM
DN_CONV_K = 4
DN_CHUNK = 64
MOBA_HEAD_DIM = 128
MOBA_HEADS = D_MODEL // MOBA_HEAD_DIM
MOBA_WIDTH = MOBA_HEADS * MOBA_HEAD_DIM
MOBA_BLOCK = 256
MOBA_TOPK = 3
MOBA_Q_CHUNK = 32
ROPE_THETA = 10000.0
N_BRANCHES = 3
FFN_HIDDEN = -(-(8 * D_MODEL) // (3 * 256)) * 256
IN_SPLITS = (CONV_A_WIDTH, CONV_A_WIDTH, CONV_A_WIDTH,
             3 * DN_WIDTH, DN_WIDTH, DN_HEADS, DN_HEADS,
             3 * MOBA_WIDTH, N_BRANCHES * D_MODEL)
N_IN = sum(IN_SPLITS)

kernel_name = "hybrid_conv_deltanet_moba_block"


def rms_norm(x, w):
    xf = x.astype(jnp.float32)
    y = xf * lax.rsqrt(jnp.mean(xf * xf, axis=-1, keepdims=True) + NORM_EPS)
    return (y * w.astype(jnp.float32)).astype(x.dtype)


def l2_norm(x):
    return x * lax.rsqrt(jnp.sum(x * x, axis=-1, keepdims=True) + NORM_EPS)


def split_columns(t, sizes):
    offsets = np.cumsum(np.array(sizes))[:-1].tolist()
    return jnp.split(t, offsets, axis=-1)


def causal_dwconv(x, w):
    K = w.shape[0]
    S = x.shape[1]
    xp = jnp.pad(x, ((0, 0), (K - 1, 0), (0, 0)))
    return sum(xp[:, j:j + S] * w[j] for j in range(K))


def rope_tables(seq, dim):
    inv = 1.0 / (ROPE_THETA ** (jnp.arange(0, dim, 2, dtype=jnp.float32) / dim))
    ang = jnp.arange(seq, dtype=jnp.float32)[:, None] * inv[None, :]
    return jnp.cos(ang), jnp.sin(ang)


def apply_rope(x, cos, sin):
    xf = x.astype(jnp.float32)
    x1, x2 = jnp.split(xf, 2, axis=-1)
    out = jnp.concatenate([x1 * cos - x2 * sin, x2 * cos + x1 * sin], axis=-1)
    return out.astype(x.dtype)


def chunk_gated_delta_rule(q, k, v, g, beta):
    B, H, S, Dk = q.shape
    Dv = v.shape[-1]
    C = DN_CHUNK
    N = S // C
    q = (q * (Dk ** -0.5)).reshape(B, H, N, C, Dk)
    k = k.reshape(B, H, N, C, Dk)
    v = v.reshape(B, H, N, C, Dv)
    g = jnp.cumsum(g.reshape(B, H, N, C), axis=-1)
    beta = beta.reshape(B, H, N, C)
    causal = jnp.tril(jnp.ones((C, C), dtype=bool))
    strict = jnp.tril(jnp.ones((C, C), dtype=bool), k=-1)
    diff = g[..., :, None] - g[..., None, :]
    decay = jnp.where(causal, jnp.exp(jnp.where(causal, diff, 0.0)), 0.0)
    k_beta = k * beta[..., None]
    a_mat = jnp.where(strict, jnp.einsum('bhnid,bhnjd->bhnij', k_beta, k) * decay, 0.0)
    eye = jnp.eye(C, dtype=jnp.float32)
    t_mat = lax.linalg.triangular_solve(eye + a_mat, jnp.broadcast_to(eye, a_mat.shape),
                                        left_side=True, lower=True)
    u = jnp.einsum('bhnij,bhnjd->bhnid', t_mat, v * beta[..., None])
    w = jnp.einsum('bhnij,bhnjd->bhnid', t_mat, k_beta * jnp.exp(g)[..., None])
    qk = jnp.einsum('bhnid,bhnjd->bhnij', q, k) * decay
    q_dec = q * jnp.exp(g)[..., None]
    k_dec = k * jnp.exp(g[..., -1:] - g)[..., None]
    chunk_decay = jnp.exp(g[..., -1])

    def step(state, xs):
        u_n, w_n, qk_n, qd_n, kd_n, cd_n = xs
        v_new = u_n - jnp.einsum('bhcd,bhde->bhce', w_n, state)
        o_n = (jnp.einsum('bhcd,bhde->bhce', qd_n, state)
               + jnp.einsum('bhij,bhje->bhie', qk_n, v_new))
        state = state * cd_n[..., None, None] + jnp.einsum('bhcd,bhce->bhde', kd_n, v_new)
        return state, o_n

    xs = tuple(jnp.moveaxis(t, 2, 0) for t in (u, w, qk, q_dec, k_dec, chunk_decay))
    state0 = jnp.zeros((B, H, Dk, Dv), jnp.float32)
    _, o = lax.scan(step, state0, xs)
    return jnp.moveaxis(o, 0, 2).reshape(B, H, S, Dv)


def moba_attention(q, k, v):
    B, H, S, Dh = q.shape
    BS = MOBA_BLOCK
    QC = MOBA_Q_CHUNK
    NB = -(-S // BS)
    S_pad = NB * BS
    pad = ((0, 0), (0, 0), (0, S_pad - S), (0, 0))
    q, k, v = (jnp.pad(t, pad) for t in (q, k, v))
    scale = Dh ** -0.5
    topk = min(MOBA_TOPK, NB)
    kb = k.reshape(B, H, NB, BS, Dh)
    vb = v.reshape(B, H, NB, BS, Dh)
    k_mean = jnp.mean(kb.astype(jnp.float32), axis=3)
    pos = jnp.arange(S_pad)
    q_blk = pos // BS
    gate = jnp.einsum('bhsd,bhnd->bhsn', q.astype(jnp.float32), k_mean)
    fully_past = jnp.arange(NB)[None, :] < q_blk[:, None]
    gate = jnp.where(fully_past, gate, -jnp.inf)
    _, sel = lax.top_k(gate, topk)
    sel_valid = jnp.arange(topk)[None, :] < q_blk[:, None]

    nq = S_pad // QC
    q_c = jnp.moveaxis(q.reshape(B, H, nq, QC, Dh), 2, 0)
    sel_c = jnp.moveaxis(sel.reshape(B, H, nq, QC, topk), 2, 0)
    valid_c = sel_valid.reshape(nq, QC, topk)
    pos_c = pos.reshape(nq, QC)
    gather_blocks = jax.vmap(jax.vmap(lambda blocks, idx: blocks[idx]))

    def query_chunk(args):
        qc, selc, validc, posc = args
        own = posc[0] // BS
        k_own = lax.dynamic_index_in_dim(kb, own, axis=2, keepdims=False)
        v_own = lax.dynamic_index_in_dim(vb, own, axis=2, keepdims=False)
        k_sel = gather_blocks(kb, selc)
        v_sel = gather_blocks(vb, selc)
        s_sel = jnp.einsum('bhqd,bhqnkd->bhqnk', qc, k_sel).astype(jnp.float32) * scale
        s_sel = jnp.where(validc[None, None, :, :, None], s_sel, -jnp.inf)
        s_own = jnp.einsum('bhqd,bhkd->bhqk', qc, k_own).astype(jnp.float32) * scale
        k_pos = own * BS + jnp.arange(BS)
        s_own = jnp.where(k_pos[None, :] <= posc[:, None], s_own, -jnp.inf)
        logits = jnp.concatenate([s_sel.reshape(B, H, QC, topk * BS), s_own], axis=-1)
        p = jax.nn.softmax(logits, axis=-1).astype(v.dtype)
        p_sel = p[..., :topk * BS].reshape(B, H, QC, topk, BS)
        p_own = p[..., topk * BS:]
        return (jnp.einsum('bhqnk,bhqnkd->bhqd', p_sel, v_sel)
                + jnp.einsum('bhqk,bhkd->bhqd', p_own, v_own))

    o = lax.map(query_chunk, (q_c, sel_c, valid_c, pos_c))
    o = jnp.moveaxis(o, 0, 2).reshape(B, H, S_pad, Dh)
    return o[:, :, :S, :]


def hybrid_layer(x, cos, sin, attn_norm, w_in, conv_a_w, dn_conv_w, dn_a_log, dn_dt_bias,
                 dn_norm, w_br_a, w_br_dn, w_br_moba, w_out, ffn_norm, w_gate_up, w_down):
    B, S, _ = x.shape
    h = rms_norm(x, attn_norm)
    proj = h @ w_in
    (a_x, a_c, a_b, dn_qkv, dn_z, dn_b, dn_a, mb_qkv, gate_logits) = split_columns(proj, IN_SPLITS)

    y_a = a_b * causal_dwconv(a_c * a_x, conv_a_w)

    dn_qkv = jax.nn.silu(causal_dwconv(dn_qkv, dn_conv_w)).astype(jnp.float32)
    to_heads = lambda t: t.reshape(B, S, DN_HEADS, DN_HEAD_DIM).transpose(0, 2, 1, 3)
    dq, dk, dv = (to_heads(t) for t in jnp.split(dn_qkv, 3, axis=-1))
    beta = jax.nn.sigmoid(dn_b.astype(jnp.float32)).transpose(0, 2, 1)
    log_decay = (-jnp.exp(dn_a_log.astype(jnp.float32))
                 * jax.nn.softplus(dn_a.astype(jnp.float32) + dn_dt_bias.astype(jnp.float32)))
    o_dn = chunk_gated_delta_rule(l2_norm(dq), l2_norm(dk), dv, log_decay.transpose(0, 2, 1), beta)
    o_dn = o_dn.transpose(0, 2, 1, 3).astype(x.dtype)
    z = dn_z.reshape(B, S, DN_HEADS, DN_HEAD_DIM)
    y_dn = (rms_norm(o_dn, dn_norm) * jax.nn.silu(z)).reshape(B, S, DN_WIDTH)

    to_mheads = lambda t: t.reshape(B, S, MOBA_HEADS, MOBA_HEAD_DIM).transpose(0, 2, 1, 3)
    mq, mk, mv = (to_mheads(t) for t in jnp.split(mb_qkv, 3, axis=-1))
    o_mb = moba_attention(apply_rope(mq, cos, sin), apply_rope(mk, cos, sin), mv)
    y_mb = o_mb.transpose(0, 2, 1, 3).reshape(B, S, MOBA_WIDTH)

    g_a, g_dn, g_mb = jnp.split(jax.nn.sigmoid(gate_logits), N_BRANCHES, axis=-1)
    merged = g_a * (y_a @ w_br_a) + g_dn * (y_dn @ w_br_dn) + g_mb * (y_mb @ w_br_moba)
    x = x + merged @ w_out

    h2 = rms_norm(x, ffn_norm)
    gg, uu = jnp.split(h2 @ w_gate_up, 2, axis=-1)
    return x + (jax.nn.silu(gg) * uu) @ w_down


def setup_inputs(seed: int = 0) -> dict:
    key = jax.random.key(seed)
    ks = jax.random.split(key, 17)
    f32 = jnp.float32
    nrm = lambda k, shape, fan_in: jax.random.normal(k, shape, f32) * (fan_in ** -0.5)
    gain = lambda k, shape: 1.0 + 0.02 * jax.random.normal(k, shape, f32)
    dt = jnp.exp(jax.random.uniform(ks[6], (DEPTH, DN_HEADS), f32, math.log(1e-3), math.log(1e-1)))
    return {
        "x": jax.random.normal(ks[0], (BATCH, SEQ, D_MODEL), f32),
        "attn_norm": gain(ks[1], (DEPTH, D_MODEL)),
        "w_in": nrm(ks[2], (DEPTH, D_MODEL, N_IN), D_MODEL),
        "conv_a_w": nrm(ks[3], (DEPTH, CONV_A_K, CONV_A_WIDTH), CONV_A_K),
        "dn_conv_w": nrm(ks[4], (DEPTH, DN_CONV_K, 3 * DN_WIDTH), DN_CONV_K),
        "dn_a_log": jnp.log(jax.random.uniform(ks[5], (DEPTH, DN_HEADS), f32, 1.0, 16.0)),
        "dn_dt_bias": dt + jnp.log(-jnp.expm1(-dt)),
        "dn_norm": gain(ks[7], (DEPTH, DN_HEAD_DIM)),
        "w_br_a": nrm(ks[8], (DEPTH, CONV_A_WIDTH, D_MODEL), CONV_A_WIDTH),
        "w_br_dn": nrm(ks[9], (DEPTH, DN_WIDTH, D_MODEL), DN_WIDTH),
        "w_br_moba": nrm(ks[10], (DEPTH, MOBA_WIDTH, D_MODEL), MOBA_WIDTH),
        "w_out": nrm(ks[11], (DEPTH, D_MODEL, D_MODEL), D_MODEL),
        "ffn_norm": gain(ks[12], (DEPTH, D_MODEL)),
        "w_gate_up": nrm(ks[13], (DEPTH, D_MODEL, 2 * FFN_HIDDEN), D_MODEL),
        "w_down": nrm(ks[14], (DEPTH, FFN_HIDDEN, D_MODEL), FFN_HIDDEN),
        "final_norm": gain(ks[15], (D_MODEL,)),
    }


def reference(x, attn_norm, w_in, conv_a_w, dn_conv_w, dn_a_log, dn_dt_bias, dn_norm,
              w_br_a, w_br_dn, w_br_moba, w_out, ffn_norm, w_gate_up, w_down, final_norm):
    cos, sin = rope_tables(x.shape[1], MOBA_HEAD_DIM)
    for l in range(DEPTH):
        x = hybrid_layer(x, cos, sin, attn_norm[l], w_in[l], conv_a_w[l], dn_conv_w[l],
                         dn_a_log[l], dn_dt_bias[l], dn_norm[l], w_br_a[l], w_br_dn[l],
                         w_br_moba[l], w_out[l], ffn_norm[l], w_gate_up[l], w_down[l])
    return rms_norm(x, final_norm)
```

```python
import functools

import jax
import jax.numpy as jnp
from jax import lax
from jax.experimental import pallas as pl
from jax.experimental.pallas import tpu as pltpu

F32 = jnp.float32
BF16 = jnp.bfloat16

D_MODEL = 1024
HEADS = 8
HEAD_DIM = 128
NORM_EPS = 1e-6
CONV_A_K = 3
DN_CONV_K = 4
DN_CHUNK = 64
MOBA_BLOCK = 256
MOBA_TOPK = 3
ROPE_THETA = 10000.0
FFN_HIDDEN = 2816

COL_AX, COL_AC, COL_AB = 0, 1, 2
COL_DQ, COL_DK, COL_DV, COL_DZ = 3, 4, 5, 6
COL_MQ, COL_MK, COL_MV = 7, 8, 9
COL_GATE = 10
N_MAIN = 13 * D_MODEL
N_SMALL = 512
N_PROJ = N_MAIN + N_SMALL

MASK_NEG = -1e30
VMEM_LIMIT = 56 * 1024 * 1024
SUBLANES = 8
LANES = 128


def _cparams(*sem):
    return pltpu.CompilerParams(dimension_semantics=sem, vmem_limit_bytes=VMEM_LIMIT)


def _silu(x):
    return x * jax.nn.sigmoid(x)


def _dot(a, b):
    return jnp.dot(a, b, preferred_element_type=F32)


def _dot_nt(a, b):
    return lax.dot_general(a, b, (((1,), (1,)), ((), ())), preferred_element_type=F32)


def _dot_tn(a, b):
    return lax.dot_general(a, b, (((0,), (0,)), ((), ())), preferred_element_type=F32)


def _norm_matmul_kernel(x_ref, g_ref, w_ref, o_ref, h_ref):
    @pl.when(pl.program_id(1) == 0)
    def _():
        x = x_ref[...]
        ms = jnp.mean(x * x, axis=-1, keepdims=True)
        h_ref[...] = (x * lax.rsqrt(ms + NORM_EPS) * g_ref[...]).astype(BF16)

    o_ref[...] = _dot(h_ref[...], w_ref[...])


def _norm_matmul(x, gain, w, tm, tn):
    t, d = x.shape
    n = w.shape[1]
    return pl.pallas_call(
        _norm_matmul_kernel,
        grid=(t // tm, n // tn),
        in_specs=[
            pl.BlockSpec((tm, d), lambda i, j: (i, 0)),
            pl.BlockSpec((1, d), lambda i, j: (0, 0)),
            pl.BlockSpec((d, tn), lambda i, j: (0, j)),
        ],
        out_specs=pl.BlockSpec((tm, tn), lambda i, j: (i, j)),
        out_shape=jax.ShapeDtypeStruct((t, n), F32),
        scratch_shapes=[pltpu.VMEM((tm, d), BF16)],
        compiler_params=_cparams("parallel", "arbitrary"),
        name="in_proj",
    )(x, gain, w)


def _dn_gates_kernel(raw_ref, alog_ref, dt_ref, o_ref):
    raw = raw_ref[...]
    tm = raw.shape[0]
    xa = raw + dt_ref[...]
    softplus = jnp.maximum(xa, 0.0) + jnp.log1p(jnp.exp(-jnp.abs(xa)))
    g = -jnp.exp(alog_ref[...]) * softplus
    row = lax.broadcasted_iota(jnp.int32, (tm, LANES), 0) % DN_CHUNK
    shift = 1
    while shift < DN_CHUNK:
        g = g + jnp.where(row >= shift, pltpu.roll(g, shift, axis=0), 0.0)
        shift *= 2
    lane = lax.broadcasted_iota(jnp.int32, (tm, LANES), 1)
    o_ref[...] = jnp.where(lane < HEADS, jax.nn.sigmoid(raw), g)


def _dn_gates(proj, alog_row, dt_row, tm):
    t = proj.shape[0]
    return pl.pallas_call(
        _dn_gates_kernel,
        grid=(t // tm,),
        in_specs=[
            pl.BlockSpec((tm, LANES), lambda i: (i, N_MAIN // LANES)),
            pl.BlockSpec((1, LANES), lambda i: (0, 0)),
            pl.BlockSpec((1, LANES), lambda i: (0, 0)),
        ],
        out_specs=pl.BlockSpec((tm, LANES), lambda i: (i, 0)),
        out_shape=jax.ShapeDtypeStruct((t, LANES), F32),
        compiler_params=_cparams("parallel"),
        name="dn_gates",
    )(proj, alog_row, dt_row)


def _causal_conv(x, carry, w, k):
    xe = jnp.concatenate([carry, x], axis=0)
    y = x * w[k - 1:k]
    for j in range(1, k):
        y = y + pltpu.roll(xe, j, axis=0)[SUBLANES:] * w[k - 1 - j:k - j]
    return y


def _deltanet_kernel(q_ref, k_ref, v_ref, z_ref, gates_ref, grow_ref, cw_ref, nw_ref, o_ref,
                     carry_ref, state_ref, qn_ref, kn_ref, vv_ref):
    L = q_ref.shape[0]
    C = DN_CHUNK

    @pl.when(pl.program_id(1) == 0)
    def _():
        carry_ref[...] = jnp.zeros_like(carry_ref)
        state_ref[...] = jnp.zeros_like(state_ref)

    for idx, (src, dst) in enumerate(((q_ref, qn_ref), (k_ref, kn_ref), (v_ref, vv_ref))):
        cols = slice(idx * D_MODEL, (idx + 1) * D_MODEL)
        x = src[...]
        y = _silu(_causal_conv(x, carry_ref[:, cols], cw_ref[:, cols], DN_CONV_K))
        carry_ref[:, cols] = x[L - SUBLANES:]
        if idx == 2:
            dst[...] = y
        else:
            scale = HEAD_DIM ** -0.5 if idx == 0 else 1.0
            for h in range(HEADS):
                hs = slice(h * HEAD_DIM, (h + 1) * HEAD_DIM)
                yh = y[:, hs]
                inv = lax.rsqrt(jnp.sum(yh * yh, axis=-1, keepdims=True) + NORM_EPS)
                dst[:, hs] = yh * (inv * scale)

    ri = lax.broadcasted_iota(jnp.int32, (C, C), 0)
    ci = lax.broadcasted_iota(jnp.int32, (C, C), 1)
    causal = ci <= ri
    strict = ci < ri
    eye = (ci == ri).astype(F32)
    level_masks = []
    bs = 1
    while bs < C:
        same = (ri // (2 * bs)) == (ci // (2 * bs))
        level_masks.append(same & ((ri % (2 * bs)) >= bs) & ((ci % (2 * bs)) < bs))
        bs *= 2
    nw = nw_ref[...]

    def chunk_body(c, _):
        r = pl.ds(pl.multiple_of(c * C, C), C)
        gb = gates_ref[r, :]
        for h in range(HEADS):
            hs = slice(h * HEAD_DIM, (h + 1) * HEAD_DIM)
            q = qn_ref[r, hs]
            k = kn_ref[r, hs]
            v = vv_ref[r, hs]
            beta = gb[:, h:h + 1]
            gc = gb[:, HEADS + h:HEADS + h + 1]
            gr = grow_ref[0, h, pl.ds(c, 1), :]
            glast = gc[C - 1:C, :]
            decay = jnp.where(causal, jnp.exp(jnp.where(causal, gc - gr, 0.0)), 0.0)
            kb = k * beta
            a = jnp.where(strict, _dot_nt(kb, k) * decay, 0.0)
            t = eye - jnp.where(level_masks[0], a, 0.0)
            for m in level_masks[1:]:
                t = t - _dot(_dot(t, jnp.where(m, a, 0.0)), t)
            eg = jnp.exp(gc)
            uw = _dot(t, jnp.concatenate([v * beta, kb * eg], axis=1))
            u = uw[:, :HEAD_DIM]
            w = uw[:, HEAD_DIM:]
            qk = _dot_nt(q, k) * decay
            kd = k * jnp.exp(glast - gc)
            s = state_ref[h]
            ws = _dot(jnp.concatenate([w, q * eg], axis=0), s)
            v_new = u - ws[:C]
            o = ws[C:] + _dot(qk, v_new)
            state_ref[h] = s * jnp.exp(glast) + _dot_tn(kd, v_new)
            on = o * lax.rsqrt(jnp.mean(o * o, axis=-1, keepdims=True) + NORM_EPS) * nw
            o_ref[r, hs] = on * _silu(z_ref[r, hs])
        return 0

    lax.fori_loop(0, L // C, chunk_body, 0)


def _deltanet(proj, gates, g_row, conv_w, norm_w, batch, seq, L):
    t = proj.shape[0]
    nl = seq // L
    blk = lambda col: pl.BlockSpec((L, D_MODEL), lambda b, s, col=col: (b * nl + s, col))
    return pl.pallas_call(
        _deltanet_kernel,
        grid=(batch, nl),
        in_specs=[
            blk(COL_DQ), blk(COL_DK), blk(COL_DV), blk(COL_DZ),
            pl.BlockSpec((L, LANES), lambda b, s: (b * nl + s, 0)),
            pl.BlockSpec((1, HEADS, L // DN_CHUNK, DN_CHUNK), lambda b, s: (b, 0, s, 0)),
            pl.BlockSpec((DN_CONV_K, 3 * D_MODEL), lambda b, s: (0, 0)),
            pl.BlockSpec((1, HEAD_DIM), lambda b, s: (0, 0)),
        ],
        out_specs=pl.BlockSpec((L, D_MODEL), lambda b, s: (b * nl + s, 0)),
        out_shape=jax.ShapeDtypeStruct((t, D_MODEL), F32),
        scratch_shapes=[
            pltpu.VMEM((SUBLANES, 3 * D_MODEL), F32),
            pltpu.VMEM((HEADS, HEAD_DIM, HEAD_DIM), F32),
            pltpu.VMEM((L, D_MODEL), F32),
            pltpu.VMEM((L, D_MODEL), F32),
            pltpu.VMEM((L, D_MODEL), F32),
        ],
        compiler_params=_cparams("parallel", "arbitrary"),
        name="deltanet",
    )(proj, proj, proj, proj, gates, g_row, conv_w, norm_w)


def _moba_prep_kernel(q_ref, k_ref, v_ref, cos_ref, sin_ref, qa_ref, ka_ref, v16_ref, kmean_ref):
    i = pl.program_id(2)
    BS = MOBA_BLOCK

    @pl.when(i == 0)
    def _():
        kmean_ref[...] = jnp.zeros_like(kmean_ref)

    cos = cos_ref[...]
    sin = sin_ref[...]
    rope = lambda x: x * cos + pltpu.roll(x, HEAD_DIM // 2, axis=1) * sin
    q = rope(q_ref[...])
    k = rope(k_ref[...])

    lane = lax.broadcasted_iota(jnp.int32, (BS, LANES), 1)
    gate = lax.dot_general(q, kmean_ref[...], (((1,), (1,)), ((), ())),
                           precision=lax.Precision.HIGHEST, preferred_element_type=F32)
    neg_inf = jnp.float32(-jnp.inf)
    g = jnp.where(lane < i, gate, neg_inf)
    sel = lane == i
    lane_f = lane.astype(F32)
    for _ in range(MOBA_TOPK):
        m = jnp.max(g, axis=-1, keepdims=True)
        cand = jnp.where((g == m) & (g > neg_inf), lane_f, float(LANES))
        first = jnp.min(cand, axis=-1, keepdims=True)
        pick = lane_f == first
        sel = sel | pick
        g = jnp.where(pick, neg_inf, g)
    bias = jnp.where(sel, 0.0, MASK_NEG)

    qa_ref[0, 0, :, :HEAD_DIM] = (q * (HEAD_DIM ** -0.5)).astype(BF16)
    qa_ref[0, 0, :, HEAD_DIM:] = bias.astype(BF16)
    ka_ref[0, 0, :, :HEAD_DIM] = k.astype(BF16)
    ka_ref[0, 0, :, HEAD_DIM:] = (lane == i).astype(BF16)
    v16_ref[0, 0] = v_ref[...].astype(BF16)
    kmean_ref[pl.ds(i, 1), :] = jnp.mean(k, axis=0, keepdims=True)


def _moba_prep(proj, cos_full, sin_signed, batch, seq):
    nb = seq // MOBA_BLOCK
    cpb = D_MODEL // HEAD_DIM
    blk = lambda col: pl.BlockSpec((MOBA_BLOCK, HEAD_DIM),
                                   lambda b, h, i, col=col: (b * nb + i, col * cpb + h))
    tab = pl.BlockSpec((MOBA_BLOCK, HEAD_DIM), lambda b, h, i: (i, 0))
    aug = pl.BlockSpec((1, 1, MOBA_BLOCK, 2 * HEAD_DIM), lambda b, h, i: (b, h, i, 0))
    return pl.pallas_call(
        _moba_prep_kernel,
        grid=(batch, HEADS, nb),
        in_specs=[blk(COL_MQ), blk(COL_MK), blk(COL_MV), tab, tab],
        out_specs=[aug, aug,
                   pl.BlockSpec((1, 1, MOBA_BLOCK, HEAD_DIM), lambda b, h, i: (b, h, i, 0))],
        out_shape=[jax.ShapeDtypeStruct((batch, HEADS, seq, 2 * HEAD_DIM), BF16),
                   jax.ShapeDtypeStruct((batch, HEADS, seq, 2 * HEAD_DIM), BF16),
                   jax.ShapeDtypeStruct((batch, HEADS, seq, HEAD_DIM), BF16)],
        scratch_shapes=[pltpu.VMEM((LANES, HEAD_DIM), F32)],
        compiler_params=_cparams("parallel", "parallel", "arbitrary"),
        name="moba_prep",
    )(proj, proj, proj, cos_full, sin_signed)


def _moba_attn_kernel(qa_ref, ka_ref, v_ref, o_ref):
    i = pl.program_id(2)
    BS = MOBA_BLOCK
    q = qa_ref[0, 0]

    def scores(j):
        r = pl.ds(pl.multiple_of(j * BS, BS), BS)
        return _dot_nt(q, ka_ref[0, 0, r, :]), v_ref[0, 0, r, :]

    s, vj = scores(i)
    ri = lax.broadcasted_iota(jnp.int32, (BS, BS), 0)
    ci = lax.broadcasted_iota(jnp.int32, (BS, BS), 1)
    s = jnp.where(ci <= ri, s, MASK_NEG)
    m0 = jnp.max(s, axis=-1, keepdims=True)
    p = jnp.exp(s - m0)
    l0 = jnp.sum(p, axis=-1, keepdims=True)
    acc0 = _dot(p.astype(BF16), vj)

    def body(j, carry):
        m, l, acc = carry
        s, vj = scores(j)
        m_new = jnp.maximum(m, jnp.max(s, axis=-1, keepdims=True))
        alpha = jnp.exp(m - m_new)
        p = jnp.exp(s - m_new)
        l = alpha * l + jnp.sum(p, axis=-1, keepdims=True)
        acc = alpha * acc + _dot(p.astype(BF16), vj)
        return m_new, l, acc

    _, l, acc = lax.fori_loop(0, i, body, (m0, l0, acc0))
    o_ref[...] = acc / l


def _moba_attn(q_aug, k_aug, v16, batch, seq):
    nb = seq // MOBA_BLOCK
    return pl.pallas_call(
        _moba_attn_kernel,
        grid=(batch, HEADS, nb),
        in_specs=[
            pl.BlockSpec((1, 1, MOBA_BLOCK, 2 * HEAD_DIM), lambda b, h, i: (b, h, i, 0)),
            pl.BlockSpec((1, 1, seq, 2 * HEAD_DIM), lambda b, h, i: (b, h, 0, 0)),
            pl.BlockSpec((1, 1, seq, HEAD_DIM), lambda b, h, i: (b, h, 0, 0)),
        ],
        out_specs=pl.BlockSpec((MOBA_BLOCK, HEAD_DIM), lambda b, h, i: (b * nb + i, h)),
        out_shape=jax.ShapeDtypeStruct((batch * seq, D_MODEL), F32),
        compiler_params=_cparams("parallel", "parallel", "arbitrary"),
        name="moba_attn",
    )(q_aug, k_aug, v16)


def _merge_kernel(ax_ref, ac_ref, ab_ref, hx_ref, hc_ref, cw_ref, ydn_ref, ymb_ref,
                  ga_ref, gd_ref, gm_ref, x_ref, wa_ref, wd_ref, wm_ref, wo_ref, o_ref, *, seq):
    tm = ax_ref.shape[0]
    first = (pl.program_id(0) * tm) % seq == 0
    halo = jnp.where(first, 0.0, hx_ref[...] * hc_ref[...])
    p = ax_ref[...] * ac_ref[...]
    y_a = ab_ref[...] * _causal_conv(p, halo, cw_ref[...], CONV_A_K)
    merged = (jax.nn.sigmoid(ga_ref[...]) * _dot(y_a.astype(BF16), wa_ref[...])
              + jax.nn.sigmoid(gd_ref[...]) * _dot(ydn_ref[...].astype(BF16), wd_ref[...])
              + jax.nn.sigmoid(gm_ref[...]) * _dot(ymb_ref[...].astype(BF16), wm_ref[...]))
    o_ref[...] = x_ref[...] + _dot(merged.astype(BF16), wo_ref[...])


def _merge(proj, conv_w, y_dn, y_mb, x, wa, wd, wm, wo, seq, tm):
    t = x.shape[0]
    blk = lambda col: pl.BlockSpec((tm, D_MODEL), lambda i, col=col: (i, col))
    halo = lambda col: pl.BlockSpec(
        (SUBLANES, D_MODEL), lambda i, col=col: (jnp.maximum(i * (tm // SUBLANES) - 1, 0), col))
    row = pl.BlockSpec((tm, D_MODEL), lambda i: (i, 0))
    wspec = pl.BlockSpec((D_MODEL, D_MODEL), lambda i: (0, 0))
    return pl.pallas_call(
        functools.partial(_merge_kernel, seq=seq),
        grid=(t // tm,),
        in_specs=[blk(COL_AX), blk(COL_AC), blk(COL_AB), halo(COL_AX), halo(COL_AC),
                  pl.BlockSpec((CONV_A_K, D_MODEL), lambda i: (0, 0)),
                  row, row, blk(COL_GATE), blk(COL_GATE + 1), blk(COL_GATE + 2), row,
                  wspec, wspec, wspec, wspec],
        out_specs=row,
        out_shape=jax.ShapeDtypeStruct((t, D_MODEL), F32),
        compiler_params=_cparams("parallel"),
        name="merge",
    )(proj, proj, proj, proj, proj, conv_w, y_dn, y_mb, proj, proj, proj, x, wa, wd, wm, wo)


def _ffn_kernel(x_ref, g_ref, wg_ref, wu_ref, wd_ref, o_ref, h_ref, acc_ref):
    j = pl.program_id(1)

    @pl.when(j == 0)
    def _():
        x = x_ref[...]
        ms = jnp.mean(x * x, axis=-1, keepdims=True)
        h_ref[...] = (x * lax.rsqrt(ms + NORM_EPS) * g_ref[...]).astype(BF16)
        acc_ref[...] = x

    h = h_ref[...]
    act = _silu(_dot(h, wg_ref[...])) * _dot(h, wu_ref[...])
    acc_ref[...] += _dot(act.astype(BF16), wd_ref[...])

    @pl.when(j == pl.num_programs(1) - 1)
    def _():
        o_ref[...] = acc_ref[...]


def _ffn(x, gain, wg, wu, wd, tm, th):
    t, d = x.shape
    hid = wg.shape[1]
    return pl.pallas_call(
        _ffn_kernel,
        grid=(t // tm, hid // th),
        in_specs=[
            pl.BlockSpec((tm, d), lambda i, j: (i, 0)),
            pl.BlockSpec((1, d), lambda i, j: (0, 0)),
            pl.BlockSpec((d, th), lambda i, j: (0, j)),
            pl.BlockSpec((d, th), lambda i, j: (0, j)),
            pl.BlockSpec((th, d), lambda i, j: (j, 0)),
        ],
        out_specs=pl.BlockSpec((tm, d), lambda i, j: (i, 0)),
        out_shape=jax.ShapeDtypeStruct((t, d), F32),
        scratch_shapes=[pltpu.VMEM((tm, d), BF16), pltpu.VMEM((tm, d), F32)],
        compiler_params=_cparams("parallel", "arbitrary"),
        name="ffn",
    )(x, gain, wg, wu, wd)


def _final_norm_kernel(x_ref, g_ref, o_ref):
    x = x_ref[...]
    ms = jnp.mean(x * x, axis=-1, keepdims=True)
    o_ref[...] = x * lax.rsqrt(ms + NORM_EPS) * g_ref[...]


def _final_norm(x, gain, tm):
    t, d = x.shape
    return pl.pallas_call(
        _final_norm_kernel,
        grid=(t // tm,),
        in_specs=[pl.BlockSpec((tm, d), lambda i: (i, 0)), pl.BlockSpec((1, d), lambda i: (0, 0))],
        out_specs=pl.BlockSpec((tm, d), lambda i: (i, 0)),
        out_shape=jax.ShapeDtypeStruct((t, d), F32),
        compiler_params=_cparams("parallel"),
        name="final_norm",
    )(x, gain)


def _largest_tile(n, cap):
    t = cap
    while n % t:
        t //= 2
    return t


def _prep_w_in(w):
    d = D_MODEL
    o_dn_qkv = 3 * d
    o_dn_z = o_dn_qkv + 3 * d
    o_dn_b = o_dn_z + d
    o_dn_a = o_dn_b + HEADS
    o_mb = o_dn_a + HEADS
    o_gate = o_mb + 3 * d
    small = jnp.concatenate([w[:, o_dn_b:o_mb],
                             jnp.zeros((d, N_SMALL - 2 * HEADS), w.dtype)], axis=1)
    return jnp.concatenate([w[:, :o_dn_b], w[:, o_mb:o_gate + 3 * d], small], axis=1).astype(BF16)


def _lane_row(vals, offset):
    row = jnp.zeros((1, LANES), F32)
    return row.at[0, offset:offset + vals.shape[0]].set(vals.astype(F32))


def kernel(x, attn_norm, w_in, conv_a_w, dn_conv_w, dn_a_log, dn_dt_bias, dn_norm, w_br_a, w_br_dn,
           w_br_moba, w_out, ffn_norm, w_gate_up, w_down, final_norm):
    batch, seq, d = x.shape
    depth = attn_norm.shape[0]
    assert d == D_MODEL and seq % MOBA_BLOCK == 0 and seq // MOBA_BLOCK <= LANES
    t = batch * seq
    tm_big = _largest_tile(t, 1024)
    dn_len = _largest_tile(seq, 512)
    assert dn_len % (SUBLANES * DN_CHUNK) == 0

    inv = 1.0 / (ROPE_THETA ** (jnp.arange(0, HEAD_DIM, 2, dtype=F32) / HEAD_DIM))
    ang = jnp.arange(seq, dtype=F32)[:, None] * inv[None, :]
    cos_full = jnp.concatenate([jnp.cos(ang), jnp.cos(ang)], axis=-1)
    sin_signed = jnp.concatenate([-jnp.sin(ang), jnp.sin(ang)], axis=-1)

    xf = x.reshape(t, d)
    for l in range(depth):
        proj = _norm_matmul(xf, attn_norm[l][None, :], _prep_w_in(w_in[l]), tm_big, 512)

        gates = _dn_gates(proj, _lane_row(dn_a_log[l], HEADS), _lane_row(dn_dt_bias[l], HEADS), tm_big)
        g_row = (gates[:, HEADS:2 * HEADS].reshape(batch, seq, HEADS).transpose(0, 2, 1)
                 .reshape(batch, HEADS, seq // DN_CHUNK, DN_CHUNK))
        y_dn = _deltanet(proj, gates, g_row, dn_conv_w[l], dn_norm[l][None, :], batch, seq, dn_len)

        q_aug, k_aug, v16 = _moba_prep(proj, cos_full, sin_signed, batch, seq)
        y_mb = _moba_attn(q_aug, k_aug, v16, batch, seq)

        x1 = _merge(proj, conv_a_w[l], y_dn, y_mb, xf, w_br_a[l].astype(BF16),
                    w_br_dn[l].astype(BF16), w_br_moba[l].astype(BF16), w_out[l].astype(BF16),
                    seq, _largest_tile(t, 256))
        wgu = w_gate_up[l].astype(BF16)
        xf = _ffn(x1, ffn_norm[l][None, :], wgu[:, :FFN_HIDDEN], wgu[:, FFN_HIDDEN:],
                  w_down[l].astype(BF16), tm_big, 256)

    return _final_norm(xf, final_norm[None, :], tm_big).reshape(batch, seq, d)
```

```python
import functools

import jax
import jax.numpy as jnp
from jax import lax
from jax.experimental import pallas as pl
from jax.experimental.pallas import tpu as pltpu

F32 = jnp.float32
BF16 = jnp.bfloat16

D_MODEL = 1024
HEADS = 8
HEAD_DIM = 128
NORM_EPS = 1e-6
CONV_A_K = 3
DN_CONV_K = 4
DN_CHUNK = 64
DN_PAIR = 2 * DN_CHUNK
MOBA_BLOCK = 256
MOBA_TOPK = 3
MOBA_GROUP = 2
LOG2_E = 1.4426950408889634
ROPE_THETA = 10000.0
FFN_HIDDEN = 2816

COL_AX, COL_AC, COL_AB = 0, 1, 2
COL_DQ, COL_DK, COL_DV, COL_DZ = 3, 4, 5, 6
COL_MQ, COL_MK, COL_MV = 7, 8, 9
COL_GATE = 10
N_MAIN = 13 * D_MODEL
N_SMALL = 512
N_PROJ = N_MAIN + N_SMALL

MASK_NEG = -1e30
VMEM_LIMIT = 56 * 1024 * 1024
SUBLANES = 8
LANES = 128


def _cparams(*sem):
    return pltpu.CompilerParams(dimension_semantics=sem, vmem_limit_bytes=VMEM_LIMIT)


def _silu(x):
    return x * jax.nn.sigmoid(x)


def _dot(a, b):
    return jnp.dot(a, b, preferred_element_type=F32)


def _dot_nt(a, b):
    return lax.dot_general(a, b, (((1,), (1,)), ((), ())), preferred_element_type=F32)


def _dot_tn(a, b):
    return lax.dot_general(a, b, (((0,), (0,)), ((), ())), preferred_element_type=F32)


def _norm_matmul_kernel(x_ref, g_ref, w_ref, o_ref, h_ref):
    @pl.when(pl.program_id(1) == 0)
    def _():
        x = x_ref[...]
        ms = jnp.mean(x * x, axis=-1, keepdims=True)
        h_ref[...] = (x * lax.rsqrt(ms + NORM_EPS) * g_ref[...]).astype(BF16)

    o_ref[...] = _dot(h_ref[...], w_ref[...])


def _norm_matmul(x, gain, w, tm, tn):
    t, d = x.shape
    n = w.shape[1]
    return pl.pallas_call(
        _norm_matmul_kernel,
        grid=(t // tm, n // tn),
        in_specs=[
            pl.BlockSpec((tm, d), lambda i, j: (i, 0)),
            pl.BlockSpec((1, d), lambda i, j: (0, 0)),
            pl.BlockSpec((d, tn), lambda i, j: (0, j)),
        ],
        out_specs=pl.BlockSpec((tm, tn), lambda i, j: (i, j)),
        out_shape=jax.ShapeDtypeStruct((t, n), F32),
        scratch_shapes=[pltpu.VMEM((tm, d), BF16)],
        compiler_params=_cparams("parallel", "arbitrary"),
        name="in_proj",
    )(x, gain, w)


def _dn_gates_kernel(raw_ref, alog_ref, dt_ref, o_ref):
    raw = raw_ref[...]
    tm = raw.shape[0]
    xa = raw + dt_ref[...]
    softplus = jnp.maximum(xa, 0.0) + jnp.log1p(jnp.exp(-jnp.abs(xa)))
    g = -jnp.exp(alog_ref[...]) * softplus
    row = lax.broadcasted_iota(jnp.int32, (tm, LANES), 0) % DN_CHUNK
    shift = 1
    while shift < DN_CHUNK:
        g = g + jnp.where(row >= shift, pltpu.roll(g, shift, axis=0), 0.0)
        shift *= 2
    lane = lax.broadcasted_iota(jnp.int32, (tm, LANES), 1)
    o_ref[...] = jnp.where(lane < HEADS, jax.nn.sigmoid(raw), g)


def _dn_gates(proj, alog_row, dt_row, tm):
    t = proj.shape[0]
    return pl.pallas_call(
        _dn_gates_kernel,
        grid=(t // tm,),
        in_specs=[
            pl.BlockSpec((tm, LANES), lambda i: (i, N_MAIN // LANES)),
            pl.BlockSpec((1, LANES), lambda i: (0, 0)),
            pl.BlockSpec((1, LANES), lambda i: (0, 0)),
        ],
        out_specs=pl.BlockSpec((tm, LANES), lambda i: (i, 0)),
        out_shape=jax.ShapeDtypeStruct((t, LANES), F32),
        compiler_params=_cparams("parallel"),
        name="dn_gates",
    )(proj, alog_row, dt_row)


def _causal_conv(x, carry, w, k):
    xe = jnp.concatenate([carry, x], axis=0)
    y = x * w[k - 1:k]
    for j in range(1, k):
        y = y + pltpu.roll(xe, j, axis=0)[SUBLANES:] * w[k - 1 - j:k - j]
    return y


def _head_cols(h):
    return slice(h * HEAD_DIM, (h + 1) * HEAD_DIM)


def _deltanet_kernel(q_ref, k_ref, v_ref, z_ref, gates_ref, grow_ref, cw_ref, nw_ref, o_ref,
                     carry_ref, state_ref, qn_ref, kn_ref, vv_ref,
                     u_ref, w_ref, qd_ref, kd_ref, qk_ref):
    L = q_ref.shape[0]
    C = DN_CHUNK
    P = DN_PAIR
    hrange = range(HEADS)

    @pl.when(pl.program_id(1) == 0)
    def _():
        carry_ref[...] = jnp.zeros_like(carry_ref)
        state_ref[...] = jnp.zeros_like(state_ref)

    for idx, (src, dst) in enumerate(((q_ref, qn_ref), (k_ref, kn_ref), (v_ref, vv_ref))):
        cols = slice(idx * D_MODEL, (idx + 1) * D_MODEL)
        x = src[...]
        y = _silu(_causal_conv(x, carry_ref[:, cols], cw_ref[:, cols], DN_CONV_K))
        carry_ref[:, cols] = x[L - SUBLANES:]
        if idx == 2:
            dst[...] = y
        else:
            scale = HEAD_DIM ** -0.5 if idx == 0 else 1.0
            for h in hrange:
                yh = y[:, _head_cols(h)]
                inv = lax.rsqrt(jnp.sum(yh * yh, axis=-1, keepdims=True) + NORM_EPS)
                dst[:, _head_cols(h)] = yh * (inv * scale)

    ri = lax.broadcasted_iota(jnp.int32, (P, P), 0)
    ci = lax.broadcasted_iota(jnp.int32, (P, P), 1)
    same_chunk = (ri // C) == (ci // C)
    causal = same_chunk & (ci <= ri)
    strict = same_chunk & (ci < ri)
    eye = (ci == ri).astype(F32)
    level_masks = []
    bs = 1
    while bs < C:
        same = (ri // (2 * bs)) == (ci // (2 * bs))
        level_masks.append(same & ((ri % (2 * bs)) >= bs) & ((ci % (2 * bs)) < bs))
        bs *= 2
    first_half = lax.broadcasted_iota(jnp.int32, (P, 1), 0) < C
    pair0 = pl.program_id(1) * (L // P)

    def phase_a(sc, _):
        r = pl.ds(pl.multiple_of(sc * P, P), P)
        gb = gates_ref[r, :]
        q = [qn_ref[r, _head_cols(h)] for h in hrange]
        k = [kn_ref[r, _head_cols(h)] for h in hrange]
        v = [vv_ref[r, _head_cols(h)] for h in hrange]
        beta = [gb[:, h:h + 1] for h in hrange]
        gc = [gb[:, HEADS + h:HEADS + h + 1] for h in hrange]
        gr = [grow_ref[0, h, pl.ds(pair0 + sc, 1), :] for h in hrange]
        decay = [jnp.where(causal, jnp.exp(jnp.where(causal, gc[h] - gr[h], 0.0)), 0.0) for h in hrange]
        kb = [k[h] * beta[h] for h in hrange]
        k16 = [k[h].astype(BF16) for h in hrange]
        a = [jnp.where(strict, _dot_nt(kb[h].astype(BF16), k16[h]) * decay[h], 0.0) for h in hrange]
        t = [eye - jnp.where(level_masks[0], a[h], 0.0) for h in hrange]
        for m in level_masks[1:]:
            t16 = [t[h].astype(BF16) for h in hrange]
            tx = [_dot(t16[h], jnp.where(m, a[h], 0.0).astype(BF16)) for h in hrange]
            t = [t[h] - _dot(tx[h].astype(BF16), t16[h]) for h in hrange]
        eg = [jnp.exp(gc[h]) for h in hrange]
        rhs = [jnp.concatenate([v[h] * beta[h], kb[h] * eg[h]], axis=1).astype(BF16) for h in hrange]
        uw = [_dot(t[h].astype(BF16), rhs[h]) for h in hrange]
        qk = [(_dot_nt(q[h].astype(BF16), k16[h]) * decay[h]).astype(BF16) for h in hrange]
        for h in hrange:
            glast = jnp.where(first_half, gc[h][C - 1:C, :], gc[h][P - 1:P, :])
            u_ref[r, _head_cols(h)] = uw[h][:, :HEAD_DIM]
            w_ref[r, _head_cols(h)] = uw[h][:, HEAD_DIM:].astype(BF16)
            qd_ref[r, _head_cols(h)] = (q[h] * eg[h]).astype(BF16)
            kd_ref[r, _head_cols(h)] = (k[h] * jnp.exp(glast - gc[h])).astype(BF16)
            qk_ref[pl.ds(pl.multiple_of(sc * P, P), C), h * C:(h + 1) * C] = qk[h][:C, :C]
            qk_ref[pl.ds(pl.multiple_of(sc * P + C, C), C), h * C:(h + 1) * C] = qk[h][C:, C:]
        return 0

    lax.fori_loop(0, L // P, phase_a, 0)

    nw = nw_ref[...]

    def phase_b(c, _):
        r = pl.ds(pl.multiple_of(c * C, C), C)
        gb = gates_ref[r, :]
        s = [state_ref[h] for h in hrange]
        lhs = [jnp.concatenate([w_ref[r, _head_cols(h)], qd_ref[r, _head_cols(h)]], axis=0) for h in hrange]
        ws = [_dot(lhs[h], s[h].astype(BF16)) for h in hrange]
        v_new = [u_ref[r, _head_cols(h)] - ws[h][:C] for h in hrange]
        vn16 = [v_new[h].astype(BF16) for h in hrange]
        o = [ws[h][C:] + _dot(qk_ref[r, h * C:(h + 1) * C], vn16[h]) for h in hrange]
        for h in hrange:
            cd = jnp.exp(gb[C - 1:C, HEADS + h:HEADS + h + 1])
            state_ref[h] = s[h] * cd + _dot_tn(kd_ref[r, _head_cols(h)], vn16[h])
        for h in hrange:
            on = o[h] * lax.rsqrt(jnp.mean(o[h] * o[h], axis=-1, keepdims=True) + NORM_EPS) * nw
            o_ref[r, _head_cols(h)] = on * _silu(z_ref[r, _head_cols(h)])
        return 0

    lax.fori_loop(0, L // C, phase_b, 0)


def _deltanet(proj, gates, g_row, conv_w, norm_w, batch, seq, L):
    t = proj.shape[0]
    nl = seq // L
    blk = lambda col: pl.BlockSpec((L, D_MODEL), lambda b, s, col=col: (b * nl + s, col))
    return pl.pallas_call(
        _deltanet_kernel,
        grid=(batch, nl),
        in_specs=[
            blk(COL_DQ), blk(COL_DK), blk(COL_DV), blk(COL_DZ),
            pl.BlockSpec((L, LANES), lambda b, s: (b * nl + s, 0)),
            pl.BlockSpec((1, HEADS, seq // DN_PAIR, DN_PAIR), lambda b, s: (b, 0, 0, 0)),
            pl.BlockSpec((DN_CONV_K, 3 * D_MODEL), lambda b, s: (0, 0)),
            pl.BlockSpec((1, HEAD_DIM), lambda b, s: (0, 0)),
        ],
        out_specs=pl.BlockSpec((L, D_MODEL), lambda b, s: (b * nl + s, 0)),
        out_shape=jax.ShapeDtypeStruct((t, D_MODEL), F32),
        scratch_shapes=[
            pltpu.VMEM((SUBLANES, 3 * D_MODEL), F32),
            pltpu.VMEM((HEADS, HEAD_DIM, HEAD_DIM), F32),
            pltpu.VMEM((L, D_MODEL), F32),
            pltpu.VMEM((L, D_MODEL), F32),
            pltpu.VMEM((L, D_MODEL), F32),
            pltpu.VMEM((L, D_MODEL), F32),
            pltpu.VMEM((L, D_MODEL), BF16),
            pltpu.VMEM((L, D_MODEL), BF16),
            pltpu.VMEM((L, D_MODEL), BF16),
            pltpu.VMEM((L, HEADS * DN_CHUNK), BF16),
        ],
        compiler_params=_cparams("parallel", "arbitrary"),
        name="deltanet",
    )(proj, proj, proj, proj, gates, g_row, conv_w, norm_w)


def _moba_prep_kernel(q_ref, k_ref, v_ref, cos_ref, sin_ref, qa_ref, ka_ref, vt_ref, kmean_ref):
    i = pl.program_id(2)
    BS = MOBA_BLOCK

    @pl.when(i == 0)
    def _():
        kmean_ref[...] = jnp.zeros_like(kmean_ref)

    cos = cos_ref[...]
    sin = sin_ref[...]
    rope = lambda x: x * cos + pltpu.roll(x, HEAD_DIM // 2, axis=1) * sin
    q = rope(q_ref[...])
    k = rope(k_ref[...])

    lane = lax.broadcasted_iota(jnp.int32, (BS, LANES), 1)
    gate = lax.dot_general(q, kmean_ref[...], (((1,), (1,)), ((), ())),
                           precision=lax.Precision.HIGHEST, preferred_element_type=F32)
    neg_inf = jnp.float32(-jnp.inf)
    g = jnp.where(lane < i, gate, neg_inf)
    sel = lane == i
    lane_f = lane.astype(F32)
    for _ in range(MOBA_TOPK):
        m = jnp.max(g, axis=-1, keepdims=True)
        cand = jnp.where((g == m) & (g > neg_inf), lane_f, float(LANES))
        first = jnp.min(cand, axis=-1, keepdims=True)
        pick = lane_f == first
        sel = sel | pick
        g = jnp.where(pick, neg_inf, g)
    bias = jnp.where(sel, 0.0, MASK_NEG)

    qa_ref[0, 0, :HEAD_DIM, :] = (q * (HEAD_DIM ** -0.5 * LOG2_E)).T.astype(BF16)
    qa_ref[0, 0, HEAD_DIM:, :] = bias.T.astype(BF16)
    ka_ref[0, 0, :, :HEAD_DIM] = k.astype(BF16)
    ka_ref[0, 0, :, HEAD_DIM:] = (lane == i).astype(BF16)
    vt_ref[0, 0, 0] = v_ref[...].T.astype(BF16)
    kmean_ref[pl.ds(i, 1), :] = jnp.mean(k, axis=0, keepdims=True)


def _moba_prep(proj, cos_full, sin_signed, batch, seq):
    nb = seq // MOBA_BLOCK
    cpb = D_MODEL // HEAD_DIM
    blk = lambda col: pl.BlockSpec((MOBA_BLOCK, HEAD_DIM),
                                   lambda b, h, i, col=col: (b * nb + i, col * cpb + h))
    tab = pl.BlockSpec((MOBA_BLOCK, HEAD_DIM), lambda b, h, i: (i, 0))
    return pl.pallas_call(
        _moba_prep_kernel,
        grid=(batch, HEADS, nb),
        in_specs=[blk(COL_MQ), blk(COL_MK), blk(COL_MV), tab, tab],
        out_specs=[pl.BlockSpec((1, 1, 2 * HEAD_DIM, MOBA_BLOCK), lambda b, h, i: (b, h, 0, i)),
                   pl.BlockSpec((1, 1, MOBA_BLOCK, 2 * HEAD_DIM), lambda b, h, i: (b, h, i, 0)),
                   pl.BlockSpec((1, 1, 1, HEAD_DIM, MOBA_BLOCK),
                                lambda b, h, i: (b, h, i // MOBA_GROUP, 0, i % MOBA_GROUP))],
        out_shape=[jax.ShapeDtypeStruct((batch, HEADS, 2 * HEAD_DIM, seq), BF16),
                   jax.ShapeDtypeStruct((batch, HEADS, seq, 2 * HEAD_DIM), BF16),
                   jax.ShapeDtypeStruct((batch, HEADS, nb // MOBA_GROUP, HEAD_DIM,
                                         MOBA_GROUP * MOBA_BLOCK), BF16)],
        scratch_shapes=[pltpu.VMEM((LANES, HEAD_DIM), F32)],
        compiler_params=_cparams("parallel", "parallel", "arbitrary"),
        name="moba_prep",
    )(proj, proj, proj, cos_full, sin_signed)


def _moba_attn_kernel(qa_ref, ka_ref, vt_ref, o_ref, sa_ref, sb_ref):
    t = pl.program_id(2)
    TK = MOBA_GROUP * MOBA_BLOCK
    qt = qa_ref[0, 0]

    def scores(j):
        return _dot(ka_ref[0, 0, pl.ds(pl.multiple_of(j * TK, TK), TK), :], qt)

    def attend(s, j, m, l, acc):
        m_new = jnp.maximum(m, jnp.max(s, axis=0, keepdims=True))
        alpha = jnp.exp2(m - m_new)
        p = jnp.exp2(s - m_new)
        l = alpha * l + jnp.sum(p, axis=0, keepdims=True)
        acc = alpha * acc + _dot(vt_ref[0, 0, j], p.astype(BF16))
        return m_new, l, acc

    ki = lax.broadcasted_iota(jnp.int32, (TK, TK), 0)
    qi = lax.broadcasted_iota(jnp.int32, (TK, TK), 1)
    s_own = jnp.where(ki <= qi, scores(t), MASK_NEG)
    carry = attend(s_own, t, jnp.full((1, TK), MASK_NEG, F32), jnp.zeros((1, TK), F32),
                   jnp.zeros((HEAD_DIM, TK), F32))

    @pl.when(t > 0)
    def _():
        sa_ref[...] = scores(0)

    def step(g, cur_ref, nxt_ref, carry):
        nxt_ref[...] = scores(jnp.minimum(g + 1, t - 1))
        return attend(cur_ref[...], g, *carry)

    def two_steps(h, carry):
        carry = step(2 * h, sa_ref, sb_ref, carry)
        return step(2 * h + 1, sb_ref, sa_ref, carry)

    carry = lax.fori_loop(0, t // 2, two_steps, carry)
    _, l, acc = lax.fori_loop(0, t % 2, lambda _, c: attend(sa_ref[...], t - 1, *c), carry)
    o_ref[...] = (acc / l).T


def _moba_attn(q_aug_t, k_aug, v_t, batch, seq):
    tk = MOBA_GROUP * MOBA_BLOCK
    nt = seq // tk
    return pl.pallas_call(
        _moba_attn_kernel,
        grid=(batch, HEADS, nt),
        in_specs=[
            pl.BlockSpec((1, 1, 2 * HEAD_DIM, tk), lambda b, h, i: (b, h, 0, i)),
            pl.BlockSpec((1, 1, seq, 2 * HEAD_DIM), lambda b, h, i: (b, h, 0, 0)),
            pl.BlockSpec((1, 1, nt, HEAD_DIM, tk), lambda b, h, i: (b, h, 0, 0, 0)),
        ],
        out_specs=pl.BlockSpec((tk, HEAD_DIM), lambda b, h, i: (b * nt + i, h)),
        out_shape=jax.ShapeDtypeStruct((batch * seq, D_MODEL), F32),
        scratch_shapes=[pltpu.VMEM((tk, tk), F32), pltpu.VMEM((tk, tk), F32)],
        compiler_params=_cparams("parallel", "parallel", "arbitrary"),
        name="moba_attn",
    )(q_aug_t, k_aug, v_t)


def _merge_kernel(ax_ref, ac_ref, ab_ref, hx_ref, hc_ref, cw_ref, ydn_ref, ymb_ref,
                  ga_ref, gd_ref, gm_ref, x_ref, wa_ref, wd_ref, wm_ref, wo_ref, o_ref, *, seq):
    tm = ax_ref.shape[0]
    first = (pl.program_id(0) * tm) % seq == 0
    halo = jnp.where(first, 0.0, hx_ref[...] * hc_ref[...])
    p = ax_ref[...] * ac_ref[...]
    y_a = ab_ref[...] * _causal_conv(p, halo, cw_ref[...], CONV_A_K)
    merged = (jax.nn.sigmoid(ga_ref[...]) * _dot(y_a.astype(BF16), wa_ref[...])
              + jax.nn.sigmoid(gd_ref[...]) * _dot(ydn_ref[...].astype(BF16), wd_ref[...])
              + jax.nn.sigmoid(gm_ref[...]) * _dot(ymb_ref[...].astype(BF16), wm_ref[...]))
    o_ref[...] = x_ref[...] + _dot(merged.astype(BF16), wo_ref[...])


def _merge(proj, conv_w, y_dn, y_mb, x, wa, wd, wm, wo, seq, tm):
    t = x.shape[0]
    blk = lambda col: pl.BlockSpec((tm, D_MODEL), lambda i, col=col: (i, col))
    halo = lambda col: pl.BlockSpec(
        (SUBLANES, D_MODEL), lambda i, col=col: (jnp.maximum(i * (tm // SUBLANES) - 1, 0), col))
    row = pl.BlockSpec((tm, D_MODEL), lambda i: (i, 0))
    wspec = pl.BlockSpec((D_MODEL, D_MODEL), lambda i: (0, 0))
    return pl.pallas_call(
        functools.partial(_merge_kernel, seq=seq),
        grid=(t // tm,),
        in_specs=[blk(COL_AX), blk(COL_AC), blk(COL_AB), halo(COL_AX), halo(COL_AC),
                  pl.BlockSpec((CONV_A_K, D_MODEL), lambda i: (0, 0)),
                  row, row, blk(COL_GATE), blk(COL_GATE + 1), blk(COL_GATE + 2), row,
                  wspec, wspec, wspec, wspec],
        out_specs=row,
        out_shape=jax.ShapeDtypeStruct((t, D_MODEL), F32),
        compiler_params=_cparams("parallel"),
        name="merge",
    )(proj, proj, proj, proj, proj, conv_w, y_dn, y_mb, proj, proj, proj, x, wa, wd, wm, wo)


def _ffn_kernel(x_ref, g_ref, wg_ref, wu_ref, wd_ref, o_ref, h_ref, acc_ref):
    j = pl.program_id(1)

    @pl.when(j == 0)
    def _():
        x = x_ref[...]
        ms = jnp.mean(x * x, axis=-1, keepdims=True)
        h_ref[...] = (x * lax.rsqrt(ms + NORM_EPS) * g_ref[...]).astype(BF16)
        acc_ref[...] = x

    h = h_ref[...]
    act = _silu(_dot(h, wg_ref[...])) * _dot(h, wu_ref[...])
    acc_ref[...] += _dot(act.astype(BF16), wd_ref[...])

    @pl.when(j == pl.num_programs(1) - 1)
    def _():
        o_ref[...] = acc_ref[...]


def _ffn(x, gain, wg, wu, wd, tm, th):
    t, d = x.shape
    hid = wg.shape[1]
    return pl.pallas_call(
        _ffn_kernel,
        grid=(t // tm, hid // th),
        in_specs=[
            pl.BlockSpec((tm, d), lambda i, j: (i, 0)),
            pl.BlockSpec((1, d), lambda i, j: (0, 0)),
            pl.BlockSpec((d, th), lambda i, j: (0, j)),
            pl.BlockSpec((d, th), lambda i, j: (0, j)),
            pl.BlockSpec((th, d), lambda i, j: (j, 0)),
        ],
        out_specs=pl.BlockSpec((tm, d), lambda i, j: (i, 0)),
        out_shape=jax.ShapeDtypeStruct((t, d), F32),
        scratch_shapes=[pltpu.VMEM((tm, d), BF16), pltpu.VMEM((tm, d), F32)],
        compiler_params=_cparams("parallel", "arbitrary"),
        name="ffn",
    )(x, gain, wg, wu, wd)


def _final_norm_kernel(x_ref, g_ref, o_ref):
    x = x_ref[...]
    ms = jnp.mean(x * x, axis=-1, keepdims=True)
    o_ref[...] = x * lax.rsqrt(ms + NORM_EPS) * g_ref[...]


def _final_norm(x, gain, tm):
    t, d = x.shape
    return pl.pallas_call(
        _final_norm_kernel,
        grid=(t // tm,),
        in_specs=[pl.BlockSpec((tm, d), lambda i: (i, 0)), pl.BlockSpec((1, d), lambda i: (0, 0))],
        out_specs=pl.BlockSpec((tm, d), lambda i: (i, 0)),
        out_shape=jax.ShapeDtypeStruct((t, d), F32),
        compiler_params=_cparams("parallel"),
        name="final_norm",
    )(x, gain)


def _largest_tile(n, cap):
    t = cap
    while n % t:
        t //= 2
    return t


def _prep_w_in(w):
    d = D_MODEL
    o_dn_qkv = 3 * d
    o_dn_z = o_dn_qkv + 3 * d
    o_dn_b = o_dn_z + d
    o_dn_a = o_dn_b + HEADS
    o_mb = o_dn_a + HEADS
    o_gate = o_mb + 3 * d
    small = jnp.concatenate([w[:, o_dn_b:o_mb],
                             jnp.zeros((d, N_SMALL - 2 * HEADS), w.dtype)], axis=1)
    return jnp.concatenate([w[:, :o_dn_b], w[:, o_mb:o_gate + 3 * d], small], axis=1).astype(BF16)


def _lane_row(vals, offset):
    row = jnp.zeros((1, LANES), F32)
    return row.at[0, offset:offset + vals.shape[0]].set(vals.astype(F32))


def kernel(x, attn_norm, w_in, conv_a_w, dn_conv_w, dn_a_log, dn_dt_bias, dn_norm, w_br_a, w_br_dn,
           w_br_moba, w_out, ffn_norm, w_gate_up, w_down, final_norm):
    batch, seq, d = x.shape
    depth = attn_norm.shape[0]
    assert d == D_MODEL and seq % (MOBA_GROUP * MOBA_BLOCK) == 0 and seq // MOBA_BLOCK <= LANES
    t = batch * seq
    tm_big = _largest_tile(t, 1024)
    dn_len = _largest_tile(seq, 512)
    assert dn_len % DN_PAIR == 0 and (seq // DN_PAIR) % SUBLANES == 0

    inv = 1.0 / (ROPE_THETA ** (jnp.arange(0, HEAD_DIM, 2, dtype=F32) / HEAD_DIM))
    ang = jnp.arange(seq, dtype=F32)[:, None] * inv[None, :]
    cos_full = jnp.concatenate([jnp.cos(ang), jnp.cos(ang)], axis=-1)
    sin_signed = jnp.concatenate([-jnp.sin(ang), jnp.sin(ang)], axis=-1)

    xf = x.reshape(t, d)
    for l in range(depth):
        proj = _norm_matmul(xf, attn_norm[l][None, :], _prep_w_in(w_in[l]), tm_big, 512)

        gates = _dn_gates(proj, _lane_row(dn_a_log[l], HEADS), _lane_row(dn_dt_bias[l], HEADS), tm_big)
        g_row = (gates[:, HEADS:2 * HEADS].reshape(batch, seq, HEADS).transpose(0, 2, 1)
                 .reshape(batch, HEADS, seq // DN_PAIR, DN_PAIR))
        y_dn = _deltanet(proj, gates, g_row, dn_conv_w[l], dn_norm[l][None, :], batch, seq, dn_len)

        q_aug_t, k_aug, v_t = _moba_prep(proj, cos_full, sin_signed, batch, seq)
        y_mb = _moba_attn(q_aug_t, k_aug, v_t, batch, seq)

        x1 = _merge(proj, conv_a_w[l], y_dn, y_mb, xf, w_br_a[l].astype(BF16),
                    w_br_dn[l].astype(BF16), w_br_moba[l].astype(BF16), w_out[l].astype(BF16),
                    seq, _largest_tile(t, 256))
        wgu = w_gate_up[l].astype(BF16)
        xf = _ffn(x1, ffn_norm[l][None, :], wgu[:, :FFN_HIDDEN], wgu[:, FFN_HIDDEN:],
                  w_down[l].astype(BF16), tm_big, 256)

    return _final_norm(xf, final_norm[None, :], tm_big).reshape(batch, seq, d)
```

```python
import functools

import jax
import jax.numpy as jnp
from jax import lax
from jax.experimental import pallas as pl
from jax.experimental.pallas import tpu as pltpu

F32 = jnp.float32
BF16 = jnp.bfloat16

D_MODEL = 1024
HEADS = 8
HEAD_DIM = 128
NORM_EPS = 1e-6
CONV_A_K = 3
DN_CONV_K = 4
DN_CHUNK = 64
DN_PAIR = 2 * DN_CHUNK
MOBA_BLOCK = 256
MOBA_TOPK = 3
MOBA_GROUP = 2
LOG2_E = 1.4426950408889634
ROPE_THETA = 10000.0
FFN_HIDDEN = 2816

COL_AX, COL_AC, COL_AB = 0, 1, 2
COL_DQ, COL_DK, COL_DV, COL_DZ = 3, 4, 5, 6
COL_MQ, COL_MK, COL_MV = 7, 8, 9
COL_GATE = 10
N_MAIN = 13 * D_MODEL
N_SMALL = 128

MASK_NEG = -1e30
VMEM_LIMIT = 56 * 1024 * 1024
SUBLANES = 8
LANES = 128
HALO_ROWS = 16


def _cparams(*sem):
    return pltpu.CompilerParams(dimension_semantics=sem, vmem_limit_bytes=VMEM_LIMIT)


def _silu(x):
    return x * jax.nn.sigmoid(x)


def _dot(a, b):
    return jnp.dot(a, b, preferred_element_type=F32)


def _dot_nt(a, b):
    return lax.dot_general(a, b, (((1,), (1,)), ((), ())), preferred_element_type=F32)


def _dot_tn(a, b):
    return lax.dot_general(a, b, (((0,), (0,)), ((), ())), preferred_element_type=F32)


def _norm_matmul_kernel(x_ref, g_ref, w_ref, o_ref, h_ref):
    @pl.when(pl.program_id(1) == 0)
    def _():
        x = x_ref[...]
        ms = jnp.mean(x * x, axis=-1, keepdims=True)
        h_ref[...] = (x * lax.rsqrt(ms + NORM_EPS) * g_ref[...]).astype(BF16)

    o_ref[...] = _dot(h_ref[...], w_ref[...]).astype(o_ref.dtype)


def _norm_matmul(x, gain, w, tm, tn, out_dtype):
    t, d = x.shape
    n = w.shape[1]
    return pl.pallas_call(
        _norm_matmul_kernel,
        grid=(t // tm, n // tn),
        in_specs=[
            pl.BlockSpec((tm, d), lambda i, j: (i, 0)),
            pl.BlockSpec((1, d), lambda i, j: (0, 0)),
            pl.BlockSpec((d, tn), lambda i, j: (0, j)),
        ],
        out_specs=pl.BlockSpec((tm, tn), lambda i, j: (i, j)),
        out_shape=jax.ShapeDtypeStruct((t, n), out_dtype),
        scratch_shapes=[pltpu.VMEM((tm, d), BF16)],
        compiler_params=_cparams("parallel", "arbitrary"),
        name="in_proj",
    )(x, gain, w)


def _dn_gates_kernel(raw_ref, alog_ref, dt_ref, o_ref):
    raw = raw_ref[...]
    tm = raw.shape[0]
    xa = raw + dt_ref[...]
    softplus = jnp.maximum(xa, 0.0) + jnp.log1p(jnp.exp(-jnp.abs(xa)))
    g = -jnp.exp(alog_ref[...]) * softplus
    row = lax.broadcasted_iota(jnp.int32, (tm, LANES), 0) % DN_CHUNK
    shift = 1
    while shift < DN_CHUNK:
        g = g + jnp.where(row >= shift, pltpu.roll(g, shift, axis=0), 0.0)
        shift *= 2
    lane = lax.broadcasted_iota(jnp.int32, (tm, LANES), 1)
    o_ref[...] = jnp.where(lane < HEADS, jax.nn.sigmoid(raw), g)


def _dn_gates(raw, alog_row, dt_row, tm):
    t = raw.shape[0]
    return pl.pallas_call(
        _dn_gates_kernel,
        grid=(t // tm,),
        in_specs=[
            pl.BlockSpec((tm, LANES), lambda i: (i, 0)),
            pl.BlockSpec((1, LANES), lambda i: (0, 0)),
            pl.BlockSpec((1, LANES), lambda i: (0, 0)),
        ],
        out_specs=pl.BlockSpec((tm, LANES), lambda i: (i, 0)),
        out_shape=jax.ShapeDtypeStruct((t, LANES), F32),
        compiler_params=_cparams("parallel"),
        name="dn_gates",
    )(raw, alog_row, dt_row)


def _causal_conv(x, carry, w, k):
    xe = jnp.concatenate([carry, x], axis=0)
    y = x * w[k - 1:k]
    for j in range(1, k):
        y = y + pltpu.roll(xe, j, axis=0)[SUBLANES:] * w[k - 1 - j:k - j]
    return y


def _head_cols(h):
    return slice(h * HEAD_DIM, (h + 1) * HEAD_DIM)


def _deltanet_kernel(q_ref, k_ref, v_ref, z_ref, gates_ref, grow_ref, cw_ref, nw_ref, o_ref,
                     carry_ref, state_ref, qn_ref, kn_ref, vv_ref,
                     u_ref, w_ref, qd_ref, kd_ref, qk_ref, xs_ref):
    L = q_ref.shape[0]
    C = DN_CHUNK
    P = DN_PAIR
    hrange = range(HEADS)

    @pl.when(pl.program_id(1) == 0)
    def _():
        carry_ref[...] = jnp.zeros_like(carry_ref)
        state_ref[...] = jnp.zeros_like(state_ref)

    for idx, (src, dst) in enumerate(((q_ref, qn_ref), (k_ref, kn_ref), (v_ref, vv_ref))):
        cols = slice(idx * D_MODEL, (idx + 1) * D_MODEL)
        xs_ref[:SUBLANES, :] = carry_ref[:, cols]
        xs_ref[SUBLANES:, :] = src[...].astype(F32)
        carry_ref[:, cols] = xs_ref[L:, :]
        scale = HEAD_DIM ** -0.5 if idx == 0 else 1.0
        for h in hrange:
            hs = _head_cols(h)
            w = cw_ref[:, idx * D_MODEL + h * HEAD_DIM:idx * D_MODEL + (h + 1) * HEAD_DIM]
            y = xs_ref[SUBLANES:, hs] * w[DN_CONV_K - 1:DN_CONV_K]
            for j in range(1, DN_CONV_K):
                y = y + xs_ref[SUBLANES - j:SUBLANES - j + L, hs] * w[DN_CONV_K - 1 - j:DN_CONV_K - j]
            y = _silu(y)
            if idx < 2:
                y = y * (lax.rsqrt(jnp.sum(y * y, axis=-1, keepdims=True) + NORM_EPS) * scale)
            dst[:, hs] = y

    ri = lax.broadcasted_iota(jnp.int32, (P, P), 0)
    ci = lax.broadcasted_iota(jnp.int32, (P, P), 1)
    same_chunk = (ri // C) == (ci // C)
    causal = same_chunk & (ci <= ri)
    strict = same_chunk & (ci < ri)
    eye = (ci == ri).astype(F32)
    level_masks = []
    bs = 1
    while bs < C:
        same = (ri // (2 * bs)) == (ci // (2 * bs))
        level_masks.append(same & ((ri % (2 * bs)) >= bs) & ((ci % (2 * bs)) < bs))
        bs *= 2
    first_half = lax.broadcasted_iota(jnp.int32, (P, 1), 0) < C
    pair0 = pl.program_id(1) * (L // P)

    def phase_a(sc, _):
        r = pl.ds(pl.multiple_of(sc * P, P), P)
        gb = gates_ref[r, :]
        q = [qn_ref[r, _head_cols(h)] for h in hrange]
        k = [kn_ref[r, _head_cols(h)] for h in hrange]
        v = [vv_ref[r, _head_cols(h)] for h in hrange]
        beta = [gb[:, h:h + 1] for h in hrange]
        gc = [gb[:, HEADS + h:HEADS + h + 1] for h in hrange]
        gr = [grow_ref[0, h, pl.ds(pair0 + sc, 1), :] for h in hrange]
        decay = [jnp.where(causal, jnp.exp(jnp.where(causal, gc[h] - gr[h], 0.0)), 0.0) for h in hrange]
        kb = [k[h] * beta[h] for h in hrange]
        k16 = [k[h].astype(BF16) for h in hrange]
        a = [jnp.where(strict, _dot_nt(kb[h].astype(BF16), k16[h]) * decay[h], 0.0) for h in hrange]
        t = [eye - jnp.where(level_masks[0], a[h], 0.0) for h in hrange]
        for m in level_masks[1:]:
            t16 = [t[h].astype(BF16) for h in hrange]
            tx = [_dot(t16[h], jnp.where(m, a[h], 0.0).astype(BF16)) for h in hrange]
            t = [t[h] - _dot(tx[h].astype(BF16), t16[h]) for h in hrange]
        eg = [jnp.exp(gc[h]) for h in hrange]
        rhs = [jnp.concatenate([v[h] * beta[h], kb[h] * eg[h]], axis=1).astype(BF16) for h in hrange]
        uw = [_dot(t[h].astype(BF16), rhs[h]) for h in hrange]
        qk = [(_dot_nt(q[h].astype(BF16), k16[h]) * decay[h]).astype(BF16) for h in hrange]
        for h in hrange:
            glast = jnp.where(first_half, gc[h][C - 1:C, :], gc[h][P - 1:P, :])
            u_ref[r, _head_cols(h)] = uw[h][:, :HEAD_DIM]
            w_ref[r, _head_cols(h)] = uw[h][:, HEAD_DIM:].astype(BF16)
            qd_ref[r, _head_cols(h)] = (q[h] * eg[h]).astype(BF16)
            kd_ref[r, _head_cols(h)] = (k[h] * jnp.exp(glast - gc[h])).astype(BF16)
            qk_ref[pl.ds(pl.multiple_of(sc * P, P), C), h * C:(h + 1) * C] = qk[h][:C, :C]
            qk_ref[pl.ds(pl.multiple_of(sc * P + C, C), C), h * C:(h + 1) * C] = qk[h][C:, C:]
        return 0

    lax.fori_loop(0, L // P, phase_a, 0)

    nw = nw_ref[...]

    def phase_b(c, _):
        r = pl.ds(pl.multiple_of(c * C, C), C)
        gb = gates_ref[r, :]
        s = [state_ref[h] for h in hrange]
        lhs = [jnp.concatenate([w_ref[r, _head_cols(h)], qd_ref[r, _head_cols(h)]], axis=0) for h in hrange]
        ws = [_dot(lhs[h], s[h].astype(BF16)) for h in hrange]
        v_new = [u_ref[r, _head_cols(h)] - ws[h][:C] for h in hrange]
        vn16 = [v_new[h].astype(BF16) for h in hrange]
        o = [ws[h][C:] + _dot(qk_ref[r, h * C:(h + 1) * C], vn16[h]) for h in hrange]
        for h in hrange:
            cd = jnp.exp(gb[C - 1:C, HEADS + h:HEADS + h + 1])
            state_ref[h] = s[h] * cd + _dot_tn(kd_ref[r, _head_cols(h)], vn16[h])
        for h in hrange:
            on = o[h] * lax.rsqrt(jnp.mean(o[h] * o[h], axis=-1, keepdims=True) + NORM_EPS) * nw
            o_ref[r, _head_cols(h)] = (on * _silu(z_ref[r, _head_cols(h)].astype(F32))).astype(BF16)
        return 0

    lax.fori_loop(0, L // C, phase_b, 0)


def _deltanet(proj, gates, g_row, conv_w, norm_w, batch, seq, L):
    t = proj.shape[0]
    nl = seq // L
    blk = lambda col: pl.BlockSpec((L, D_MODEL), lambda b, s, col=col: (b * nl + s, col))
    return pl.pallas_call(
        _deltanet_kernel,
        grid=(batch, nl),
        in_specs=[
            blk(COL_DQ), blk(COL_DK), blk(COL_DV), blk(COL_DZ),
            pl.BlockSpec((L, LANES), lambda b, s: (b * nl + s, 0)),
            pl.BlockSpec((1, HEADS, seq // DN_PAIR, DN_PAIR), lambda b, s: (b, 0, 0, 0)),
            pl.BlockSpec((DN_CONV_K, 3 * D_MODEL), lambda b, s: (0, 0)),
            pl.BlockSpec((1, HEAD_DIM), lambda b, s: (0, 0)),
        ],
        out_specs=pl.BlockSpec((L, D_MODEL), lambda b, s: (b * nl + s, 0)),
        out_shape=jax.ShapeDtypeStruct((t, D_MODEL), BF16),
        scratch_shapes=[
            pltpu.VMEM((SUBLANES, 3 * D_MODEL), F32),
            pltpu.VMEM((HEADS, HEAD_DIM, HEAD_DIM), F32),
            pltpu.VMEM((L, D_MODEL), F32),
            pltpu.VMEM((L, D_MODEL), F32),
            pltpu.VMEM((L, D_MODEL), F32),
            pltpu.VMEM((L, D_MODEL), F32),
            pltpu.VMEM((L, D_MODEL), BF16),
            pltpu.VMEM((L, D_MODEL), BF16),
            pltpu.VMEM((L, D_MODEL), BF16),
            pltpu.VMEM((L, HEADS * DN_CHUNK), BF16),
            pltpu.VMEM((L + SUBLANES, D_MODEL), F32),
        ],
        compiler_params=_cparams("parallel", "arbitrary"),
        name="deltanet",
    )(proj, proj, proj, proj, gates, g_row, conv_w, norm_w)


def _moba_prep_kernel(q_ref, k_ref, v_ref, cos_ref, sin_ref, qa_ref, ka_ref, vt_ref, kmean_ref):
    i = pl.program_id(1)
    BS = MOBA_BLOCK
    nbp = kmean_ref.shape[1]
    hrange = range(HEADS)

    @pl.when(i == 0)
    def _():
        kmean_ref[...] = jnp.zeros_like(kmean_ref)

    cos = cos_ref[...]
    sin = sin_ref[...]
    rope = lambda x: x * cos + pltpu.roll(x, HEAD_DIM // 2, axis=1) * sin
    q = [rope(q_ref[:, _head_cols(h)].astype(F32)) for h in hrange]
    k = [rope(k_ref[:, _head_cols(h)].astype(F32)) for h in hrange]

    gate = [lax.dot_general(kmean_ref[h], q[h], (((1,), (1,)), ((), ())),
                            precision=lax.Precision.HIGHEST, preferred_element_type=F32)
            for h in hrange]
    blk = lax.broadcasted_iota(jnp.int32, (nbp, BS), 0)
    blk_f = blk.astype(F32)
    neg_inf = jnp.float32(-jnp.inf)
    g = [jnp.where(blk < i, gate[h], neg_inf) for h in hrange]
    sel = [blk == i for h in hrange]
    for _ in range(MOBA_TOPK):
        m = [jnp.max(g[h], axis=0, keepdims=True) for h in hrange]
        first = [jnp.min(jnp.where((g[h] == m[h]) & (g[h] > neg_inf), blk_f, float(nbp)),
                         axis=0, keepdims=True) for h in hrange]
        pick = [blk_f == first[h] for h in hrange]
        sel = [sel[h] | pick[h] for h in hrange]
        g = [jnp.where(pick[h], neg_inf, g[h]) for h in hrange]

    lane = lax.broadcasted_iota(jnp.int32, (BS, LANES), 1)
    onehot = (lane == i).astype(BF16)
    qt = [(q[h] * (HEAD_DIM ** -0.5 * LOG2_E)).T.astype(BF16) for h in hrange]
    vt = [v_ref[:, _head_cols(h)].astype(F32).T.astype(BF16) for h in hrange]
    for h in hrange:
        qa_ref[0, h, :HEAD_DIM, :] = qt[h]
        qa_ref[0, h, HEAD_DIM:HEAD_DIM + nbp, :] = jnp.where(sel[h], 0.0, MASK_NEG).astype(BF16)
        if nbp < HEAD_DIM:
            qa_ref[0, h, HEAD_DIM + nbp:, :] = jnp.zeros((HEAD_DIM - nbp, BS), BF16)
        ka_ref[0, h, :, :HEAD_DIM] = k[h].astype(BF16)
        ka_ref[0, h, :, HEAD_DIM:] = onehot
        vt_ref[0, h, 0] = vt[h]
        kmean_ref[h, pl.ds(i, 1), :] = jnp.mean(k[h], axis=0, keepdims=True)


def _moba_prep(proj, cos_full, sin_signed, batch, seq):
    nb = seq // MOBA_BLOCK
    nbp = -(-nb // 16) * 16
    blk = lambda col: pl.BlockSpec((MOBA_BLOCK, D_MODEL), lambda b, i, col=col: (b * nb + i, col))
    tab = pl.BlockSpec((MOBA_BLOCK, HEAD_DIM), lambda b, i: (i, 0))
    return pl.pallas_call(
        _moba_prep_kernel,
        grid=(batch, nb),
        in_specs=[blk(COL_MQ), blk(COL_MK), blk(COL_MV), tab, tab],
        out_specs=[pl.BlockSpec((1, HEADS, 2 * HEAD_DIM, MOBA_BLOCK), lambda b, i: (b, 0, 0, i)),
                   pl.BlockSpec((1, HEADS, MOBA_BLOCK, 2 * HEAD_DIM), lambda b, i: (b, 0, i, 0)),
                   pl.BlockSpec((1, HEADS, 1, HEAD_DIM, MOBA_BLOCK),
                                lambda b, i: (b, 0, i // MOBA_GROUP, 0, i % MOBA_GROUP))],
        out_shape=[jax.ShapeDtypeStruct((batch, HEADS, 2 * HEAD_DIM, seq), BF16),
                   jax.ShapeDtypeStruct((batch, HEADS, seq, 2 * HEAD_DIM), BF16),
                   jax.ShapeDtypeStruct((batch, HEADS, nb // MOBA_GROUP, HEAD_DIM,
                                         MOBA_GROUP * MOBA_BLOCK), BF16)],
        scratch_shapes=[pltpu.VMEM((HEADS, nbp, HEAD_DIM), F32)],
        compiler_params=_cparams("parallel", "arbitrary"),
        name="moba_prep",
    )(proj, proj, proj, cos_full, sin_signed)


def _moba_attn_kernel(qa_ref, ka_ref, vt_ref, o_ref, sa_ref, sb_ref):
    t = pl.program_id(2)
    TK = MOBA_GROUP * MOBA_BLOCK
    qt = qa_ref[0, 0]

    def scores(j):
        return _dot(ka_ref[0, 0, pl.ds(pl.multiple_of(j * TK, TK), TK), :], qt)

    def attend(s, j, m, l, acc):
        m_new = jnp.maximum(m, jnp.max(s, axis=0, keepdims=True))
        alpha = jnp.exp2(m - m_new)
        p = jnp.exp2(s - m_new)
        l = alpha * l + jnp.sum(p, axis=0, keepdims=True)
        acc = alpha * acc + _dot(vt_ref[0, 0, j], p.astype(BF16))
        return m_new, l, acc

    ki = lax.broadcasted_iota(jnp.int32, (TK, TK), 0)
    qi = lax.broadcasted_iota(jnp.int32, (TK, TK), 1)
    s_own = jnp.where(ki <= qi, scores(t), MASK_NEG)
    carry = attend(s_own, t, jnp.full((1, TK), MASK_NEG, F32), jnp.zeros((1, TK), F32),
                   jnp.zeros((HEAD_DIM, TK), F32))

    @pl.when(t > 0)
    def _():
        sa_ref[...] = scores(0)

    def step(g, cur_ref, nxt_ref, carry):
        nxt_ref[...] = scores(jnp.minimum(g + 1, t - 1))
        return attend(cur_ref[...], g, *carry)

    def two_steps(h, carry):
        carry = step(2 * h, sa_ref, sb_ref, carry)
        return step(2 * h + 1, sb_ref, sa_ref, carry)

    carry = lax.fori_loop(0, t // 2, two_steps, carry)
    _, l, acc = lax.fori_loop(0, t % 2, lambda _, c: attend(sa_ref[...], t - 1, *c), carry)
    o_ref[...] = (acc / l).T.astype(BF16)


def _moba_attn(q_aug_t, k_aug, v_t, batch, seq):
    tk = MOBA_GROUP * MOBA_BLOCK
    nt = seq // tk
    return pl.pallas_call(
        _moba_attn_kernel,
        grid=(batch, HEADS, nt),
        in_specs=[
            pl.BlockSpec((1, 1, 2 * HEAD_DIM, tk), lambda b, h, i: (b, h, 0, i)),
            pl.BlockSpec((1, 1, seq, 2 * HEAD_DIM), lambda b, h, i: (b, h, 0, 0)),
            pl.BlockSpec((1, 1, nt, HEAD_DIM, tk), lambda b, h, i: (b, h, 0, 0, 0)),
        ],
        out_specs=pl.BlockSpec((tk, HEAD_DIM), lambda b, h, i: (b * nt + i, h)),
        out_shape=jax.ShapeDtypeStruct((batch * seq, D_MODEL), BF16),
        scratch_shapes=[pltpu.VMEM((tk, tk), F32), pltpu.VMEM((tk, tk), F32)],
        compiler_params=_cparams("parallel", "parallel", "arbitrary"),
        name="moba_attn",
    )(q_aug_t, k_aug, v_t)


def _merge_kernel(ax_ref, ac_ref, ab_ref, hx_ref, hc_ref, cw_ref, ydn_ref, ymb_ref,
                  ga_ref, gd_ref, gm_ref, x_ref, wa_ref, wd_ref, wm_ref, wo_ref, o_ref, *, seq):
    tm = ax_ref.shape[0]
    f32 = lambda ref: ref[...].astype(F32)
    first = (pl.program_id(0) * tm) % seq == 0
    halo = jnp.where(first, 0.0, (f32(hx_ref) * f32(hc_ref))[HALO_ROWS - SUBLANES:])
    p = f32(ax_ref) * f32(ac_ref)
    y_a = f32(ab_ref) * _causal_conv(p, halo, cw_ref[...], CONV_A_K)
    merged = (jax.nn.sigmoid(f32(ga_ref)) * _dot(y_a.astype(BF16), wa_ref[...])
              + jax.nn.sigmoid(f32(gd_ref)) * _dot(ydn_ref[...], wd_ref[...])
              + jax.nn.sigmoid(f32(gm_ref)) * _dot(ymb_ref[...], wm_ref[...]))
    o_ref[...] = x_ref[...] + _dot(merged.astype(BF16), wo_ref[...])


def _merge(proj, conv_w, y_dn, y_mb, x, wa, wd, wm, wo, seq, tm):
    t = x.shape[0]
    blk = lambda col: pl.BlockSpec((tm, D_MODEL), lambda i, col=col: (i, col))
    halo = lambda col: pl.BlockSpec(
        (HALO_ROWS, D_MODEL), lambda i, col=col: (jnp.maximum(i * (tm // HALO_ROWS) - 1, 0), col))
    row = pl.BlockSpec((tm, D_MODEL), lambda i: (i, 0))
    wspec = pl.BlockSpec((D_MODEL, D_MODEL), lambda i: (0, 0))
    return pl.pallas_call(
        functools.partial(_merge_kernel, seq=seq),
        grid=(t // tm,),
        in_specs=[blk(COL_AX), blk(COL_AC), blk(COL_AB), halo(COL_AX), halo(COL_AC),
                  pl.BlockSpec((CONV_A_K, D_MODEL), lambda i: (0, 0)),
                  row, row, blk(COL_GATE), blk(COL_GATE + 1), blk(COL_GATE + 2), row,
                  wspec, wspec, wspec, wspec],
        out_specs=row,
        out_shape=jax.ShapeDtypeStruct((t, D_MODEL), F32),
        compiler_params=_cparams("parallel"),
        name="merge",
    )(proj, proj, proj, proj, proj, conv_w, y_dn, y_mb, proj, proj, proj, x, wa, wd, wm, wo)


def _ffn_kernel(x_ref, g_ref, wg_ref, wu_ref, wd_ref, o_ref, h_ref, acc_ref):
    j = pl.program_id(1)

    @pl.when(j == 0)
    def _():
        x = x_ref[...]
        ms = jnp.mean(x * x, axis=-1, keepdims=True)
        h_ref[...] = (x * lax.rsqrt(ms + NORM_EPS) * g_ref[...]).astype(BF16)
        acc_ref[...] = x

    h = h_ref[...]
    act = _silu(_dot(h, wg_ref[...])) * _dot(h, wu_ref[...])
    acc_ref[...] += _dot(act.astype(BF16), wd_ref[...])

    @pl.when(j == pl.num_programs(1) - 1)
    def _():
        o_ref[...] = acc_ref[...]


def _ffn(x, gain, wg, wu, wd, tm, th):
    t, d = x.shape
    hid = wg.shape[1]
    return pl.pallas_call(
        _ffn_kernel,
        grid=(t // tm, hid // th),
        in_specs=[
            pl.BlockSpec((tm, d), lambda i, j: (i, 0)),
            pl.BlockSpec((1, d), lambda i, j: (0, 0)),
            pl.BlockSpec((d, th), lambda i, j: (0, j)),
            pl.BlockSpec((d, th), lambda i, j: (0, j)),
            pl.BlockSpec((th, d), lambda i, j: (j, 0)),
        ],
        out_specs=pl.BlockSpec((tm, d), lambda i, j: (i, 0)),
        out_shape=jax.ShapeDtypeStruct((t, d), F32),
        scratch_shapes=[pltpu.VMEM((tm, d), BF16), pltpu.VMEM((tm, d), F32)],
        compiler_params=_cparams("parallel", "arbitrary"),
        name="ffn",
    )(x, gain, wg, wu, wd)


def _final_norm_kernel(x_ref, g_ref, o_ref):
    x = x_ref[...]
    ms = jnp.mean(x * x, axis=-1, keepdims=True)
    o_ref[...] = x * lax.rsqrt(ms + NORM_EPS) * g_ref[...]


def _final_norm(x, gain, tm):
    t, d = x.shape
    return pl.pallas_call(
        _final_norm_kernel,
        grid=(t // tm,),
        in_specs=[pl.BlockSpec((tm, d), lambda i: (i, 0)), pl.BlockSpec((1, d), lambda i: (0, 0))],
        out_specs=pl.BlockSpec((tm, d), lambda i: (i, 0)),
        out_shape=jax.ShapeDtypeStruct((t, d), F32),
        compiler_params=_cparams("parallel"),
        name="final_norm",
    )(x, gain)


def _largest_tile(n, cap):
    t = cap
    while n % t:
        t //= 2
    return t


def _prep_w_in(w):
    d = D_MODEL
    o_dn_qkv = 3 * d
    o_dn_z = o_dn_qkv + 3 * d
    o_dn_b = o_dn_z + d
    o_dn_a = o_dn_b + HEADS
    o_mb = o_dn_a + HEADS
    o_gate = o_mb + 3 * d
    small = jnp.concatenate([w[:, o_dn_b:o_mb],
                             jnp.zeros((d, N_SMALL - 2 * HEADS), w.dtype)], axis=1)
    main = jnp.concatenate([w[:, :o_dn_b].astype(BF16), w[:, o_mb:o_gate + 3 * d].astype(BF16)], axis=1)
    return main, small.astype(BF16)


def _lane_row(vals, offset):
    row = jnp.zeros((1, LANES), F32)
    return row.at[0, offset:offset + vals.shape[0]].set(vals.astype(F32))


def kernel(x, attn_norm, w_in, conv_a_w, dn_conv_w, dn_a_log, dn_dt_bias, dn_norm, w_br_a, w_br_dn,
           w_br_moba, w_out, ffn_norm, w_gate_up, w_down, final_norm):
    batch, seq, d = x.shape
    depth = attn_norm.shape[0]
    assert d == D_MODEL and seq % (MOBA_GROUP * MOBA_BLOCK) == 0 and seq // MOBA_BLOCK <= LANES
    t = batch * seq
    tm_big = _largest_tile(t, 1024)
    dn_len = _largest_tile(seq, 512)
    assert dn_len % DN_PAIR == 0 and (seq // DN_PAIR) % SUBLANES == 0

    inv = 1.0 / (ROPE_THETA ** (jnp.arange(0, HEAD_DIM, 2, dtype=F32) / HEAD_DIM))
    ang = jnp.arange(seq, dtype=F32)[:, None] * inv[None, :]
    cos_full = jnp.concatenate([jnp.cos(ang), jnp.cos(ang)], axis=-1)
    sin_signed = jnp.concatenate([-jnp.sin(ang), jnp.sin(ang)], axis=-1)

    xf = x.reshape(t, d)
    for l in range(depth):
        w_main, w_small = _prep_w_in(w_in[l])
        gain = attn_norm[l][None, :]
        proj = _norm_matmul(xf, gain, w_main, _largest_tile(t, 2048), 512, BF16)
        raw_gates = _norm_matmul(xf, gain, w_small, tm_big, N_SMALL, F32)

        gates = _dn_gates(raw_gates, _lane_row(dn_a_log[l], HEADS), _lane_row(dn_dt_bias[l], HEADS), tm_big)
        g_row = (gates[:, HEADS:2 * HEADS].reshape(batch, seq, HEADS).transpose(0, 2, 1)
                 .reshape(batch, HEADS, seq // DN_PAIR, DN_PAIR))
        y_dn = _deltanet(proj, gates, g_row, dn_conv_w[l], dn_norm[l][None, :], batch, seq, dn_len)

        q_aug_t, k_aug, v_t = _moba_prep(proj, cos_full, sin_signed, batch, seq)
        y_mb = _moba_attn(q_aug_t, k_aug, v_t, batch, seq)

        x1 = _merge(proj, conv_a_w[l], y_dn, y_mb, xf, w_br_a[l].astype(BF16),
                    w_br_dn[l].astype(BF16), w_br_moba[l].astype(BF16), w_out[l].astype(BF16),
                    seq, _largest_tile(t, 512))
        wgu = w_gate_up[l].astype(BF16)
        xf = _ffn(x1, ffn_norm[l][None, :], wgu[:, :FFN_HIDDEN], wgu[:, FFN_HIDDEN:],
                  w_down[l].astype(BF16), tm_big, 256)

    return _final_norm(xf, final_norm[None, :], tm_big).reshape(batch, seq, d)
```

```python
import functools

import jax
import jax.numpy as jnp
from jax import lax
from jax.experimental import pallas as pl
from jax.experimental.pallas import tpu as pltpu

F32 = jnp.float32
BF16 = jnp.bfloat16

D_MODEL = 1024
HEADS = 8
HEAD_DIM = 128
NORM_EPS = 1e-6
CONV_A_K = 3
DN_CONV_K = 4
DN_CHUNK = 64
DN_PAIR = 2 * DN_CHUNK
DN_PAIRS_PER_ITER = 4
MOBA_BLOCK = 256
MOBA_TOPK = 3
MOBA_GROUP = 2
MOBA_QTILE_KEYS = 2
LOG2_E = 1.4426950408889634
ROPE_THETA = 10000.0
FFN_HIDDEN = 2816

COL_AX, COL_AC, COL_AB = 0, 1, 2
COL_DQ, COL_DK, COL_DV, COL_DZ = 3, 4, 5, 6
COL_MQ, COL_MK, COL_MV = 7, 8, 9
COL_GATE = 10
N_MAIN = 13 * D_MODEL
N_SMALL = 128

MASK_NEG = -1e30
VMEM_LIMIT = 56 * 1024 * 1024
SUBLANES = 8
LANES = 128
HALO_ROWS = 16


def _cparams(*sem):
    return pltpu.CompilerParams(dimension_semantics=sem, vmem_limit_bytes=VMEM_LIMIT)


def _silu(x):
    return x * jax.nn.sigmoid(x)


def _dot(a, b):
    return jnp.dot(a, b, preferred_element_type=F32)


def _dot_nt(a, b):
    return lax.dot_general(a, b, (((1,), (1,)), ((), ())), preferred_element_type=F32)


def _dot_tn(a, b):
    return lax.dot_general(a, b, (((0,), (0,)), ((), ())), preferred_element_type=F32)


def _norm_matmul_kernel(x_ref, g_ref, w_ref, o_ref, h_ref):
    @pl.when(pl.program_id(1) == 0)
    def _():
        x = x_ref[...]
        ms = jnp.mean(x * x, axis=-1, keepdims=True)
        h_ref[...] = (x * lax.rsqrt(ms + NORM_EPS) * g_ref[...]).astype(BF16)

    o_ref[...] = _dot(h_ref[...], w_ref[...]).astype(o_ref.dtype)


def _norm_matmul(x, gain, w, tm, tn, out_dtype):
    t, d = x.shape
    n = w.shape[1]
    return pl.pallas_call(
        _norm_matmul_kernel,
        grid=(t // tm, n // tn),
        in_specs=[
            pl.BlockSpec((tm, d), lambda i, j: (i, 0)),
            pl.BlockSpec((1, d), lambda i, j: (0, 0)),
            pl.BlockSpec((d, tn), lambda i, j: (0, j)),
        ],
        out_specs=pl.BlockSpec((tm, tn), lambda i, j: (i, j)),
        out_shape=jax.ShapeDtypeStruct((t, n), out_dtype),
        scratch_shapes=[pltpu.VMEM((tm, d), BF16)],
        compiler_params=_cparams("parallel", "arbitrary"),
        name="in_proj",
    )(x, gain, w)


def _dn_gates_kernel(raw_ref, alog_ref, dt_ref, o_ref):
    raw = raw_ref[...]
    tm = raw.shape[0]
    xa = raw + dt_ref[...]
    softplus = jnp.maximum(xa, 0.0) + jnp.log1p(jnp.exp(-jnp.abs(xa)))
    g = -jnp.exp(alog_ref[...]) * softplus
    row = lax.broadcasted_iota(jnp.int32, (tm, LANES), 0) % DN_CHUNK
    shift = 1
    while shift < DN_CHUNK:
        g = g + jnp.where(row >= shift, pltpu.roll(g, shift, axis=0), 0.0)
        shift *= 2
    lane = lax.broadcasted_iota(jnp.int32, (tm, LANES), 1)
    o_ref[...] = jnp.where(lane < HEADS, jax.nn.sigmoid(raw), g)


def _dn_gates(raw, alog_row, dt_row, tm):
    t = raw.shape[0]
    return pl.pallas_call(
        _dn_gates_kernel,
        grid=(t // tm,),
        in_specs=[
            pl.BlockSpec((tm, LANES), lambda i: (i, 0)),
            pl.BlockSpec((1, LANES), lambda i: (0, 0)),
            pl.BlockSpec((1, LANES), lambda i: (0, 0)),
        ],
        out_specs=pl.BlockSpec((tm, LANES), lambda i: (i, 0)),
        out_shape=jax.ShapeDtypeStruct((t, LANES), F32),
        compiler_params=_cparams("parallel"),
        name="dn_gates",
    )(raw, alog_row, dt_row)


def _causal_conv(x, carry, w, k):
    xe = jnp.concatenate([carry, x], axis=0)
    y = x * w[k - 1:k]
    for j in range(1, k):
        y = y + pltpu.roll(xe, j, axis=0)[SUBLANES:] * w[k - 1 - j:k - j]
    return y


def _head_cols(h):
    return slice(h * HEAD_DIM, (h + 1) * HEAD_DIM)


def _deltanet_kernel(q_ref, k_ref, v_ref, z_ref, gates_ref, grow_ref, cw_ref, nw_ref, o_ref,
                     carry_ref, state_ref, qn_ref, kn_ref, vv_ref,
                     u_ref, w_ref, qd_ref, kd_ref, qk_ref, xs_ref):
    L = q_ref.shape[0]
    C = DN_CHUNK
    P = DN_PAIR
    hrange = range(HEADS)

    @pl.when(pl.program_id(1) == 0)
    def _():
        carry_ref[...] = jnp.zeros_like(carry_ref)
        state_ref[...] = jnp.zeros_like(state_ref)

    for idx, (src, dst) in enumerate(((q_ref, qn_ref), (k_ref, kn_ref), (v_ref, vv_ref))):
        cols = slice(idx * D_MODEL, (idx + 1) * D_MODEL)
        xs_ref[:SUBLANES, :] = carry_ref[:, cols]
        xs_ref[SUBLANES:, :] = src[...].astype(F32)
        carry_ref[:, cols] = xs_ref[L:, :]
        scale = HEAD_DIM ** -0.5 if idx == 0 else 1.0
        for h in hrange:
            hs = _head_cols(h)
            w = cw_ref[:, idx * D_MODEL + h * HEAD_DIM:idx * D_MODEL + (h + 1) * HEAD_DIM]
            y = xs_ref[SUBLANES:, hs] * w[DN_CONV_K - 1:DN_CONV_K]
            for j in range(1, DN_CONV_K):
                y = y + xs_ref[SUBLANES - j:SUBLANES - j + L, hs] * w[DN_CONV_K - 1 - j:DN_CONV_K - j]
            y = _silu(y)
            if idx < 2:
                y = y * (lax.rsqrt(jnp.sum(y * y, axis=-1, keepdims=True) + NORM_EPS) * scale)
            dst[:, hs] = y

    ri = lax.broadcasted_iota(jnp.int32, (P, P), 0)
    ci = lax.broadcasted_iota(jnp.int32, (P, P), 1)
    same_chunk = (ri // C) == (ci // C)
    causal = same_chunk & (ci <= ri)
    strict = same_chunk & (ci < ri)
    eye = (ci == ri).astype(F32)
    level_masks = []
    bs = 1
    while bs < C:
        same = (ri // (2 * bs)) == (ci // (2 * bs))
        level_masks.append(same & ((ri % (2 * bs)) >= bs) & ((ci % (2 * bs)) < bs))
        bs *= 2
    first_half = lax.broadcasted_iota(jnp.int32, (P, 1), 0) < C
    pair0 = pl.program_id(1) * (L // P)

    def phase_a(it, _):
        pairs = [it * DN_PAIRS_PER_ITER + s for s in range(DN_PAIRS_PER_ITER)]
        rows = [pl.ds(pl.multiple_of(sc * P, P), P) for sc in pairs]
        gbs = [gates_ref[r, :] for r in rows]
        probs = [(s, h) for s in range(DN_PAIRS_PER_ITER) for h in hrange]
        n = range(len(probs))
        q = [qn_ref[rows[s], _head_cols(h)] for s, h in probs]
        k = [kn_ref[rows[s], _head_cols(h)] for s, h in probs]
        v = [vv_ref[rows[s], _head_cols(h)] for s, h in probs]
        beta = [gbs[s][:, h:h + 1] for s, h in probs]
        gc = [gbs[s][:, HEADS + h:HEADS + h + 1] for s, h in probs]
        gr = [grow_ref[0, h, pl.ds(pair0 + pairs[s], 1), :] for s, h in probs]
        decay = [jnp.where(causal, jnp.exp(jnp.where(causal, gc[i] - gr[i], 0.0)), 0.0) for i in n]
        kb = [k[i] * beta[i] for i in n]
        k16 = [k[i].astype(BF16) for i in n]
        a = [jnp.where(strict, _dot_nt(kb[i].astype(BF16), k16[i]) * decay[i], 0.0) for i in n]
        t = [eye - jnp.where(level_masks[0], a[i], 0.0) for i in n]
        for m in level_masks[1:]:
            t16 = [t[i].astype(BF16) for i in n]
            tx = [_dot(t16[i], jnp.where(m, a[i], 0.0).astype(BF16)) for i in n]
            t = [t[i] - _dot(tx[i].astype(BF16), t16[i]) for i in n]
        eg = [jnp.exp(gc[i]) for i in n]
        rhs = [jnp.concatenate([v[i] * beta[i], kb[i] * eg[i]], axis=1).astype(BF16) for i in n]
        uw = [_dot(t[i].astype(BF16), rhs[i]) for i in n]
        qk = [(_dot_nt(q[i].astype(BF16), k16[i]) * decay[i]).astype(BF16) for i in n]
        for i, (s, h) in enumerate(probs):
            r, sc = rows[s], pairs[s]
            glast = jnp.where(first_half, gc[i][C - 1:C, :], gc[i][P - 1:P, :])
            u_ref[r, _head_cols(h)] = uw[i][:, :HEAD_DIM]
            w_ref[r, _head_cols(h)] = uw[i][:, HEAD_DIM:].astype(BF16)
            qd_ref[r, _head_cols(h)] = (q[i] * eg[i]).astype(BF16)
            kd_ref[r, _head_cols(h)] = (k[i] * jnp.exp(glast - gc[i])).astype(BF16)
            qk_ref[pl.ds(pl.multiple_of(sc * P, P), C), h * C:(h + 1) * C] = qk[i][:C, :C]
            qk_ref[pl.ds(pl.multiple_of(sc * P + C, C), C), h * C:(h + 1) * C] = qk[i][C:, C:]
        return 0

    lax.fori_loop(0, L // (P * DN_PAIRS_PER_ITER), phase_a, 0)

    nw = nw_ref[...]

    def phase_b(c, _):
        r = pl.ds(pl.multiple_of(c * C, C), C)
        gb = gates_ref[r, :]
        s = [state_ref[h] for h in hrange]
        lhs = [jnp.concatenate([w_ref[r, _head_cols(h)], qd_ref[r, _head_cols(h)]], axis=0) for h in hrange]
        ws = [_dot(lhs[h], s[h].astype(BF16)) for h in hrange]
        v_new = [u_ref[r, _head_cols(h)] - ws[h][:C] for h in hrange]
        vn16 = [v_new[h].astype(BF16) for h in hrange]
        o = [ws[h][C:] + _dot(qk_ref[r, h * C:(h + 1) * C], vn16[h]) for h in hrange]
        for h in hrange:
            cd = jnp.exp(gb[C - 1:C, HEADS + h:HEADS + h + 1])
            state_ref[h] = s[h] * cd + _dot_tn(kd_ref[r, _head_cols(h)], vn16[h])
        for h in hrange:
            on = o[h] * lax.rsqrt(jnp.mean(o[h] * o[h], axis=-1, keepdims=True) + NORM_EPS) * nw
            o_ref[r, _head_cols(h)] = (on * _silu(z_ref[r, _head_cols(h)].astype(F32))).astype(BF16)
        return 0

    lax.fori_loop(0, L // C, phase_b, 0)


def _deltanet(proj, gates, g_row, conv_w, norm_w, batch, seq, L):
    t = proj.shape[0]
    nl = seq // L
    blk = lambda col: pl.BlockSpec((L, D_MODEL), lambda b, s, col=col: (b * nl + s, col))
    return pl.pallas_call(
        _deltanet_kernel,
        grid=(batch, nl),
        in_specs=[
            blk(COL_DQ), blk(COL_DK), blk(COL_DV), blk(COL_DZ),
            pl.BlockSpec((L, LANES), lambda b, s: (b * nl + s, 0)),
            pl.BlockSpec((1, HEADS, seq // DN_PAIR, DN_PAIR), lambda b, s: (b, 0, 0, 0)),
            pl.BlockSpec((DN_CONV_K, 3 * D_MODEL), lambda b, s: (0, 0)),
            pl.BlockSpec((1, HEAD_DIM), lambda b, s: (0, 0)),
        ],
        out_specs=pl.BlockSpec((L, D_MODEL), lambda b, s: (b * nl + s, 0)),
        out_shape=jax.ShapeDtypeStruct((t, D_MODEL), BF16),
        scratch_shapes=[
            pltpu.VMEM((SUBLANES, 3 * D_MODEL), F32),
            pltpu.VMEM((HEADS, HEAD_DIM, HEAD_DIM), F32),
            pltpu.VMEM((L, D_MODEL), F32),
            pltpu.VMEM((L, D_MODEL), F32),
            pltpu.VMEM((L, D_MODEL), F32),
            pltpu.VMEM((L, D_MODEL), F32),
            pltpu.VMEM((L, D_MODEL), BF16),
            pltpu.VMEM((L, D_MODEL), BF16),
            pltpu.VMEM((L, D_MODEL), BF16),
            pltpu.VMEM((L, HEADS * DN_CHUNK), BF16),
            pltpu.VMEM((L + SUBLANES, D_MODEL), F32),
        ],
        compiler_params=_cparams("parallel", "arbitrary"),
        name="deltanet",
    )(proj, proj, proj, proj, gates, g_row, conv_w, norm_w)


def _moba_prep_kernel(q_ref, k_ref, v_ref, cos_ref, sin_ref, qa_ref, ka_ref, vt_ref, kmean_ref):
    i = pl.program_id(1)
    BS = MOBA_BLOCK
    nbp = kmean_ref.shape[1]
    hrange = range(HEADS)

    @pl.when(i == 0)
    def _():
        kmean_ref[...] = jnp.zeros_like(kmean_ref)

    cos = cos_ref[...]
    sin = sin_ref[...]
    rope = lambda x: x * cos + pltpu.roll(x, HEAD_DIM // 2, axis=1) * sin
    q = [rope(q_ref[:, _head_cols(h)].astype(F32)) for h in hrange]
    k = [rope(k_ref[:, _head_cols(h)].astype(F32)) for h in hrange]

    gate = [lax.dot_general(kmean_ref[h], q[h], (((1,), (1,)), ((), ())),
                            precision=lax.Precision.HIGHEST, preferred_element_type=F32)
            for h in hrange]
    blk = lax.broadcasted_iota(jnp.int32, (nbp, BS), 0)
    blk_f = blk.astype(F32)
    neg_inf = jnp.float32(-jnp.inf)
    g = [jnp.where(blk < i, gate[h], neg_inf) for h in hrange]
    sel = [blk == i for h in hrange]
    for _ in range(MOBA_TOPK):
        m = [jnp.max(g[h], axis=0, keepdims=True) for h in hrange]
        first = [jnp.min(jnp.where((g[h] == m[h]) & (g[h] > neg_inf), blk_f, float(nbp)),
                         axis=0, keepdims=True) for h in hrange]
        pick = [blk_f == first[h] for h in hrange]
        sel = [sel[h] | pick[h] for h in hrange]
        g = [jnp.where(pick[h], neg_inf, g[h]) for h in hrange]

    lane = lax.broadcasted_iota(jnp.int32, (BS, LANES), 1)
    onehot = (lane == i).astype(BF16)
    qt = [(q[h] * (HEAD_DIM ** -0.5 * LOG2_E)).T.astype(BF16) for h in hrange]
    vt = [v_ref[:, _head_cols(h)].astype(F32).T.astype(BF16) for h in hrange]
    for h in hrange:
        qa_ref[0, h, :HEAD_DIM, :] = qt[h]
        qa_ref[0, h, HEAD_DIM:HEAD_DIM + nbp, :] = jnp.where(sel[h], 0.0, MASK_NEG).astype(BF16)
        if nbp < HEAD_DIM:
            qa_ref[0, h, HEAD_DIM + nbp:, :] = jnp.zeros((HEAD_DIM - nbp, BS), BF16)
        ka_ref[0, h, :, :HEAD_DIM] = k[h].astype(BF16)
        ka_ref[0, h, :, HEAD_DIM:] = onehot
        vt_ref[0, h, 0] = vt[h]
        kmean_ref[h, pl.ds(i, 1), :] = jnp.mean(k[h], axis=0, keepdims=True)


def _moba_prep(proj, cos_full, sin_signed, batch, seq):
    nb = seq // MOBA_BLOCK
    nbp = -(-nb // 16) * 16
    blk = lambda col: pl.BlockSpec((MOBA_BLOCK, D_MODEL), lambda b, i, col=col: (b * nb + i, col))
    tab = pl.BlockSpec((MOBA_BLOCK, HEAD_DIM), lambda b, i: (i, 0))
    return pl.pallas_call(
        _moba_prep_kernel,
        grid=(batch, nb),
        in_specs=[blk(COL_MQ), blk(COL_MK), blk(COL_MV), tab, tab],
        out_specs=[pl.BlockSpec((1, HEADS, 2 * HEAD_DIM, MOBA_BLOCK), lambda b, i: (b, 0, 0, i)),
                   pl.BlockSpec((1, HEADS, MOBA_BLOCK, 2 * HEAD_DIM), lambda b, i: (b, 0, i, 0)),
                   pl.BlockSpec((1, HEADS, 1, HEAD_DIM, MOBA_BLOCK),
                                lambda b, i: (b, 0, i // MOBA_GROUP, 0, i % MOBA_GROUP))],
        out_shape=[jax.ShapeDtypeStruct((batch, HEADS, 2 * HEAD_DIM, seq), BF16),
                   jax.ShapeDtypeStruct((batch, HEADS, seq, 2 * HEAD_DIM), BF16),
                   jax.ShapeDtypeStruct((batch, HEADS, nb // MOBA_GROUP, HEAD_DIM,
                                         MOBA_GROUP * MOBA_BLOCK), BF16)],
        scratch_shapes=[pltpu.VMEM((HEADS, nbp, HEAD_DIM), F32)],
        compiler_params=_cparams("parallel", "arbitrary"),
        name="moba_prep",
    )(proj, proj, proj, cos_full, sin_signed)


def _moba_attn_kernel(qa_ref, ka_ref, vt_ref, o_ref, sa_ref, sb_ref):
    tq_idx = pl.program_id(2)
    TK = MOBA_GROUP * MOBA_BLOCK
    TQ = MOBA_QTILE_KEYS * TK
    qt = qa_ref[0, 0]
    n_key_tiles = ka_ref.shape[2] // TK

    def scores(j):
        return _dot(ka_ref[0, 0, pl.ds(pl.multiple_of(j * TK, TK), TK), :], qt)

    def attend(s, j, m, l, acc):
        m_new = jnp.maximum(m, jnp.max(s, axis=0, keepdims=True))
        alpha = jnp.exp2(m - m_new)
        p = jnp.exp2(s - m_new)
        l = alpha * l + jnp.sum(p, axis=0, keepdims=True)
        acc = alpha * acc + _dot(vt_ref[0, 0, j], p.astype(BF16))
        return m_new, l, acc

    def step(g, cur_ref, nxt_ref, carry, own=None):
        if nxt_ref is not None:
            nxt_ref[...] = scores(jnp.minimum(g + 1, n_key_tiles - 1))
        s = cur_ref[...]
        if own is not None:
            ki = lax.broadcasted_iota(jnp.int32, (TK, TQ), 0) + own * TK
            qi = lax.broadcasted_iota(jnp.int32, (TK, TQ), 1)
            s = jnp.where(ki <= qi, s, MASK_NEG)
        return attend(s, g, *carry)

    assert MOBA_QTILE_KEYS == 2
    sa_ref[...] = scores(0)
    carry = (jnp.full((1, TQ), MASK_NEG, F32), jnp.zeros((1, TQ), F32), jnp.zeros((HEAD_DIM, TQ), F32))

    def two_steps(h, carry):
        carry = step(2 * h, sa_ref, sb_ref, carry)
        return step(2 * h + 1, sb_ref, sa_ref, carry)

    carry = lax.fori_loop(0, tq_idx, two_steps, carry)
    carry = step(2 * tq_idx, sa_ref, sb_ref, carry, own=0)
    _, l, acc = step(2 * tq_idx + 1, sb_ref, None, carry, own=1)
    o_ref[...] = (acc / l).T.astype(BF16)


def _moba_attn(q_aug_t, k_aug, v_t, batch, seq):
    tk = MOBA_GROUP * MOBA_BLOCK
    tq = MOBA_QTILE_KEYS * tk
    return pl.pallas_call(
        _moba_attn_kernel,
        grid=(batch, HEADS, seq // tq),
        in_specs=[
            pl.BlockSpec((1, 1, 2 * HEAD_DIM, tq), lambda b, h, i: (b, h, 0, i)),
            pl.BlockSpec((1, 1, seq, 2 * HEAD_DIM), lambda b, h, i: (b, h, 0, 0)),
            pl.BlockSpec((1, 1, seq // tk, HEAD_DIM, tk), lambda b, h, i: (b, h, 0, 0, 0)),
        ],
        out_specs=pl.BlockSpec((tq, HEAD_DIM), lambda b, h, i: (b * (seq // tq) + i, h)),
        out_shape=jax.ShapeDtypeStruct((batch * seq, D_MODEL), BF16),
        scratch_shapes=[pltpu.VMEM((tk, tq), F32), pltpu.VMEM((tk, tq), F32)],
        compiler_params=_cparams("parallel", "parallel", "arbitrary"),
        name="moba_attn",
    )(q_aug_t, k_aug, v_t)


def _merge_kernel(ax_ref, ac_ref, ab_ref, hx_ref, hc_ref, cw_ref, ydn_ref, ymb_ref,
                  ga_ref, gd_ref, gm_ref, x_ref, wa_ref, wd_ref, wm_ref, wo_ref, o_ref, *, seq):
    tm = ax_ref.shape[0]
    f32 = lambda ref: ref[...].astype(F32)
    first = (pl.program_id(0) * tm) % seq == 0
    halo = jnp.where(first, 0.0, (f32(hx_ref) * f32(hc_ref))[HALO_ROWS - SUBLANES:])
    p = f32(ax_ref) * f32(ac_ref)
    y_a = f32(ab_ref) * _causal_conv(p, halo, cw_ref[...], CONV_A_K)
    merged = (jax.nn.sigmoid(f32(ga_ref)) * _dot(y_a.astype(BF16), wa_ref[...])
              + jax.nn.sigmoid(f32(gd_ref)) * _dot(ydn_ref[...], wd_ref[...])
              + jax.nn.sigmoid(f32(gm_ref)) * _dot(ymb_ref[...], wm_ref[...]))
    o_ref[...] = x_ref[...] + _dot(merged.astype(BF16), wo_ref[...])


def _merge(proj, conv_w, y_dn, y_mb, x, wa, wd, wm, wo, seq, tm):
    t = x.shape[0]
    blk = lambda col: pl.BlockSpec((tm, D_MODEL), lambda i, col=col: (i, col))
    halo = lambda col: pl.BlockSpec(
        (HALO_ROWS, D_MODEL), lambda i, col=col: (jnp.maximum(i * (tm // HALO_ROWS) - 1, 0), col))
    row = pl.BlockSpec((tm, D_MODEL), lambda i: (i, 0))
    wspec = pl.BlockSpec((D_MODEL, D_MODEL), lambda i: (0, 0))
    return pl.pallas_call(
        functools.partial(_merge_kernel, seq=seq),
        grid=(t // tm,),
        in_specs=[blk(COL_AX), blk(COL_AC), blk(COL_AB), halo(COL_AX), halo(COL_AC),
                  pl.BlockSpec((CONV_A_K, D_MODEL), lambda i: (0, 0)),
                  row, row, blk(COL_GATE), blk(COL_GATE + 1), blk(COL_GATE + 2), row,
                  wspec, wspec, wspec, wspec],
        out_specs=row,
        out_shape=jax.ShapeDtypeStruct((t, D_MODEL), F32),
        compiler_params=_cparams("parallel"),
        name="merge",
    )(proj, proj, proj, proj, proj, conv_w, y_dn, y_mb, proj, proj, proj, x, wa, wd, wm, wo)


def _ffn_kernel(x_ref, g_ref, wg_ref, wu_ref, wd_ref, o_ref, h_ref, acc_ref):
    j = pl.program_id(1)

    @pl.when(j == 0)
    def _():
        x = x_ref[...]
        ms = jnp.mean(x * x, axis=-1, keepdims=True)
        h_ref[...] = (x * lax.rsqrt(ms + NORM_EPS) * g_ref[...]).astype(BF16)
        acc_ref[...] = x

    h = h_ref[...]
    act = _silu(_dot(h, wg_ref[...])) * _dot(h, wu_ref[...])
    acc_ref[...] += _dot(act.astype(BF16), wd_ref[...])

    @pl.when(j == pl.num_programs(1) - 1)
    def _():
        o_ref[...] = acc_ref[...]


def _ffn(x, gain, wg, wu, wd, tm, th):
    t, d = x.shape
    hid = wg.shape[1]
    return pl.pallas_call(
        _ffn_kernel,
        grid=(t // tm, hid // th),
        in_specs=[
            pl.BlockSpec((tm, d), lambda i, j: (i, 0)),
            pl.BlockSpec((1, d), lambda i, j: (0, 0)),
            pl.BlockSpec((d, th), lambda i, j: (0, j)),
            pl.BlockSpec((d, th), lambda i, j: (0, j)),
            pl.BlockSpec((th, d), lambda i, j: (j, 0)),
        ],
        out_specs=pl.BlockSpec((tm, d), lambda i, j: (i, 0)),
        out_shape=jax.ShapeDtypeStruct((t, d), F32),
        scratch_shapes=[pltpu.VMEM((tm, d), BF16), pltpu.VMEM((tm, d), F32)],
        compiler_params=_cparams("parallel", "arbitrary"),
        name="ffn",
    )(x, gain, wg, wu, wd)


def _final_norm_kernel(x_ref, g_ref, o_ref):
    x = x_ref[...]
    ms = jnp.mean(x * x, axis=-1, keepdims=True)
    o_ref[...] = x * lax.rsqrt(ms + NORM_EPS) * g_ref[...]


def _final_norm(x, gain, tm):
    t, d = x.shape
    return pl.pallas_call(
        _final_norm_kernel,
        grid=(t // tm,),
        in_specs=[pl.BlockSpec((tm, d), lambda i: (i, 0)), pl.BlockSpec((1, d), lambda i: (0, 0))],
        out_specs=pl.BlockSpec((tm, d), lambda i: (i, 0)),
        out_shape=jax.ShapeDtypeStruct((t, d), F32),
        compiler_params=_cparams("parallel"),
        name="final_norm",
    )(x, gain)


def _largest_tile(n, cap):
    t = cap
    while n % t:
        t //= 2
    return t


def _prep_w_in(w):
    d = D_MODEL
    o_dn_qkv = 3 * d
    o_dn_z = o_dn_qkv + 3 * d
    o_dn_b = o_dn_z + d
    o_dn_a = o_dn_b + HEADS
    o_mb = o_dn_a + HEADS
    o_gate = o_mb + 3 * d
    small = jnp.concatenate([w[:, o_dn_b:o_mb],
                             jnp.zeros((d, N_SMALL - 2 * HEADS), w.dtype)], axis=1)
    main = jnp.concatenate([w[:, :o_dn_b].astype(BF16), w[:, o_mb:o_gate + 3 * d].astype(BF16)], axis=1)
    return main, small.astype(BF16)


def _lane_row(vals, offset):
    row = jnp.zeros((1, LANES), F32)
    return row.at[0, offset:offset + vals.shape[0]].set(vals.astype(F32))


def kernel(x, attn_norm, w_in, conv_a_w, dn_conv_w, dn_a_log, dn_dt_bias, dn_norm, w_br_a, w_br_dn,
           w_br_moba, w_out, ffn_norm, w_gate_up, w_down, final_norm):
    batch, seq, d = x.shape
    depth = attn_norm.shape[0]
    assert d == D_MODEL and seq // MOBA_BLOCK <= LANES
    assert seq % (MOBA_QTILE_KEYS * MOBA_GROUP * MOBA_BLOCK) == 0
    t = batch * seq
    tm_big = _largest_tile(t, 1024)
    dn_len = _largest_tile(seq, 512)
    assert dn_len % DN_PAIR == 0 and (seq // DN_PAIR) % SUBLANES == 0

    inv = 1.0 / (ROPE_THETA ** (jnp.arange(0, HEAD_DIM, 2, dtype=F32) / HEAD_DIM))
    ang = jnp.arange(seq, dtype=F32)[:, None] * inv[None, :]
    cos_full = jnp.concatenate([jnp.cos(ang), jnp.cos(ang)], axis=-1)
    sin_signed = jnp.concatenate([-jnp.sin(ang), jnp.sin(ang)], axis=-1)

    xf = x.reshape(t, d)
    for l in range(depth):
        w_main, w_small = _prep_w_in(w_in[l])
        gain = attn_norm[l][None, :]
        proj = _norm_matmul(xf, gain, w_main, _largest_tile(t, 2048), 512, BF16)
        raw_gates = _norm_matmul(xf, gain, w_small, tm_big, N_SMALL, F32)

        gates = _dn_gates(raw_gates, _lane_row(dn_a_log[l], HEADS), _lane_row(dn_dt_bias[l], HEADS), tm_big)
        g_row = (gates[:, HEADS:2 * HEADS].reshape(batch, seq, HEADS).transpose(0, 2, 1)
                 .reshape(batch, HEADS, seq // DN_PAIR, DN_PAIR))
        y_dn = _deltanet(proj, gates, g_row, dn_conv_w[l], dn_norm[l][None, :], batch, seq, dn_len)

        q_aug_t, k_aug, v_t = _moba_prep(proj, cos_full, sin_signed, batch, seq)
        y_mb = _moba_attn(q_aug_t, k_aug, v_t, batch, seq)

        x1 = _merge(proj, conv_a_w[l], y_dn, y_mb, xf, w_br_a[l].astype(BF16),
                    w_br_dn[l].astype(BF16), w_br_moba[l].astype(BF16), w_out[l].astype(BF16),
                    seq, _largest_tile(t, 512))
        wgu = w_gate_up[l].astype(BF16)
        xf = _ffn(x1, ffn_norm[l][None, :], wgu[:, :FFN_HIDDEN], wgu[:, FFN_HIDDEN:],
                  w_down[l].astype(BF16), tm_big, 256)

    return _final_norm(xf, final_norm[None, :], tm_big).reshape(batch, seq, d)
```

```python
import functools

import jax
import jax.numpy as jnp
from jax import lax
from jax.experimental import pallas as pl
from jax.experimental.pallas import tpu as pltpu

F32 = jnp.float32
BF16 = jnp.bfloat16

D_MODEL = 1024
HEADS = 8
HEAD_DIM = 128
NORM_EPS = 1e-6
CONV_A_K = 3
DN_CONV_K = 4
DN_CHUNK = 64
DN_PAIR = 2 * DN_CHUNK
DN_PAIRS_PER_ITER = 4
MOBA_BLOCK = 256
MOBA_TOPK = 3
MOBA_GROUP = 2
MOBA_QTILE_KEYS = 2
LOG2_E = 1.4426950408889634
ROPE_THETA = 10000.0
FFN_HIDDEN = 2816

COL_AX, COL_AC, COL_AB = 0, 1, 2
COL_DQ, COL_DK, COL_DV, COL_DZ = 3, 4, 5, 6
N_PROJ_A = 7 * D_MODEL
COL_MQ, COL_MK, COL_MV = 0, 1, 2
COL_GATE = 3
N_PROJ_B = 6 * D_MODEL
N_SMALL = 128

MASK_NEG = -1e30
VMEM_LIMIT = 56 * 1024 * 1024
SUBLANES = 8
LANES = 128
HALO_ROWS = 16


def _cparams(*sem):
    return pltpu.CompilerParams(dimension_semantics=sem, vmem_limit_bytes=VMEM_LIMIT)


def _silu(x):
    return x * jax.nn.sigmoid(x)


def _dot(a, b):
    return jnp.dot(a, b, preferred_element_type=F32)


def _dot_nt(a, b):
    return lax.dot_general(a, b, (((1,), (1,)), ((), ())), preferred_element_type=F32)


def _dot_tn(a, b):
    return lax.dot_general(a, b, (((0,), (0,)), ((), ())), preferred_element_type=F32)


def _in_proj_kernel(x_ref, g_ref, wa_ref, wb_ref, ws_ref, oa_ref, ob_ref, os_ref, h_ref, *, na):
    j = pl.program_id(1)

    @pl.when(j == 0)
    def _():
        x = x_ref[...]
        ms = jnp.mean(x * x, axis=-1, keepdims=True)
        h_ref[...] = (x * lax.rsqrt(ms + NORM_EPS) * g_ref[...]).astype(BF16)
        os_ref[...] = _dot(h_ref[...], ws_ref[...])

    @pl.when(j < na)
    def _():
        oa_ref[...] = _dot(h_ref[...], wa_ref[...]).astype(BF16)

    @pl.when(j >= na)
    def _():
        ob_ref[...] = _dot(h_ref[...], wb_ref[...]).astype(BF16)


def _in_proj(x, gain, wa, wb, ws, tm, tn):
    t, d = x.shape
    na, nb = wa.shape[1] // tn, wb.shape[1] // tn
    a_idx = lambda j: jnp.minimum(j, na - 1)
    b_idx = lambda j: jnp.maximum(j - na, 0)
    return pl.pallas_call(
        functools.partial(_in_proj_kernel, na=na),
        grid=(t // tm, na + nb),
        in_specs=[
            pl.BlockSpec((tm, d), lambda i, j: (i, 0)),
            pl.BlockSpec((1, d), lambda i, j: (0, 0)),
            pl.BlockSpec((d, tn), lambda i, j: (0, a_idx(j))),
            pl.BlockSpec((d, tn), lambda i, j: (0, b_idx(j))),
            pl.BlockSpec((d, N_SMALL), lambda i, j: (0, 0)),
        ],
        out_specs=[pl.BlockSpec((tm, tn), lambda i, j: (i, a_idx(j))),
                   pl.BlockSpec((tm, tn), lambda i, j: (i, b_idx(j))),
                   pl.BlockSpec((tm, N_SMALL), lambda i, j: (i, 0))],
        out_shape=[jax.ShapeDtypeStruct((t, wa.shape[1]), BF16),
                   jax.ShapeDtypeStruct((t, wb.shape[1]), BF16),
                   jax.ShapeDtypeStruct((t, N_SMALL), F32)],
        scratch_shapes=[pltpu.VMEM((tm, d), BF16)],
        compiler_params=_cparams("parallel", "arbitrary"),
        name="in_proj",
    )(x, gain, wa, wb, ws)


def _dn_gates_kernel(raw_ref, alog_ref, dt_ref, o_ref):
    raw = raw_ref[...]
    tm = raw.shape[0]
    xa = raw + dt_ref[...]
    softplus = jnp.maximum(xa, 0.0) + jnp.log1p(jnp.exp(-jnp.abs(xa)))
    g = -jnp.exp(alog_ref[...]) * softplus
    row = lax.broadcasted_iota(jnp.int32, (tm, LANES), 0) % DN_CHUNK
    shift = 1
    while shift < DN_CHUNK:
        g = g + jnp.where(row >= shift, pltpu.roll(g, shift, axis=0), 0.0)
        shift *= 2
    lane = lax.broadcasted_iota(jnp.int32, (tm, LANES), 1)
    o_ref[...] = jnp.where(lane < HEADS, jax.nn.sigmoid(raw), g)


def _dn_gates(raw, alog_row, dt_row, tm):
    t = raw.shape[0]
    return pl.pallas_call(
        _dn_gates_kernel,
        grid=(t // tm,),
        in_specs=[
            pl.BlockSpec((tm, LANES), lambda i: (i, 0)),
            pl.BlockSpec((1, LANES), lambda i: (0, 0)),
            pl.BlockSpec((1, LANES), lambda i: (0, 0)),
        ],
        out_specs=pl.BlockSpec((tm, LANES), lambda i: (i, 0)),
        out_shape=jax.ShapeDtypeStruct((t, LANES), F32),
        compiler_params=_cparams("parallel"),
        name="dn_gates",
    )(raw, alog_row, dt_row)


def _causal_conv(x, carry, w, k):
    xe = jnp.concatenate([carry, x], axis=0)
    y = x * w[k - 1:k]
    for j in range(1, k):
        y = y + pltpu.roll(xe, j, axis=0)[SUBLANES:] * w[k - 1 - j:k - j]
    return y


def _head_cols(h):
    return slice(h * HEAD_DIM, (h + 1) * HEAD_DIM)


def _deltanet_kernel(q_ref, k_ref, v_ref, z_ref, gates_ref, grow_ref, cw_ref, nw_ref, o_ref,
                     carry_ref, state_ref, qn_ref, kn_ref, vv_ref,
                     u_ref, w_ref, qd_ref, kd_ref, qk_ref, xs_ref):
    L = q_ref.shape[0]
    C = DN_CHUNK
    P = DN_PAIR
    hrange = range(HEADS)

    @pl.when(pl.program_id(1) == 0)
    def _():
        carry_ref[...] = jnp.zeros_like(carry_ref)
        state_ref[...] = jnp.zeros_like(state_ref)

    for idx, (src, dst) in enumerate(((q_ref, qn_ref), (k_ref, kn_ref), (v_ref, vv_ref))):
        cols = slice(idx * D_MODEL, (idx + 1) * D_MODEL)
        xs_ref[:SUBLANES, :] = carry_ref[:, cols]
        xs_ref[SUBLANES:, :] = src[...].astype(F32)
        carry_ref[:, cols] = xs_ref[L:, :]
        scale = HEAD_DIM ** -0.5 if idx == 0 else 1.0
        for h in hrange:
            hs = _head_cols(h)
            w = cw_ref[:, idx * D_MODEL + h * HEAD_DIM:idx * D_MODEL + (h + 1) * HEAD_DIM]
            y = xs_ref[SUBLANES:, hs] * w[DN_CONV_K - 1:DN_CONV_K]
            for j in range(1, DN_CONV_K):
                y = y + xs_ref[SUBLANES - j:SUBLANES - j + L, hs] * w[DN_CONV_K - 1 - j:DN_CONV_K - j]
            y = _silu(y)
            if idx < 2:
                y = y * (lax.rsqrt(jnp.sum(y * y, axis=-1, keepdims=True) + NORM_EPS) * scale)
            dst[:, hs] = y

    ri = lax.broadcasted_iota(jnp.int32, (P, P), 0)
    ci = lax.broadcasted_iota(jnp.int32, (P, P), 1)
    same_chunk = (ri // C) == (ci // C)
    causal = same_chunk & (ci <= ri)
    strict = same_chunk & (ci < ri)
    eye = (ci == ri).astype(F32)
    level_masks = []
    bs = 1
    while bs < C:
        same = (ri // (2 * bs)) == (ci // (2 * bs))
        level_masks.append(same & ((ri % (2 * bs)) >= bs) & ((ci % (2 * bs)) < bs))
        bs *= 2
    first_half = lax.broadcasted_iota(jnp.int32, (P, 1), 0) < C
    pair0 = pl.program_id(1) * (L // P)

    def phase_a(it, _):
        pairs = [it * DN_PAIRS_PER_ITER + s for s in range(DN_PAIRS_PER_ITER)]
        rows = [pl.ds(pl.multiple_of(sc * P, P), P) for sc in pairs]
        gbs = [gates_ref[r, :] for r in rows]
        probs = [(s, h) for s in range(DN_PAIRS_PER_ITER) for h in hrange]
        n = range(len(probs))
        q = [qn_ref[rows[s], _head_cols(h)] for s, h in probs]
        k = [kn_ref[rows[s], _head_cols(h)] for s, h in probs]
        v = [vv_ref[rows[s], _head_cols(h)] for s, h in probs]
        beta = [gbs[s][:, h:h + 1] for s, h in probs]
        gc = [gbs[s][:, HEADS + h:HEADS + h + 1] for s, h in probs]
        gr = [grow_ref[0, h, pl.ds(pair0 + pairs[s], 1), :] for s, h in probs]
        decay = [jnp.where(causal, jnp.exp(jnp.where(causal, gc[i] - gr[i], 0.0)), 0.0) for i in n]
        kb = [k[i] * beta[i] for i in n]
        k16 = [k[i].astype(BF16) for i in n]
        a = [jnp.where(strict, _dot_nt(kb[i].astype(BF16), k16[i]) * decay[i], 0.0) for i in n]
        t = [eye - jnp.where(level_masks[0], a[i], 0.0) for i in n]
        for m in level_masks[1:]:
            t16 = [t[i].astype(BF16) for i in n]
            tx = [_dot(t16[i], jnp.where(m, a[i], 0.0).astype(BF16)) for i in n]
            t = [t[i] - _dot(tx[i].astype(BF16), t16[i]) for i in n]
        eg = [jnp.exp(gc[i]) for i in n]
        rhs = [jnp.concatenate([v[i] * beta[i], kb[i] * eg[i]], axis=1).astype(BF16) for i in n]
        uw = [_dot(t[i].astype(BF16), rhs[i]) for i in n]
        qk = [(_dot_nt(q[i].astype(BF16), k16[i]) * decay[i]).astype(BF16) for i in n]
        for i, (s, h) in enumerate(probs):
            r, sc = rows[s], pairs[s]
            glast = jnp.where(first_half, gc[i][C - 1:C, :], gc[i][P - 1:P, :])
            u_ref[r, _head_cols(h)] = uw[i][:, :HEAD_DIM]
            w_ref[r, _head_cols(h)] = uw[i][:, HEAD_DIM:].astype(BF16)
            qd_ref[r, _head_cols(h)] = (q[i] * eg[i]).astype(BF16)
            kd_ref[r, _head_cols(h)] = (k[i] * jnp.exp(glast - gc[i])).astype(BF16)
            qk_ref[pl.ds(pl.multiple_of(sc * P, P), C), h * C:(h + 1) * C] = qk[i][:C, :C]
            qk_ref[pl.ds(pl.multiple_of(sc * P + C, C), C), h * C:(h + 1) * C] = qk[i][C:, C:]
        return 0

    lax.fori_loop(0, L // (P * DN_PAIRS_PER_ITER), phase_a, 0)

    nw = nw_ref[...]

    def phase_b(c, _):
        r = pl.ds(pl.multiple_of(c * C, C), C)
        gb = gates_ref[r, :]
        s = [state_ref[h] for h in hrange]
        lhs = [jnp.concatenate([w_ref[r, _head_cols(h)], qd_ref[r, _head_cols(h)]], axis=0) for h in hrange]
        ws = [_dot(lhs[h], s[h].astype(BF16)) for h in hrange]
        v_new = [u_ref[r, _head_cols(h)] - ws[h][:C] for h in hrange]
        vn16 = [v_new[h].astype(BF16) for h in hrange]
        o = [ws[h][C:] + _dot(qk_ref[r, h * C:(h + 1) * C], vn16[h]) for h in hrange]
        for h in hrange:
            cd = jnp.exp(gb[C - 1:C, HEADS + h:HEADS + h + 1])
            state_ref[h] = s[h] * cd + _dot_tn(kd_ref[r, _head_cols(h)], vn16[h])
        for h in hrange:
            on = o[h] * lax.rsqrt(jnp.mean(o[h] * o[h], axis=-1, keepdims=True) + NORM_EPS) * nw
            o_ref[r, _head_cols(h)] = (on * _silu(z_ref[r, _head_cols(h)].astype(F32))).astype(BF16)
        return 0

    lax.fori_loop(0, L // C, phase_b, 0)


def _deltanet(proj, gates, g_row, conv_w, norm_w, batch, seq, L):
    t = proj.shape[0]
    nl = seq // L
    blk = lambda col: pl.BlockSpec((L, D_MODEL), lambda b, s, col=col: (b * nl + s, col))
    return pl.pallas_call(
        _deltanet_kernel,
        grid=(batch, nl),
        in_specs=[
            blk(COL_DQ), blk(COL_DK), blk(COL_DV), blk(COL_DZ),
            pl.BlockSpec((L, LANES), lambda b, s: (b * nl + s, 0)),
            pl.BlockSpec((1, HEADS, seq // DN_PAIR, DN_PAIR), lambda b, s: (b, 0, 0, 0)),
            pl.BlockSpec((DN_CONV_K, 3 * D_MODEL), lambda b, s: (0, 0)),
            pl.BlockSpec((1, HEAD_DIM), lambda b, s: (0, 0)),
        ],
        out_specs=pl.BlockSpec((L, D_MODEL), lambda b, s: (b * nl + s, 0)),
        out_shape=jax.ShapeDtypeStruct((t, D_MODEL), BF16),
        scratch_shapes=[
            pltpu.VMEM((SUBLANES, 3 * D_MODEL), F32),
            pltpu.VMEM((HEADS, HEAD_DIM, HEAD_DIM), F32),
            pltpu.VMEM((L, D_MODEL), F32),
            pltpu.VMEM((L, D_MODEL), F32),
            pltpu.VMEM((L, D_MODEL), F32),
            pltpu.VMEM((L, D_MODEL), F32),
            pltpu.VMEM((L, D_MODEL), BF16),
            pltpu.VMEM((L, D_MODEL), BF16),
            pltpu.VMEM((L, D_MODEL), BF16),
            pltpu.VMEM((L, HEADS * DN_CHUNK), BF16),
            pltpu.VMEM((L + SUBLANES, D_MODEL), F32),
        ],
        compiler_params=_cparams("parallel", "arbitrary"),
        name="deltanet",
    )(proj, proj, proj, proj, gates, g_row, conv_w, norm_w)


def _moba_prep_kernel(q_ref, k_ref, v_ref, cos_ref, sin_ref, qa_ref, ka_ref, vt_ref, kmean_ref):
    i = pl.program_id(1)
    BS = MOBA_BLOCK
    nbp = kmean_ref.shape[1]
    hrange = range(HEADS)

    @pl.when(i == 0)
    def _():
        kmean_ref[...] = jnp.zeros_like(kmean_ref)

    cos = cos_ref[...]
    sin = sin_ref[...]
    rope = lambda x: x * cos + pltpu.roll(x, HEAD_DIM // 2, axis=1) * sin
    q = [rope(q_ref[:, _head_cols(h)].astype(F32)) for h in hrange]
    k = [rope(k_ref[:, _head_cols(h)].astype(F32)) for h in hrange]

    gate = [lax.dot_general(kmean_ref[h], q[h], (((1,), (1,)), ((), ())),
                            precision=lax.Precision.HIGHEST, preferred_element_type=F32)
            for h in hrange]
    blk = lax.broadcasted_iota(jnp.int32, (nbp, BS), 0)
    blk_f = blk.astype(F32)
    neg_inf = jnp.float32(-jnp.inf)
    g = [jnp.where(blk < i, gate[h], neg_inf) for h in hrange]
    sel = [blk == i for h in hrange]
    for _ in range(MOBA_TOPK):
        m = [jnp.max(g[h], axis=0, keepdims=True) for h in hrange]
        first = [jnp.min(jnp.where((g[h] == m[h]) & (g[h] > neg_inf), blk_f, float(nbp)),
                         axis=0, keepdims=True) for h in hrange]
        pick = [blk_f == first[h] for h in hrange]
        sel = [sel[h] | pick[h] for h in hrange]
        g = [jnp.where(pick[h], neg_inf, g[h]) for h in hrange]

    lane = lax.broadcasted_iota(jnp.int32, (BS, LANES), 1)
    onehot = (lane == i).astype(BF16)
    qt = [(q[h] * (HEAD_DIM ** -0.5 * LOG2_E)).T.astype(BF16) for h in hrange]
    vt = [v_ref[:, _head_cols(h)].astype(F32).T.astype(BF16) for h in hrange]
    for h in hrange:
        qa_ref[0, h, :HEAD_DIM, :] = qt[h]
        qa_ref[0, h, HEAD_DIM:HEAD_DIM + nbp, :] = jnp.where(sel[h], 0.0, MASK_NEG).astype(BF16)
        if nbp < HEAD_DIM:
            qa_ref[0, h, HEAD_DIM + nbp:, :] = jnp.zeros((HEAD_DIM - nbp, BS), BF16)
        ka_ref[0, h, :, :HEAD_DIM] = k[h].astype(BF16)
        ka_ref[0, h, :, HEAD_DIM:] = onehot
        vt_ref[0, h, 0] = vt[h]
        kmean_ref[h, pl.ds(i, 1), :] = jnp.mean(k[h], axis=0, keepdims=True)


def _moba_prep(proj, cos_full, sin_signed, batch, seq):
    nb = seq // MOBA_BLOCK
    nbp = -(-nb // 16) * 16
    blk = lambda col: pl.BlockSpec((MOBA_BLOCK, D_MODEL), lambda b, i, col=col: (b * nb + i, col))
    tab = pl.BlockSpec((MOBA_BLOCK, HEAD_DIM), lambda b, i: (i, 0))
    return pl.pallas_call(
        _moba_prep_kernel,
        grid=(batch, nb),
        in_specs=[blk(COL_MQ), blk(COL_MK), blk(COL_MV), tab, tab],
        out_specs=[pl.BlockSpec((1, HEADS, 2 * HEAD_DIM, MOBA_BLOCK), lambda b, i: (b, 0, 0, i)),
                   pl.BlockSpec((1, HEADS, MOBA_BLOCK, 2 * HEAD_DIM), lambda b, i: (b, 0, i, 0)),
                   pl.BlockSpec((1, HEADS, 1, HEAD_DIM, MOBA_BLOCK),
                                lambda b, i: (b, 0, i // MOBA_GROUP, 0, i % MOBA_GROUP))],
        out_shape=[jax.ShapeDtypeStruct((batch, HEADS, 2 * HEAD_DIM, seq), BF16),
                   jax.ShapeDtypeStruct((batch, HEADS, seq, 2 * HEAD_DIM), BF16),
                   jax.ShapeDtypeStruct((batch, HEADS, nb // MOBA_GROUP, HEAD_DIM,
                                         MOBA_GROUP * MOBA_BLOCK), BF16)],
        scratch_shapes=[pltpu.VMEM((HEADS, nbp, HEAD_DIM), F32)],
        compiler_params=_cparams("parallel", "arbitrary"),
        name="moba_prep",
    )(proj, proj, proj, cos_full, sin_signed)


def _moba_attn_kernel(qa_ref, ka_ref, vt_ref, o_ref, sa_ref, sb_ref):
    tq_idx = pl.program_id(2)
    TK = MOBA_GROUP * MOBA_BLOCK
    TQ = MOBA_QTILE_KEYS * TK
    qt = qa_ref[0, 0]
    n_key_tiles = ka_ref.shape[2] // TK

    def scores(j):
        return _dot(ka_ref[0, 0, pl.ds(pl.multiple_of(j * TK, TK), TK), :], qt)

    def attend(s, j, m, l, acc):
        m_new = jnp.maximum(m, jnp.max(s, axis=0, keepdims=True))
        alpha = jnp.exp2(m - m_new)
        p = jnp.exp2(s - m_new)
        l = alpha * l + jnp.sum(p, axis=0, keepdims=True)
        acc = alpha * acc + _dot(vt_ref[0, 0, j], p.astype(BF16))
        return m_new, l, acc

    def step(g, cur_ref, nxt_ref, carry, own=None):
        if nxt_ref is not None:
            nxt_ref[...] = scores(jnp.minimum(g + 1, n_key_tiles - 1))
        s = cur_ref[...]
        if own is not None:
            ki = lax.broadcasted_iota(jnp.int32, (TK, TQ), 0) + own * TK
            qi = lax.broadcasted_iota(jnp.int32, (TK, TQ), 1)
            s = jnp.where(ki <= qi, s, MASK_NEG)
        return attend(s, g, *carry)

    assert MOBA_QTILE_KEYS == 2
    sa_ref[...] = scores(0)
    carry = (jnp.full((1, TQ), MASK_NEG, F32), jnp.zeros((1, TQ), F32), jnp.zeros((HEAD_DIM, TQ), F32))

    def two_steps(h, carry):
        carry = step(2 * h, sa_ref, sb_ref, carry)
        return step(2 * h + 1, sb_ref, sa_ref, carry)

    carry = lax.fori_loop(0, tq_idx, two_steps, carry)
    carry = step(2 * tq_idx, sa_ref, sb_ref, carry, own=0)
    _, l, acc = step(2 * tq_idx + 1, sb_ref, None, carry, own=1)
    o_ref[...] = (acc / l).T.astype(BF16)


def _moba_attn(q_aug_t, k_aug, v_t, batch, seq):
    tk = MOBA_GROUP * MOBA_BLOCK
    tq = MOBA_QTILE_KEYS * tk
    return pl.pallas_call(
        _moba_attn_kernel,
        grid=(batch, HEADS, seq // tq),
        in_specs=[
            pl.BlockSpec((1, 1, 2 * HEAD_DIM, tq), lambda b, h, i: (b, h, 0, i)),
            pl.BlockSpec((1, 1, seq, 2 * HEAD_DIM), lambda b, h, i: (b, h, 0, 0)),
            pl.BlockSpec((1, 1, seq // tk, HEAD_DIM, tk), lambda b, h, i: (b, h, 0, 0, 0)),
        ],
        out_specs=pl.BlockSpec((tq, HEAD_DIM), lambda b, h, i: (b * (seq // tq) + i, h)),
        out_shape=jax.ShapeDtypeStruct((batch * seq, D_MODEL), BF16),
        scratch_shapes=[pltpu.VMEM((tk, tq), F32), pltpu.VMEM((tk, tq), F32)],
        compiler_params=_cparams("parallel", "parallel", "arbitrary"),
        name="moba_attn",
    )(q_aug_t, k_aug, v_t)


def _merge_kernel(ax_ref, ac_ref, ab_ref, hx_ref, hc_ref, cw_ref, ydn_ref, ymb_ref,
                  ga_ref, gd_ref, gm_ref, x_ref, wa_ref, wd_ref, wm_ref, wo_ref, o_ref, *, seq):
    tm = ax_ref.shape[0]
    f32 = lambda ref: ref[...].astype(F32)
    first = (pl.program_id(0) * tm) % seq == 0
    halo = jnp.where(first, 0.0, (f32(hx_ref) * f32(hc_ref))[HALO_ROWS - SUBLANES:])
    p = f32(ax_ref) * f32(ac_ref)
    y_a = f32(ab_ref) * _causal_conv(p, halo, cw_ref[...], CONV_A_K)
    merged = (jax.nn.sigmoid(f32(ga_ref)) * _dot(y_a.astype(BF16), wa_ref[...])
              + jax.nn.sigmoid(f32(gd_ref)) * _dot(ydn_ref[...], wd_ref[...])
              + jax.nn.sigmoid(f32(gm_ref)) * _dot(ymb_ref[...], wm_ref[...]))
    o_ref[...] = x_ref[...] + _dot(merged.astype(BF16), wo_ref[...])


def _merge(proj_a, proj_b, conv_w, y_dn, y_mb, x, wa, wd, wm, wo, seq, tm):
    t = x.shape[0]
    blk = lambda col: pl.BlockSpec((tm, D_MODEL), lambda i, col=col: (i, col))
    halo = lambda col: pl.BlockSpec(
        (HALO_ROWS, D_MODEL), lambda i, col=col: (jnp.maximum(i * (tm // HALO_ROWS) - 1, 0), col))
    row = pl.BlockSpec((tm, D_MODEL), lambda i: (i, 0))
    wspec = pl.BlockSpec((D_MODEL, D_MODEL), lambda i: (0, 0))
    return pl.pallas_call(
        functools.partial(_merge_kernel, seq=seq),
        grid=(t // tm,),
        in_specs=[blk(COL_AX), blk(COL_AC), blk(COL_AB), halo(COL_AX), halo(COL_AC),
                  pl.BlockSpec((CONV_A_K, D_MODEL), lambda i: (0, 0)),
                  row, row, blk(COL_GATE), blk(COL_GATE + 1), blk(COL_GATE + 2), row,
                  wspec, wspec, wspec, wspec],
        out_specs=row,
        out_shape=jax.ShapeDtypeStruct((t, D_MODEL), F32),
        compiler_params=_cparams("parallel"),
        name="merge",
    )(proj_a, proj_a, proj_a, proj_a, proj_a, conv_w, y_dn, y_mb, proj_b, proj_b, proj_b, x,
      wa, wd, wm, wo)


def _rms_scale(x, gain):
    return x * lax.rsqrt(jnp.mean(x * x, axis=-1, keepdims=True) + NORM_EPS) * gain


def _ffn_kernel(x_ref, g_ref, wg_ref, wu_ref, wd_ref, fg_ref, o_ref, h_ref, acc_ref, *, final_norm):
    j = pl.program_id(1)

    @pl.when(j == 0)
    def _():
        x = x_ref[...]
        h_ref[...] = _rms_scale(x, g_ref[...]).astype(BF16)
        acc_ref[...] = x

    h = h_ref[...]
    act = _silu(_dot(h, wg_ref[...])) * _dot(h, wu_ref[...])
    acc_ref[...] += _dot(act.astype(BF16), wd_ref[...])

    @pl.when(j == pl.num_programs(1) - 1)
    def _():
        y = acc_ref[...]
        o_ref[...] = _rms_scale(y, fg_ref[...]) if final_norm else y


def _ffn(x, gain, w_gate_up, w_down, final_gain, final_norm, tm, th):
    t, d = x.shape
    hid = w_down.shape[0]
    nh = hid // th
    return pl.pallas_call(
        functools.partial(_ffn_kernel, final_norm=final_norm),
        grid=(t // tm, nh),
        in_specs=[
            pl.BlockSpec((tm, d), lambda i, j: (i, 0)),
            pl.BlockSpec((1, d), lambda i, j: (0, 0)),
            pl.BlockSpec((d, th), lambda i, j: (0, j)),
            pl.BlockSpec((d, th), lambda i, j: (0, j + nh)),
            pl.BlockSpec((th, d), lambda i, j: (j, 0)),
            pl.BlockSpec((1, d), lambda i, j: (0, 0)),
        ],
        out_specs=pl.BlockSpec((tm, d), lambda i, j: (i, 0)),
        out_shape=jax.ShapeDtypeStruct((t, d), F32),
        scratch_shapes=[pltpu.VMEM((tm, d), BF16), pltpu.VMEM((tm, d), F32)],
        compiler_params=_cparams("parallel", "arbitrary"),
        name="ffn",
    )(x, gain, w_gate_up, w_gate_up, w_down, final_gain)


def _largest_tile(n, cap):
    t = cap
    while n % t:
        t //= 2
    return t


def _prep_w_in(w):
    o_small = N_PROJ_A
    o_b = o_small + 2 * HEADS
    small = jnp.pad(w[:, o_small:o_b], ((0, 0), (0, N_SMALL - 2 * HEADS)))
    return w[:, :o_small].astype(BF16), w[:, o_b:o_b + N_PROJ_B].astype(BF16), small.astype(BF16)


def _lane_row(vals, offset):
    row = jnp.zeros((1, LANES), F32)
    return row.at[0, offset:offset + vals.shape[0]].set(vals.astype(F32))


def kernel(x, attn_norm, w_in, conv_a_w, dn_conv_w, dn_a_log, dn_dt_bias, dn_norm, w_br_a, w_br_dn,
           w_br_moba, w_out, ffn_norm, w_gate_up, w_down, final_norm):
    batch, seq, d = x.shape
    depth = attn_norm.shape[0]
    assert d == D_MODEL and seq // MOBA_BLOCK <= LANES
    assert seq % (MOBA_QTILE_KEYS * MOBA_GROUP * MOBA_BLOCK) == 0
    t = batch * seq
    tm_big = _largest_tile(t, 1024)
    tm_huge = _largest_tile(t, 2048)
    dn_len = _largest_tile(seq, 512)
    assert dn_len % DN_PAIR == 0 and (seq // DN_PAIR) % SUBLANES == 0

    inv = 1.0 / (ROPE_THETA ** (jnp.arange(0, HEAD_DIM, 2, dtype=F32) / HEAD_DIM))
    ang = jnp.arange(seq, dtype=F32)[:, None] * inv[None, :]
    cos_full = jnp.concatenate([jnp.cos(ang), jnp.cos(ang)], axis=-1)
    sin_signed = jnp.concatenate([-jnp.sin(ang), jnp.sin(ang)], axis=-1)

    xf = x.reshape(t, d)
    for l in range(depth):
        w_a, w_b, w_small = _prep_w_in(w_in[l])
        proj_a, proj_b, raw_gates = _in_proj(xf, attn_norm[l][None, :], w_a, w_b, w_small, tm_huge, 1024)

        gates = _dn_gates(raw_gates, _lane_row(dn_a_log[l], HEADS), _lane_row(dn_dt_bias[l], HEADS), tm_big)
        g_row = (gates[:, HEADS:2 * HEADS].reshape(batch, seq, HEADS).transpose(0, 2, 1)
                 .reshape(batch, HEADS, seq // DN_PAIR, DN_PAIR))
        y_dn = _deltanet(proj_a, gates, g_row, dn_conv_w[l], dn_norm[l][None, :], batch, seq, dn_len)

        q_aug_t, k_aug, v_t = _moba_prep(proj_b, cos_full, sin_signed, batch, seq)
        y_mb = _moba_attn(q_aug_t, k_aug, v_t, batch, seq)

        x1 = _merge(proj_a, proj_b, conv_a_w[l], y_dn, y_mb, xf, w_br_a[l].astype(BF16),
                    w_br_dn[l].astype(BF16), w_br_moba[l].astype(BF16), w_out[l].astype(BF16),
                    seq, _largest_tile(t, 512))
        xf = _ffn(x1, ffn_norm[l][None, :], w_gate_up[l].astype(BF16), w_down[l].astype(BF16),
                  final_norm[None, :], l == depth - 1, tm_huge, 256)

    return xf.reshape(batch, seq, d)
```

```python
import functools

import jax
import jax.numpy as jnp
from jax import lax
from jax.experimental import pallas as pl
from jax.experimental.pallas import tpu as pltpu

F32 = jnp.float32
BF16 = jnp.bfloat16

D_MODEL = 1024
HEADS = 8
HEAD_DIM = 128
NORM_EPS = 1e-6
CONV_A_K = 3
DN_CONV_K = 4
DN_CHUNK = 64
DN_PAIR = 2 * DN_CHUNK
DN_TILE = 256
MOBA_BLOCK = 256
MOBA_TOPK = 3
MOBA_GROUP = 2
MOBA_QTILE_KEYS = 2
LOG2_E = 1.4426950408889634
ROPE_THETA = 10000.0
FFN_HIDDEN = 2816

COL_AX, COL_AC, COL_AB = 0, 1, 2
COL_DQ, COL_DK, COL_DV, COL_DZ = 3, 4, 5, 6
N_PROJ_A = 7 * D_MODEL
COL_MQ, COL_MK, COL_MV = 0, 1, 2
COL_GATE = 3
N_PROJ_B = 6 * D_MODEL
N_SMALL = 128

MASK_NEG = -1e30
VMEM_LIMIT = 56 * 1024 * 1024
SUBLANES = 8
LANES = 128
HALO_ROWS = 16


def _cparams(*sem, flags=None):
    return pltpu.CompilerParams(dimension_semantics=sem, vmem_limit_bytes=VMEM_LIMIT, flags=flags)


def _silu(x):
    return x * jax.nn.sigmoid(x)


def _dot(a, b):
    return jnp.dot(a, b, preferred_element_type=F32)


def _dot_nt(a, b):
    return lax.dot_general(a, b, (((1,), (1,)), ((), ())), preferred_element_type=F32)


def _dot_tn(a, b):
    return lax.dot_general(a, b, (((0,), (0,)), ((), ())), preferred_element_type=F32)


def _in_proj_kernel(x_ref, g_ref, wa_ref, wb_ref, ws_ref, oa_ref, ob_ref, os_ref, h_ref, *, na):
    j = pl.program_id(1)

    @pl.when(j == 0)
    def _():
        x = x_ref[...]
        ms = jnp.mean(x * x, axis=-1, keepdims=True)
        h_ref[...] = (x * lax.rsqrt(ms + NORM_EPS) * g_ref[...]).astype(BF16)
        os_ref[...] = _dot(h_ref[...], ws_ref[...])

    @pl.when(j < na)
    def _():
        oa_ref[...] = _dot(h_ref[...], wa_ref[...]).astype(BF16)

    @pl.when(j >= na)
    def _():
        ob_ref[...] = _dot(h_ref[...], wb_ref[...]).astype(BF16)


def _in_proj(x, gain, wa, wb, ws, tm, tn):
    t, d = x.shape
    na, nb = wa.shape[1] // tn, wb.shape[1] // tn
    a_idx = lambda j: jnp.minimum(j, na - 1)
    b_idx = lambda j: jnp.maximum(j - na, 0)
    return pl.pallas_call(
        functools.partial(_in_proj_kernel, na=na),
        grid=(t // tm, na + nb),
        in_specs=[
            pl.BlockSpec((tm, d), lambda i, j: (i, 0)),
            pl.BlockSpec((1, d), lambda i, j: (0, 0)),
            pl.BlockSpec((d, tn), lambda i, j: (0, a_idx(j))),
            pl.BlockSpec((d, tn), lambda i, j: (0, b_idx(j))),
            pl.BlockSpec((d, N_SMALL), lambda i, j: (0, 0)),
        ],
        out_specs=[pl.BlockSpec((tm, tn), lambda i, j: (i, a_idx(j))),
                   pl.BlockSpec((tm, tn), lambda i, j: (i, b_idx(j))),
                   pl.BlockSpec((tm, N_SMALL), lambda i, j: (i, 0))],
        out_shape=[jax.ShapeDtypeStruct((t, wa.shape[1]), BF16),
                   jax.ShapeDtypeStruct((t, wb.shape[1]), BF16),
                   jax.ShapeDtypeStruct((t, N_SMALL), F32)],
        scratch_shapes=[pltpu.VMEM((tm, d), BF16)],
        compiler_params=_cparams("parallel", "arbitrary"),
        name="in_proj",
    )(x, gain, wa, wb, ws)


def _dn_gates_kernel(raw_ref, alog_ref, dt_ref, o_ref):
    raw = raw_ref[...]
    tm = raw.shape[0]
    xa = raw + dt_ref[...]
    softplus = jnp.maximum(xa, 0.0) + jnp.log1p(jnp.exp(-jnp.abs(xa)))
    g = -jnp.exp(alog_ref[...]) * softplus
    row = lax.broadcasted_iota(jnp.int32, (tm, LANES), 0) % DN_CHUNK
    shift = 1
    while shift < DN_CHUNK:
        g = g + jnp.where(row >= shift, pltpu.roll(g, shift, axis=0), 0.0)
        shift *= 2
    lane = lax.broadcasted_iota(jnp.int32, (tm, LANES), 1)
    o_ref[...] = jnp.where(lane < HEADS, jax.nn.sigmoid(raw), g)


def _dn_gates(raw, alog_row, dt_row, tm):
    t = raw.shape[0]
    return pl.pallas_call(
        _dn_gates_kernel,
        grid=(t // tm,),
        in_specs=[
            pl.BlockSpec((tm, LANES), lambda i: (i, 0)),
            pl.BlockSpec((1, LANES), lambda i: (0, 0)),
            pl.BlockSpec((1, LANES), lambda i: (0, 0)),
        ],
        out_specs=pl.BlockSpec((tm, LANES), lambda i: (i, 0)),
        out_shape=jax.ShapeDtypeStruct((t, LANES), F32),
        compiler_params=_cparams("parallel"),
        name="dn_gates",
    )(raw, alog_row, dt_row)


def _causal_conv(x, carry, w, k):
    xe = jnp.concatenate([carry, x], axis=0)
    y = x * w[k - 1:k]
    for j in range(1, k):
        y = y + pltpu.roll(xe, j, axis=0)[SUBLANES:] * w[k - 1 - j:k - j]
    return y


def _head_cols(h):
    return slice(h * HEAD_DIM, (h + 1) * HEAD_DIM)


def _interleave(*streams):
    order = [((k + 0.5) / len(s), si, step) for si, s in enumerate(streams) for k, step in enumerate(s)]
    for _, _, step in sorted(order, key=lambda e: e[:2]):
        step()


def _deltanet_kernel(q_ref, k_ref, v_ref, z_ref, qx_ref, kx_ref, vx_ref, gates_ref, grow_ref, cw_ref,
                     nw_ref, o_ref, carry_ref, state_ref, xs_ref, qkv_ref,
                     u_ref, w_ref, qd_ref, kd_ref, qk_ref):
    T = DN_TILE
    C = DN_CHUNK
    P = DN_PAIR
    hrange = range(HEADS)
    step_idx = pl.program_id(1)

    def pre_stream(srcs, r0, buf):
        steps = []
        for idx, src in enumerate(srcs):
            def stage(idx=idx, src=src):
                cols = slice(idx * D_MODEL, (idx + 1) * D_MODEL)
                xs_ref[idx, :SUBLANES, :] = carry_ref[:, cols]
                xs_ref[idx, SUBLANES:, :] = src[r0:r0 + T, :].astype(F32)
                carry_ref[:, cols] = xs_ref[idx, T:, :]
            steps.append(stage)
            for h in hrange:
                def piece(idx=idx, h=h):
                    hs = _head_cols(h)
                    w = cw_ref[:, idx * D_MODEL + h * HEAD_DIM:idx * D_MODEL + (h + 1) * HEAD_DIM]
                    y = xs_ref[idx, SUBLANES:, hs] * w[DN_CONV_K - 1:DN_CONV_K]
                    for j in range(1, DN_CONV_K):
                        y = y + (xs_ref[idx, SUBLANES - j:SUBLANES - j + T, hs]
                                 * w[DN_CONV_K - 1 - j:DN_CONV_K - j])
                    y = _silu(y)
                    if idx < 2:
                        scale = HEAD_DIM ** -0.5 if idx == 0 else 1.0
                        y = y * (lax.rsqrt(jnp.sum(y * y, axis=-1, keepdims=True) + NORM_EPS) * scale)
                    qkv_ref[buf, idx, :, hs] = y
                steps.append(piece)
        return steps

    @pl.when(step_idx == 0)
    def _():
        carry_ref[...] = jnp.zeros_like(carry_ref)
        state_ref[...] = jnp.zeros_like(state_ref)
        for step in pre_stream((q_ref, k_ref, v_ref), 0, 0):
            step()

    ri = lax.broadcasted_iota(jnp.int32, (P, P), 0)
    ci = lax.broadcasted_iota(jnp.int32, (P, P), 1)
    same_chunk = (ri // C) == (ci // C)
    causal = same_chunk & (ci <= ri)
    strict = same_chunk & (ci < ri)
    eye = (ci == ri).astype(F32)
    level_masks = []
    bs = 1
    while bs < C:
        same = (ri // (2 * bs)) == (ci // (2 * bs))
        level_masks.append(same & ((ri % (2 * bs)) >= bs) & ((ci % (2 * bs)) < bs))
        bs *= 2
    first_half = lax.broadcasted_iota(jnp.int32, (P, 1), 0) < C
    pair0 = step_idx * (2 * T // P)
    nw = nw_ref[...]

    def a_stream(buf, r0):
        probs = [(sl, h) for sl in range(T // P) for h in hrange]
        n = range(len(probs))
        rows = [slice(sl * P, (sl + 1) * P) for sl in range(T // P)]
        ld = lambda idx, i: qkv_ref[buf, idx, rows[probs[i][0]], _head_cols(probs[i][1])]
        st = {}

        def s_decay():
            gbs = [gates_ref[r0 + sl * P:r0 + (sl + 1) * P, :] for sl in range(T // P)]
            st["beta"] = [gbs[sl][:, h:h + 1] for sl, h in probs]
            st["gc"] = [gbs[sl][:, HEADS + h:HEADS + h + 1] for sl, h in probs]
            gr = [grow_ref[0, h, pl.ds(pair0 + r0 // P + sl, 1), :] for sl, h in probs]
            st["decay"] = [jnp.where(causal, jnp.exp(jnp.where(causal, st["gc"][i] - gr[i], 0.0)), 0.0)
                           for i in n]

        def s_a():
            k = [ld(1, i) for i in n]
            st["kb"] = [k[i] * st["beta"][i] for i in n]
            st["k16"] = [k[i].astype(BF16) for i in n]
            st["a"] = [jnp.where(strict, _dot_nt(st["kb"][i].astype(BF16), st["k16"][i]) * st["decay"][i], 0.0)
                       for i in n]
            st["t"] = [eye - jnp.where(level_masks[0], st["a"][i], 0.0) for i in n]

        def s_tx(m):
            st["t16"] = [st["t"][i].astype(BF16) for i in n]
            st["tx"] = [_dot(st["t16"][i], jnp.where(m, st["a"][i], 0.0).astype(BF16)) for i in n]

        def s_t():
            st["t"] = [st["t"][i] - _dot(st["tx"][i].astype(BF16), st["t16"][i]) for i in n]

        def s_uw():
            st["eg"] = [jnp.exp(st["gc"][i]) for i in n]
            rhs = [jnp.concatenate([ld(2, i) * st["beta"][i], st["kb"][i] * st["eg"][i]], axis=1).astype(BF16)
                   for i in n]
            st["uw"] = [_dot(st["t"][i].astype(BF16), rhs[i]) for i in n]

        def s_qk():
            st["qk"] = [(_dot_nt(ld(0, i).astype(BF16), st["k16"][i]) * st["decay"][i]).astype(BF16) for i in n]

        def s_store():
            for i, (sl, h) in enumerate(probs):
                r, gc = rows[sl], st["gc"][i]
                glast = jnp.where(first_half, gc[C - 1:C, :], gc[P - 1:P, :])
                u_ref[buf, r, _head_cols(h)] = st["uw"][i][:, :HEAD_DIM]
                w_ref[buf, r, _head_cols(h)] = st["uw"][i][:, HEAD_DIM:].astype(BF16)
                qd_ref[buf, r, _head_cols(h)] = (ld(0, i) * st["eg"][i]).astype(BF16)
                kd_ref[buf, r, _head_cols(h)] = (ld(1, i) * jnp.exp(glast - gc)).astype(BF16)
                qk_ref[buf, sl * P:sl * P + C, h * C:(h + 1) * C] = st["qk"][i][:C, :C]
                qk_ref[buf, sl * P + C:(sl + 1) * P, h * C:(h + 1) * C] = st["qk"][i][C:, C:]

        steps = [s_decay, s_a]
        for m in level_masks[1:]:
            steps += [functools.partial(s_tx, m), s_t]
        return steps + [s_uw, s_qk, s_store]

    def b_stream(buf, r0):
        st = {}
        steps = []
        for c in range(T // C):
            rc = slice(c * C, (c + 1) * C)
            rg = slice(r0 + c * C, r0 + (c + 1) * C)

            def s_ws(rc=rc):
                st["s"] = [state_ref[h] for h in hrange]
                lhs = [jnp.concatenate([w_ref[buf, rc, _head_cols(h)], qd_ref[buf, rc, _head_cols(h)]], axis=0)
                       for h in hrange]
                st["ws"] = [_dot(lhs[h], st["s"][h].astype(BF16)) for h in hrange]

            def s_state(rc=rc, rg=rg):
                gb = gates_ref[rg, :]
                vn16 = [(u_ref[buf, rc, _head_cols(h)] - st["ws"][h][:C]).astype(BF16) for h in hrange]
                st["o"] = [st["ws"][h][C:] + _dot(qk_ref[buf, rc, h * C:(h + 1) * C], vn16[h]) for h in hrange]
                for h in hrange:
                    cd = jnp.exp(gb[C - 1:C, HEADS + h:HEADS + h + 1])
                    state_ref[h] = st["s"][h] * cd + _dot_tn(kd_ref[buf, rc, _head_cols(h)], vn16[h])

            def s_out(rg=rg):
                for h in hrange:
                    o = st["o"][h]
                    on = o * lax.rsqrt(jnp.mean(o * o, axis=-1, keepdims=True) + NORM_EPS) * nw
                    o_ref[rg, _head_cols(h)] = (on * _silu(z_ref[rg, _head_cols(h)].astype(F32))).astype(BF16)

            steps += [s_ws, s_state, s_out]
        return steps

    _interleave(a_stream(0, 0), pre_stream((q_ref, k_ref, v_ref), T, 1))
    _interleave(a_stream(1, T), b_stream(0, 0))
    _interleave(b_stream(1, T), pre_stream((qx_ref, kx_ref, vx_ref), 0, 0))


def _deltanet(proj, gates, g_row, conv_w, norm_w, batch, seq):
    t = proj.shape[0]
    T = DN_TILE
    L = 2 * T
    nl = seq // L
    blk = lambda col: pl.BlockSpec((L, D_MODEL), lambda b, s, col=col: (b * nl + s, col))
    nxt = lambda col: pl.BlockSpec(
        (T, D_MODEL), lambda b, s, col=col: (2 * b * nl + jnp.minimum(2 * s + 2, 2 * nl - 1), col))
    return pl.pallas_call(
        _deltanet_kernel,
        grid=(batch, nl),
        in_specs=[
            blk(COL_DQ), blk(COL_DK), blk(COL_DV), blk(COL_DZ),
            nxt(COL_DQ), nxt(COL_DK), nxt(COL_DV),
            pl.BlockSpec((L, LANES), lambda b, s: (b * nl + s, 0)),
            pl.BlockSpec((1, HEADS, seq // DN_PAIR, DN_PAIR), lambda b, s: (b, 0, 0, 0)),
            pl.BlockSpec((DN_CONV_K, 3 * D_MODEL), lambda b, s: (0, 0)),
            pl.BlockSpec((1, HEAD_DIM), lambda b, s: (0, 0)),
        ],
        out_specs=pl.BlockSpec((L, D_MODEL), lambda b, s: (b * nl + s, 0)),
        out_shape=jax.ShapeDtypeStruct((t, D_MODEL), BF16),
        scratch_shapes=[
            pltpu.VMEM((SUBLANES, 3 * D_MODEL), F32),
            pltpu.VMEM((HEADS, HEAD_DIM, HEAD_DIM), F32),
            pltpu.VMEM((3, T + SUBLANES, D_MODEL), F32),
            pltpu.VMEM((2, 3, T, D_MODEL), F32),
            pltpu.VMEM((2, T, D_MODEL), F32),
            pltpu.VMEM((2, T, D_MODEL), BF16),
            pltpu.VMEM((2, T, D_MODEL), BF16),
            pltpu.VMEM((2, T, D_MODEL), BF16),
            pltpu.VMEM((2, T, HEADS * DN_CHUNK), BF16),
        ],
        compiler_params=_cparams("arbitrary", "arbitrary"),
        name="deltanet",
    )(proj, proj, proj, proj, proj, proj, proj, gates, g_row, conv_w, norm_w)


def _moba_prep_kernel(q_ref, k_ref, v_ref, cos_ref, sin_ref, qa_ref, ka_ref, vt_ref, kmean_ref):
    i = pl.program_id(1)
    BS = MOBA_BLOCK
    nbp = kmean_ref.shape[1]
    hrange = range(HEADS)

    @pl.when(i == 0)
    def _():
        kmean_ref[...] = jnp.zeros_like(kmean_ref)

    cos = cos_ref[...]
    sin = sin_ref[...]
    rope = lambda x: x * cos + pltpu.roll(x, HEAD_DIM // 2, axis=1) * sin
    q = [rope(q_ref[:, _head_cols(h)].astype(F32)) for h in hrange]
    k = [rope(k_ref[:, _head_cols(h)].astype(F32)) for h in hrange]

    gate = [lax.dot_general(kmean_ref[h], q[h], (((1,), (1,)), ((), ())),
                            precision=lax.Precision.HIGHEST, preferred_element_type=F32)
            for h in hrange]
    blk = lax.broadcasted_iota(jnp.int32, (nbp, BS), 0)
    blk_f = blk.astype(F32)
    neg_inf = jnp.float32(-jnp.inf)
    g = [jnp.where(blk < i, gate[h], neg_inf) for h in hrange]
    sel = [blk == i for h in hrange]
    for _ in range(MOBA_TOPK):
        m = [jnp.max(g[h], axis=0, keepdims=True) for h in hrange]
        first = [jnp.min(jnp.where((g[h] == m[h]) & (g[h] > neg_inf), blk_f, float(nbp)),
                         axis=0, keepdims=True) for h in hrange]
        pick = [blk_f == first[h] for h in hrange]
        sel = [sel[h] | pick[h] for h in hrange]
        g = [jnp.where(pick[h], neg_inf, g[h]) for h in hrange]

    lane = lax.broadcasted_iota(jnp.int32, (BS, LANES), 1)
    onehot = (lane == i).astype(BF16)
    qt = [(q[h] * (HEAD_DIM ** -0.5 * LOG2_E)).T.astype(BF16) for h in hrange]
    vt = [v_ref[:, _head_cols(h)].astype(F32).T.astype(BF16) for h in hrange]
    for h in hrange:
        qa_ref[0, h, :HEAD_DIM, :] = qt[h]
        qa_ref[0, h, HEAD_DIM:HEAD_DIM + nbp, :] = jnp.where(sel[h], 0.0, MASK_NEG).astype(BF16)
        if nbp < HEAD_DIM:
            qa_ref[0, h, HEAD_DIM + nbp:, :] = jnp.zeros((HEAD_DIM - nbp, BS), BF16)
        ka_ref[0, h, :, :HEAD_DIM] = k[h].astype(BF16)
        ka_ref[0, h, :, HEAD_DIM:] = onehot
        vt_ref[0, h, 0] = vt[h]
        kmean_ref[h, pl.ds(i, 1), :] = jnp.mean(k[h], axis=0, keepdims=True)


def _moba_prep(proj, cos_full, sin_signed, batch, seq):
    nb = seq // MOBA_BLOCK
    nbp = -(-nb // 16) * 16
    blk = lambda col: pl.BlockSpec((MOBA_BLOCK, D_MODEL), lambda b, i, col=col: (b * nb + i, col))
    tab = pl.BlockSpec((MOBA_BLOCK, HEAD_DIM), lambda b, i: (i, 0))
    return pl.pallas_call(
        _moba_prep_kernel,
        grid=(batch, nb),
        in_specs=[blk(COL_MQ), blk(COL_MK), blk(COL_MV), tab, tab],
        out_specs=[pl.BlockSpec((1, HEADS, 2 * HEAD_DIM, MOBA_BLOCK), lambda b, i: (b, 0, 0, i)),
                   pl.BlockSpec((1, HEADS, MOBA_BLOCK, 2 * HEAD_DIM), lambda b, i: (b, 0, i, 0)),
                   pl.BlockSpec((1, HEADS, 1, HEAD_DIM, MOBA_BLOCK),
                                lambda b, i: (b, 0, i // MOBA_GROUP, 0, i % MOBA_GROUP))],
        out_shape=[jax.ShapeDtypeStruct((batch, HEADS, 2 * HEAD_DIM, seq), BF16),
                   jax.ShapeDtypeStruct((batch, HEADS, seq, 2 * HEAD_DIM), BF16),
                   jax.ShapeDtypeStruct((batch, HEADS, nb // MOBA_GROUP, HEAD_DIM,
                                         MOBA_GROUP * MOBA_BLOCK), BF16)],
        scratch_shapes=[pltpu.VMEM((HEADS, nbp, HEAD_DIM), F32)],
        compiler_params=_cparams("parallel", "arbitrary"),
        name="moba_prep",
    )(proj, proj, proj, cos_full, sin_signed)


def _moba_attn_kernel(qa_ref, ka_ref, vt_ref, o_ref, sa_ref, sb_ref):
    tq_idx = pl.program_id(2)
    TK = MOBA_GROUP * MOBA_BLOCK
    TQ = MOBA_QTILE_KEYS * TK
    qt = qa_ref[0, 0]

    def scores(j):
        return _dot(ka_ref[0, 0, pl.ds(pl.multiple_of(j * TK, TK), TK), :], qt)

    def attend(s, j, m, l, acc):
        m_new = jnp.maximum(m, jnp.max(s, axis=0, keepdims=True))
        alpha = jnp.exp2(m - m_new)
        p = jnp.exp2(s - m_new)
        l = alpha * l + jnp.sum(p, axis=0, keepdims=True)
        acc = alpha * acc + _dot(vt_ref[0, 0, j], p.astype(BF16))
        return m_new, l, acc

    def step(g, cur_ref, nxt_ref, carry, own=None):
        if nxt_ref is not None:
            nxt_ref[...] = scores(g + 1)
        s = cur_ref[...]
        if own is not None:
            ki = lax.broadcasted_iota(jnp.int32, (TK, TQ), 0) + own * TK
            qi = lax.broadcasted_iota(jnp.int32, (TK, TQ), 1)
            s = jnp.where(ki <= qi, s, MASK_NEG)
        return attend(s, g, *carry)

    assert MOBA_QTILE_KEYS == 2
    sa_ref[...] = scores(0)
    carry = (jnp.full((1, TQ), MASK_NEG, F32), jnp.zeros((1, TQ), F32), jnp.zeros((HEAD_DIM, TQ), F32))

    def two_steps(h, carry):
        carry = step(2 * h, sa_ref, sb_ref, carry)
        return step(2 * h + 1, sb_ref, sa_ref, carry)

    carry = lax.fori_loop(0, tq_idx, two_steps, carry)
    carry = step(2 * tq_idx, sa_ref, sb_ref, carry, own=0)
    _, l, acc = step(2 * tq_idx + 1, sb_ref, None, carry, own=1)
    o_ref[...] = (acc / l).T.astype(BF16)


def _moba_attn(q_aug_t, k_aug, v_t, batch, seq):
    tk = MOBA_GROUP * MOBA_BLOCK
    tq = MOBA_QTILE_KEYS * tk
    return pl.pallas_call(
        _moba_attn_kernel,
        grid=(batch, HEADS, seq // tq),
        in_specs=[
            pl.BlockSpec((1, 1, 2 * HEAD_DIM, tq), lambda b, h, i: (b, h, 0, i)),
            pl.BlockSpec((1, 1, seq, 2 * HEAD_DIM), lambda b, h, i: (b, h, 0, 0)),
            pl.BlockSpec((1, 1, seq // tk, HEAD_DIM, tk), lambda b, h, i: (b, h, 0, 0, 0)),
        ],
        out_specs=pl.BlockSpec((tq, HEAD_DIM), lambda b, h, i: (b * (seq // tq) + i, h)),
        out_shape=jax.ShapeDtypeStruct((batch * seq, D_MODEL), BF16),
        scratch_shapes=[pltpu.VMEM((tk, tq), F32), pltpu.VMEM((tk, tq), F32)],
        compiler_params=_cparams("parallel", "parallel", "arbitrary"),
        name="moba_attn",
    )(q_aug_t, k_aug, v_t)


def _merge_kernel(ax_ref, ac_ref, ab_ref, hx_ref, hc_ref, cw_ref, ydn_ref, ymb_ref,
                  ga_ref, gd_ref, gm_ref, x_ref, wa_ref, wd_ref, wm_ref, wo_ref, o_ref, *, seq):
    tm = ax_ref.shape[0]
    f32 = lambda ref: ref[...].astype(F32)
    first = (pl.program_id(0) * tm) % seq == 0
    halo = jnp.where(first, 0.0, (f32(hx_ref) * f32(hc_ref))[HALO_ROWS - SUBLANES:])
    p = f32(ax_ref) * f32(ac_ref)
    y_a = f32(ab_ref) * _causal_conv(p, halo, cw_ref[...], CONV_A_K)
    merged = (jax.nn.sigmoid(f32(ga_ref)) * _dot(y_a.astype(BF16), wa_ref[...])
              + jax.nn.sigmoid(f32(gd_ref)) * _dot(ydn_ref[...], wd_ref[...])
              + jax.nn.sigmoid(f32(gm_ref)) * _dot(ymb_ref[...], wm_ref[...]))
    o_ref[...] = x_ref[...] + _dot(merged.astype(BF16), wo_ref[...])


def _merge(proj_a, proj_b, conv_w, y_dn, y_mb, x, wa, wd, wm, wo, seq, tm):
    t = x.shape[0]
    blk = lambda col: pl.BlockSpec((tm, D_MODEL), lambda i, col=col: (i, col))
    halo = lambda col: pl.BlockSpec(
        (HALO_ROWS, D_MODEL), lambda i, col=col: (jnp.maximum(i * (tm // HALO_ROWS) - 1, 0), col))
    row = pl.BlockSpec((tm, D_MODEL), lambda i: (i, 0))
    wspec = pl.BlockSpec((D_MODEL, D_MODEL), lambda i: (0, 0))
    return pl.pallas_call(
        functools.partial(_merge_kernel, seq=seq),
        grid=(t // tm,),
        in_specs=[blk(COL_AX), blk(COL_AC), blk(COL_AB), halo(COL_AX), halo(COL_AC),
                  pl.BlockSpec((CONV_A_K, D_MODEL), lambda i: (0, 0)),
                  row, row, blk(COL_GATE), blk(COL_GATE + 1), blk(COL_GATE + 2), row,
                  wspec, wspec, wspec, wspec],
        out_specs=row,
        out_shape=jax.ShapeDtypeStruct((t, D_MODEL), F32),
        compiler_params=_cparams("parallel"),
        name="merge",
    )(proj_a, proj_a, proj_a, proj_a, proj_a, conv_w, y_dn, y_mb, proj_b, proj_b, proj_b, x,
      wa, wd, wm, wo)


def _rms_scale(x, gain):
    return x * lax.rsqrt(jnp.mean(x * x, axis=-1, keepdims=True) + NORM_EPS) * gain


def _ffn_kernel(x_ref, g_ref, wg_ref, wu_ref, wd_ref, fg_ref, o_ref, h_ref, acc_ref, *, final_norm):
    j = pl.program_id(1)

    @pl.when(j == 0)
    def _():
        x = x_ref[...]
        h_ref[...] = _rms_scale(x, g_ref[...]).astype(BF16)
        acc_ref[...] = x

    h = h_ref[...]
    act = _silu(_dot(h, wg_ref[...])) * _dot(h, wu_ref[...])
    acc_ref[...] += _dot(act.astype(BF16), wd_ref[...])

    @pl.when(j == pl.num_programs(1) - 1)
    def _():
        y = acc_ref[...]
        o_ref[...] = _rms_scale(y, fg_ref[...]) if final_norm else y


def _ffn(x, gain, w_gate_up, w_down, final_gain, final_norm, tm, th):
    t, d = x.shape
    hid = w_down.shape[0]
    nh = hid // th
    return pl.pallas_call(
        functools.partial(_ffn_kernel, final_norm=final_norm),
        grid=(t // tm, nh),
        in_specs=[
            pl.BlockSpec((tm, d), lambda i, j: (i, 0)),
            pl.BlockSpec((1, d), lambda i, j: (0, 0)),
            pl.BlockSpec((d, th), lambda i, j: (0, j)),
            pl.BlockSpec((d, th), lambda i, j: (0, j + nh)),
            pl.BlockSpec((th, d), lambda i, j: (j, 0)),
            pl.BlockSpec((1, d), lambda i, j: (0, 0)),
        ],
        out_specs=pl.BlockSpec((tm, d), lambda i, j: (i, 0)),
        out_shape=jax.ShapeDtypeStruct((t, d), F32),
        scratch_shapes=[pltpu.VMEM((tm, d), BF16), pltpu.VMEM((tm, d), F32)],
        compiler_params=_cparams("parallel", "arbitrary"),
        name="ffn",
    )(x, gain, w_gate_up, w_gate_up, w_down, final_gain)


def _largest_tile(n, cap):
    t = cap
    while n % t:
        t //= 2
    return t


def _prep_w_in(w):
    o_small = N_PROJ_A
    o_b = o_small + 2 * HEADS
    small = jnp.pad(w[:, o_small:o_b], ((0, 0), (0, N_SMALL - 2 * HEADS)))
    return w[:, :o_small].astype(BF16), w[:, o_b:o_b + N_PROJ_B].astype(BF16), small.astype(BF16)


def _lane_row(vals, offset):
    row = jnp.zeros((1, LANES), F32)
    return row.at[0, offset:offset + vals.shape[0]].set(vals.astype(F32))


def kernel(x, attn_norm, w_in, conv_a_w, dn_conv_w, dn_a_log, dn_dt_bias, dn_norm, w_br_a, w_br_dn,
           w_br_moba, w_out, ffn_norm, w_gate_up, w_down, final_norm):
    batch, seq, d = x.shape
    depth = attn_norm.shape[0]
    assert d == D_MODEL and seq // MOBA_BLOCK <= LANES
    assert seq % (MOBA_QTILE_KEYS * MOBA_GROUP * MOBA_BLOCK) == 0
    t = batch * seq
    tm_big = _largest_tile(t, 1024)
    tm_huge = _largest_tile(t, 2048)
    assert seq % (2 * DN_TILE) == 0 and DN_TILE % DN_PAIR == 0 and (seq // DN_PAIR) % SUBLANES == 0

    inv = 1.0 / (ROPE_THETA ** (jnp.arange(0, HEAD_DIM, 2, dtype=F32) / HEAD_DIM))
    ang = jnp.arange(seq, dtype=F32)[:, None] * inv[None, :]
    cos_full = jnp.concatenate([jnp.cos(ang), jnp.cos(ang)], axis=-1)
    sin_signed = jnp.concatenate([-jnp.sin(ang), jnp.sin(ang)], axis=-1)

    xf = x.reshape(t, d)
    for l in range(depth):
        w_a, w_b, w_small = _prep_w_in(w_in[l])
        proj_a, proj_b, raw_gates = _in_proj(xf, attn_norm[l][None, :], w_a, w_b, w_small, tm_huge, 1024)

        gates = _dn_gates(raw_gates, _lane_row(dn_a_log[l], HEADS), _lane_row(dn_dt_bias[l], HEADS), tm_big)
        g_row = (gates[:, HEADS:2 * HEADS].reshape(batch, seq, HEADS).transpose(0, 2, 1)
                 .reshape(batch, HEADS, seq // DN_PAIR, DN_PAIR))
        y_dn = _deltanet(proj_a, gates, g_row, dn_conv_w[l], dn_norm[l][None, :], batch, seq)

        q_aug_t, k_aug, v_t = _moba_prep(proj_b, cos_full, sin_signed, batch, seq)
        y_mb = _moba_attn(q_aug_t, k_aug, v_t, batch, seq)

        x1 = _merge(proj_a, proj_b, conv_a_w[l], y_dn, y_mb, xf, w_br_a[l].astype(BF16),
                    w_br_dn[l].astype(BF16), w_br_moba[l].astype(BF16), w_out[l].astype(BF16),
                    seq, _largest_tile(t, 512))
        xf = _ffn(x1, ffn_norm[l][None, :], w_gate_up[l].astype(BF16), w_down[l].astype(BF16),
                  final_norm[None, :], l == depth - 1, tm_huge, 256)

    return xf.reshape(batch, seq, d)
```

```python
import functools

import jax
import jax.numpy as jnp
from jax import lax
from jax.experimental import pallas as pl
from jax.experimental.pallas import tpu as pltpu

F32 = jnp.float32
BF16 = jnp.bfloat16

D_MODEL = 1024
HEADS = 8
HEAD_DIM = 128
NORM_EPS = 1e-6
CONV_A_K = 3
DN_CONV_K = 4
DN_CHUNK = 64
DN_PAIR = 2 * DN_CHUNK
DN_TILE = 256
MOBA_BLOCK = 256
MOBA_TOPK = 3
MOBA_GROUP = 2
MOBA_QTILE_KEYS = 2
LOG2_E = 1.4426950408889634
ROPE_THETA = 10000.0
FFN_HIDDEN = 2816

COL_AX, COL_AC, COL_AB = 0, 1, 2
COL_DQ, COL_DK, COL_DV, COL_DZ = 3, 4, 5, 6
N_PROJ_A = 7 * D_MODEL
COL_MQ, COL_MK, COL_MV = 0, 1, 2
COL_GATE = 3
N_PROJ_B = 6 * D_MODEL
N_SMALL = 128

MASK_NEG = -1e30
VMEM_LIMIT = 56 * 1024 * 1024
SUBLANES = 8
LANES = 128
HALO_ROWS = 16


def _cparams(*sem, flags=None):
    return pltpu.CompilerParams(dimension_semantics=sem, vmem_limit_bytes=VMEM_LIMIT, flags=flags)


def _silu(x):
    return x * jax.nn.sigmoid(x)


def _dot(a, b):
    return jnp.dot(a, b, preferred_element_type=F32)


def _dot_nt(a, b):
    return lax.dot_general(a, b, (((1,), (1,)), ((), ())), preferred_element_type=F32)


def _dot_tn(a, b):
    return lax.dot_general(a, b, (((0,), (0,)), ((), ())), preferred_element_type=F32)


def _in_proj_kernel(x_ref, g_ref, wa_ref, wb_ref, ws_ref, oa_ref, ob_ref, os_ref, h_ref, *, na):
    j = pl.program_id(1)

    @pl.when(j == 0)
    def _():
        x = x_ref[...]
        ms = jnp.mean(x * x, axis=-1, keepdims=True)
        h_ref[...] = (x * lax.rsqrt(ms + NORM_EPS) * g_ref[...]).astype(BF16)
        os_ref[...] = _dot(h_ref[...], ws_ref[...])

    @pl.when(j < na)
    def _():
        oa_ref[...] = _dot(h_ref[...], wa_ref[...]).astype(BF16)

    @pl.when(j >= na)
    def _():
        ob_ref[...] = _dot(h_ref[...], wb_ref[...]).astype(BF16)


def _in_proj(x, gain, w_all, layer, wb, ws, tm, tn):
    t, d = x.shape
    na, nb = N_PROJ_A // tn, wb.shape[1] // tn
    a_idx = lambda j: jnp.minimum(j, na - 1)
    b_idx = lambda j: jnp.maximum(j - na, 0)
    return pl.pallas_call(
        functools.partial(_in_proj_kernel, na=na),
        grid=(t // tm, na + nb),
        in_specs=[
            pl.BlockSpec((tm, d), lambda i, j: (i, 0)),
            pl.BlockSpec((1, d), lambda i, j: (0, 0)),
            pl.BlockSpec((None, d, tn), lambda i, j: (layer, 0, a_idx(j))),
            pl.BlockSpec((d, tn), lambda i, j: (0, b_idx(j))),
            pl.BlockSpec((d, N_SMALL), lambda i, j: (0, 0)),
        ],
        out_specs=[pl.BlockSpec((tm, tn), lambda i, j: (i, a_idx(j))),
                   pl.BlockSpec((tm, tn), lambda i, j: (i, b_idx(j))),
                   pl.BlockSpec((tm, N_SMALL), lambda i, j: (i, 0))],
        out_shape=[jax.ShapeDtypeStruct((t, N_PROJ_A), BF16),
                   jax.ShapeDtypeStruct((t, wb.shape[1]), BF16),
                   jax.ShapeDtypeStruct((t, N_SMALL), F32)],
        scratch_shapes=[pltpu.VMEM((tm, d), BF16)],
        compiler_params=_cparams("parallel", "arbitrary"),
        name="in_proj",
    )(x, gain, w_all, wb, ws)


def _dn_gates_kernel(raw_ref, alog_ref, dt_ref, o_ref):
    raw = raw_ref[...]
    tm = raw.shape[0]
    xa = raw + dt_ref[...]
    softplus = jnp.maximum(xa, 0.0) + jnp.log1p(jnp.exp(-jnp.abs(xa)))
    g = -jnp.exp(alog_ref[...]) * softplus
    row = lax.broadcasted_iota(jnp.int32, (tm, LANES), 0) % DN_CHUNK
    shift = 1
    while shift < DN_CHUNK:
        g = g + jnp.where(row >= shift, pltpu.roll(g, shift, axis=0), 0.0)
        shift *= 2
    lane = lax.broadcasted_iota(jnp.int32, (tm, LANES), 1)
    o_ref[...] = jnp.where(lane < HEADS, jax.nn.sigmoid(raw), g)


def _dn_gates(raw, alog_row, dt_row, tm):
    t = raw.shape[0]
    return pl.pallas_call(
        _dn_gates_kernel,
        grid=(t // tm,),
        in_specs=[
            pl.BlockSpec((tm, LANES), lambda i: (i, 0)),
            pl.BlockSpec((1, LANES), lambda i: (0, 0)),
            pl.BlockSpec((1, LANES), lambda i: (0, 0)),
        ],
        out_specs=pl.BlockSpec((tm, LANES), lambda i: (i, 0)),
        out_shape=jax.ShapeDtypeStruct((t, LANES), F32),
        compiler_params=_cparams("parallel"),
        name="dn_gates",
    )(raw, alog_row, dt_row)


def _causal_conv(x, carry, w, k):
    xe = jnp.concatenate([carry, x], axis=0)
    y = x * w[k - 1:k]
    for j in range(1, k):
        y = y + pltpu.roll(xe, j, axis=0)[SUBLANES:] * w[k - 1 - j:k - j]
    return y


def _head_cols(h):
    return slice(h * HEAD_DIM, (h + 1) * HEAD_DIM)


def _interleave(*streams):
    order = [((k + 0.5) / len(s), si, step) for si, s in enumerate(streams) for k, step in enumerate(s)]
    for _, _, step in sorted(order, key=lambda e: e[:2]):
        step()


def _deltanet_kernel(q_ref, k_ref, v_ref, z_ref, qx_ref, kx_ref, vx_ref, gates_ref, grow_ref, cw_ref,
                     nw_ref, o_ref, carry_ref, state_ref, xs_ref, qkv_ref,
                     u_ref, w_ref, qd_ref, kd_ref, qk_ref):
    T = DN_TILE
    C = DN_CHUNK
    P = DN_PAIR
    hrange = range(HEADS)
    step_idx = pl.program_id(1)

    def pre_stream(srcs, r0, buf):
        steps = []
        for idx, src in enumerate(srcs):
            def stage(idx=idx, src=src):
                cols = slice(idx * D_MODEL, (idx + 1) * D_MODEL)
                xs_ref[idx, :SUBLANES, :] = carry_ref[:, cols]
                xs_ref[idx, SUBLANES:, :] = src[r0:r0 + T, :].astype(F32)
                carry_ref[:, cols] = xs_ref[idx, T:, :]
            steps.append(stage)
            for h in hrange:
                def piece(idx=idx, h=h):
                    hs = _head_cols(h)
                    w = cw_ref[:, idx * D_MODEL + h * HEAD_DIM:idx * D_MODEL + (h + 1) * HEAD_DIM]
                    y = xs_ref[idx, SUBLANES:, hs] * w[DN_CONV_K - 1:DN_CONV_K]
                    for j in range(1, DN_CONV_K):
                        y = y + (xs_ref[idx, SUBLANES - j:SUBLANES - j + T, hs]
                                 * w[DN_CONV_K - 1 - j:DN_CONV_K - j])
                    y = _silu(y)
                    if idx < 2:
                        scale = HEAD_DIM ** -0.5 if idx == 0 else 1.0
                        y = y * (lax.rsqrt(jnp.sum(y * y, axis=-1, keepdims=True) + NORM_EPS) * scale)
                    qkv_ref[buf, idx, :, hs] = y
                steps.append(piece)
        return steps

    @pl.when(step_idx == 0)
    def _():
        carry_ref[...] = jnp.zeros_like(carry_ref)
        state_ref[...] = jnp.zeros_like(state_ref)
        for step in pre_stream((q_ref, k_ref, v_ref), 0, 0):
            step()

    ri = lax.broadcasted_iota(jnp.int32, (P, P), 0)
    ci = lax.broadcasted_iota(jnp.int32, (P, P), 1)
    same_chunk = (ri // C) == (ci // C)
    causal = same_chunk & (ci <= ri)
    strict = same_chunk & (ci < ri)
    eye = (ci == ri).astype(F32)
    level_masks = []
    bs = 1
    while bs < C:
        same = (ri // (2 * bs)) == (ci // (2 * bs))
        level_masks.append(same & ((ri % (2 * bs)) >= bs) & ((ci % (2 * bs)) < bs))
        bs *= 2
    first_half = lax.broadcasted_iota(jnp.int32, (P, 1), 0) < C
    pair0 = step_idx * (2 * T // P)
    nw = nw_ref[...]

    def a_stream(buf, r0):
        probs = [(sl, h) for sl in range(T // P) for h in hrange]
        n = range(len(probs))
        rows = [slice(sl * P, (sl + 1) * P) for sl in range(T // P)]
        ld = lambda idx, i: qkv_ref[buf, idx, rows[probs[i][0]], _head_cols(probs[i][1])]
        st = {}

        def s_decay():
            gbs = [gates_ref[r0 + sl * P:r0 + (sl + 1) * P, :] for sl in range(T // P)]
            st["beta"] = [gbs[sl][:, h:h + 1] for sl, h in probs]
            st["gc"] = [gbs[sl][:, HEADS + h:HEADS + h + 1] for sl, h in probs]
            gr = [grow_ref[0, h, pl.ds(pair0 + r0 // P + sl, 1), :] for sl, h in probs]
            st["decay"] = [jnp.where(causal, jnp.exp(jnp.where(causal, st["gc"][i] - gr[i], 0.0)), 0.0)
                           for i in n]

        def s_a():
            k = [ld(1, i) for i in n]
            st["kb"] = [k[i] * st["beta"][i] for i in n]
            st["k16"] = [k[i].astype(BF16) for i in n]
            st["a"] = [jnp.where(strict, _dot_nt(st["kb"][i].astype(BF16), st["k16"][i]) * st["decay"][i], 0.0)
                       for i in n]
            st["t"] = [eye - jnp.where(level_masks[0], st["a"][i], 0.0) for i in n]

        def s_tx(m):
            st["t16"] = [st["t"][i].astype(BF16) for i in n]
            st["tx"] = [_dot(st["t16"][i], jnp.where(m, st["a"][i], 0.0).astype(BF16)) for i in n]

        def s_t():
            st["t"] = [st["t"][i] - _dot(st["tx"][i].astype(BF16), st["t16"][i]) for i in n]

        def s_uw():
            st["eg"] = [jnp.exp(st["gc"][i]) for i in n]
            rhs = [jnp.concatenate([ld(2, i) * st["beta"][i], st["kb"][i] * st["eg"][i]], axis=1).astype(BF16)
                   for i in n]
            st["uw"] = [_dot(st["t"][i].astype(BF16), rhs[i]) for i in n]

        def s_qk():
            st["qk"] = [(_dot_nt(ld(0, i).astype(BF16), st["k16"][i]) * st["decay"][i]).astype(BF16) for i in n]

        def s_store():
            for i, (sl, h) in enumerate(probs):
                r, gc = rows[sl], st["gc"][i]
                glast = jnp.where(first_half, gc[C - 1:C, :], gc[P - 1:P, :])
                u_ref[buf, r, _head_cols(h)] = st["uw"][i][:, :HEAD_DIM]
                w_ref[buf, r, _head_cols(h)] = st["uw"][i][:, HEAD_DIM:].astype(BF16)
                qd_ref[buf, r, _head_cols(h)] = (ld(0, i) * st["eg"][i]).astype(BF16)
                kd_ref[buf, r, _head_cols(h)] = (ld(1, i) * jnp.exp(glast - gc)).astype(BF16)
                qk_ref[buf, sl * P:sl * P + C, h * C:(h + 1) * C] = st["qk"][i][:C, :C]
                qk_ref[buf, sl * P + C:(sl + 1) * P, h * C:(h + 1) * C] = st["qk"][i][C:, C:]

        steps = [s_decay, s_a]
        for m in level_masks[1:]:
            steps += [functools.partial(s_tx, m), s_t]
        return steps + [s_uw, s_qk, s_store]

    def b_stream(buf, r0):
        st = {}
        steps = []
        for c in range(T // C):
            rc = slice(c * C, (c + 1) * C)
            rg = slice(r0 + c * C, r0 + (c + 1) * C)

            def s_ws(rc=rc):
                st["s"] = [state_ref[h] for h in hrange]
                lhs = [jnp.concatenate([w_ref[buf, rc, _head_cols(h)], qd_ref[buf, rc, _head_cols(h)]], axis=0)
                       for h in hrange]
                st["ws"] = [_dot(lhs[h], st["s"][h].astype(BF16)) for h in hrange]

            def s_state(rc=rc, rg=rg):
                gb = gates_ref[rg, :]
                vn16 = [(u_ref[buf, rc, _head_cols(h)] - st["ws"][h][:C]).astype(BF16) for h in hrange]
                st["o"] = [st["ws"][h][C:] + _dot(qk_ref[buf, rc, h * C:(h + 1) * C], vn16[h]) for h in hrange]
                for h in hrange:
                    cd = jnp.exp(gb[C - 1:C, HEADS + h:HEADS + h + 1])
                    state_ref[h] = st["s"][h] * cd + _dot_tn(kd_ref[buf, rc, _head_cols(h)], vn16[h])

            def s_out(rg=rg):
                for h in hrange:
                    o = st["o"][h]
                    on = o * lax.rsqrt(jnp.mean(o * o, axis=-1, keepdims=True) + NORM_EPS) * nw
                    o_ref[rg, _head_cols(h)] = (on * _silu(z_ref[rg, _head_cols(h)].astype(F32))).astype(BF16)

            steps += [s_ws, s_state, s_out]
        return steps

    _interleave(a_stream(0, 0), pre_stream((q_ref, k_ref, v_ref), T, 1))
    _interleave(a_stream(1, T), b_stream(0, 0))
    _interleave(b_stream(1, T), pre_stream((qx_ref, kx_ref, vx_ref), 0, 0))


def _deltanet(proj, gates, g_row, conv_w, norm_w, batch, seq):
    t = proj.shape[0]
    T = DN_TILE
    L = 2 * T
    nl = seq // L
    blk = lambda col: pl.BlockSpec((L, D_MODEL), lambda b, s, col=col: (b * nl + s, col))
    nxt = lambda col: pl.BlockSpec(
        (T, D_MODEL), lambda b, s, col=col: (2 * b * nl + jnp.minimum(2 * s + 2, 2 * nl - 1), col))
    return pl.pallas_call(
        _deltanet_kernel,
        grid=(batch, nl),
        in_specs=[
            blk(COL_DQ), blk(COL_DK), blk(COL_DV), blk(COL_DZ),
            nxt(COL_DQ), nxt(COL_DK), nxt(COL_DV),
            pl.BlockSpec((L, LANES), lambda b, s: (b * nl + s, 0)),
            pl.BlockSpec((1, HEADS, seq // DN_PAIR, DN_PAIR), lambda b, s: (b, 0, 0, 0)),
            pl.BlockSpec((DN_CONV_K, 3 * D_MODEL), lambda b, s: (0, 0)),
            pl.BlockSpec((1, HEAD_DIM), lambda b, s: (0, 0)),
        ],
        out_specs=pl.BlockSpec((L, D_MODEL), lambda b, s: (b * nl + s, 0)),
        out_shape=jax.ShapeDtypeStruct((t, D_MODEL), BF16),
        scratch_shapes=[
            pltpu.VMEM((SUBLANES, 3 * D_MODEL), F32),
            pltpu.VMEM((HEADS, HEAD_DIM, HEAD_DIM), F32),
            pltpu.VMEM((3, T + SUBLANES, D_MODEL), F32),
            pltpu.VMEM((2, 3, T, D_MODEL), F32),
            pltpu.VMEM((2, T, D_MODEL), F32),
            pltpu.VMEM((2, T, D_MODEL), BF16),
            pltpu.VMEM((2, T, D_MODEL), BF16),
            pltpu.VMEM((2, T, D_MODEL), BF16),
            pltpu.VMEM((2, T, HEADS * DN_CHUNK), BF16),
        ],
        compiler_params=_cparams("arbitrary", "arbitrary"),
        name="deltanet",
    )(proj, proj, proj, proj, proj, proj, proj, gates, g_row, conv_w, norm_w)


def _moba_prep_kernel(q_ref, k_ref, v_ref, cos_ref, sin_ref, qa_ref, ka_ref, vt_ref, kmean_ref):
    i = pl.program_id(1)
    BS = MOBA_BLOCK
    nbp = kmean_ref.shape[1]
    hrange = range(HEADS)

    @pl.when(i == 0)
    def _():
        kmean_ref[...] = jnp.zeros_like(kmean_ref)

    cos = cos_ref[...]
    sin = sin_ref[...]
    rope = lambda x: x * cos + pltpu.roll(x, HEAD_DIM // 2, axis=1) * sin
    q = [rope(q_ref[:, _head_cols(h)].astype(F32)) for h in hrange]
    k = [rope(k_ref[:, _head_cols(h)].astype(F32)) for h in hrange]

    gate = [lax.dot_general(kmean_ref[h], q[h], (((1,), (1,)), ((), ())),
                            precision=lax.Precision.HIGHEST, preferred_element_type=F32)
            for h in hrange]
    blk = lax.broadcasted_iota(jnp.int32, (nbp, BS), 0)
    blk_f = blk.astype(F32)
    neg_inf = jnp.float32(-jnp.inf)
    g = [jnp.where(blk < i, gate[h], neg_inf) for h in hrange]
    sel = [blk == i for h in hrange]
    for _ in range(MOBA_TOPK):
        m = [jnp.max(g[h], axis=0, keepdims=True) for h in hrange]
        first = [jnp.min(jnp.where((g[h] == m[h]) & (g[h] > neg_inf), blk_f, float(nbp)),
                         axis=0, keepdims=True) for h in hrange]
        pick = [blk_f == first[h] for h in hrange]
        sel = [sel[h] | pick[h] for h in hrange]
        g = [jnp.where(pick[h], neg_inf, g[h]) for h in hrange]

    lane = lax.broadcasted_iota(jnp.int32, (BS, LANES), 1)
    onehot = (lane == i).astype(BF16)
    qt = [(q[h] * (HEAD_DIM ** -0.5 * LOG2_E)).T.astype(BF16) for h in hrange]
    vt = [v_ref[:, _head_cols(h)].astype(F32).T.astype(BF16) for h in hrange]
    for h in hrange:
        qa_ref[0, h, :HEAD_DIM, :] = qt[h]
        qa_ref[0, h, HEAD_DIM:HEAD_DIM + nbp, :] = jnp.where(sel[h], 0.0, MASK_NEG).astype(BF16)
        if nbp < HEAD_DIM:
            qa_ref[0, h, HEAD_DIM + nbp:, :] = jnp.zeros((HEAD_DIM - nbp, BS), BF16)
        ka_ref[0, h, :, :HEAD_DIM] = k[h].astype(BF16)
        ka_ref[0, h, :, HEAD_DIM:] = onehot
        vt_ref[0, h, 0] = vt[h]
        kmean_ref[h, pl.ds(i, 1), :] = jnp.mean(k[h], axis=0, keepdims=True)


def _moba_prep(proj, cos_full, sin_signed, batch, seq):
    nb = seq // MOBA_BLOCK
    nbp = -(-nb // 16) * 16
    blk = lambda col: pl.BlockSpec((MOBA_BLOCK, D_MODEL), lambda b, i, col=col: (b * nb + i, col))
    tab = pl.BlockSpec((MOBA_BLOCK, HEAD_DIM), lambda b, i: (i, 0))
    return pl.pallas_call(
        _moba_prep_kernel,
        grid=(batch, nb),
        in_specs=[blk(COL_MQ), blk(COL_MK), blk(COL_MV), tab, tab],
        out_specs=[pl.BlockSpec((1, HEADS, 2 * HEAD_DIM, MOBA_BLOCK), lambda b, i: (b, 0, 0, i)),
                   pl.BlockSpec((1, HEADS, MOBA_BLOCK, 2 * HEAD_DIM), lambda b, i: (b, 0, i, 0)),
                   pl.BlockSpec((1, HEADS, 1, HEAD_DIM, MOBA_BLOCK),
                                lambda b, i: (b, 0, i // MOBA_GROUP, 0, i % MOBA_GROUP))],
        out_shape=[jax.ShapeDtypeStruct((batch, HEADS, 2 * HEAD_DIM, seq), BF16),
                   jax.ShapeDtypeStruct((batch, HEADS, seq, 2 * HEAD_DIM), BF16),
                   jax.ShapeDtypeStruct((batch, HEADS, nb // MOBA_GROUP, HEAD_DIM,
                                         MOBA_GROUP * MOBA_BLOCK), BF16)],
        scratch_shapes=[pltpu.VMEM((HEADS, nbp, HEAD_DIM), F32)],
        compiler_params=_cparams("parallel", "arbitrary"),
        name="moba_prep",
    )(proj, proj, proj, cos_full, sin_signed)


def _moba_attn_kernel(qa_ref, ka_ref, vt_ref, o_ref, sa_ref, sb_ref):
    tq_idx = pl.program_id(2)
    TK = MOBA_GROUP * MOBA_BLOCK
    TQ = MOBA_QTILE_KEYS * TK
    qt = qa_ref[0, 0]

    def scores(j):
        return _dot(ka_ref[0, 0, pl.ds(pl.multiple_of(j * TK, TK), TK), :], qt)

    def attend(s, j, m, l, acc):
        m_new = jnp.maximum(m, jnp.max(s, axis=0, keepdims=True))
        alpha = jnp.exp2(m - m_new)
        p = jnp.exp2(s - m_new)
        l = alpha * l + jnp.sum(p, axis=0, keepdims=True)
        acc = alpha * acc + _dot(vt_ref[0, 0, j], p.astype(BF16))
        return m_new, l, acc

    def step(g, cur_ref, nxt_ref, carry):
        nxt_ref[...] = scores(g + 1)
        return attend(cur_ref[...], g, *carry)

    assert MOBA_QTILE_KEYS == 2
    sa_ref[...] = scores(0)
    carry = (jnp.full((1, TQ), MASK_NEG, F32), jnp.zeros((1, TQ), F32), jnp.zeros((HEAD_DIM, TQ), F32))

    def two_steps(h, carry):
        carry = step(2 * h, sa_ref, sb_ref, carry)
        return step(2 * h + 1, sb_ref, sa_ref, carry)

    carry = lax.fori_loop(0, tq_idx, two_steps, carry)

    own0, own1 = 2 * tq_idx, 2 * tq_idx + 1
    half = slice(TK, TQ)
    tri = (lax.broadcasted_iota(jnp.int32, (TK, TK), 0) <= lax.broadcasted_iota(jnp.int32, (TK, TK), 1))
    sb_ref[:, half] = _dot(ka_ref[0, 0, pl.ds(pl.multiple_of(own1 * TK, TK), TK), :], qt[:, half])
    s0 = sa_ref[...]
    s0 = jnp.concatenate([jnp.where(tri, s0[:, :TK], MASK_NEG), s0[:, half]], axis=1)
    m, l, acc = attend(s0, own0, *carry)
    m2, l2, acc2 = attend(jnp.where(tri, sb_ref[:, half], MASK_NEG), own1, m[:, half], l[:, half], acc[:, half])
    l = jnp.concatenate([l[:, :TK], l2], axis=1)
    acc = jnp.concatenate([acc[:, :TK], acc2], axis=1)
    o_ref[...] = (acc / l).T.astype(BF16)


def _moba_attn(q_aug_t, k_aug, v_t, batch, seq):
    tk = MOBA_GROUP * MOBA_BLOCK
    tq = MOBA_QTILE_KEYS * tk
    return pl.pallas_call(
        _moba_attn_kernel,
        grid=(batch, HEADS, seq // tq),
        in_specs=[
            pl.BlockSpec((1, 1, 2 * HEAD_DIM, tq), lambda b, h, i: (b, h, 0, i)),
            pl.BlockSpec((1, 1, seq, 2 * HEAD_DIM), lambda b, h, i: (b, h, 0, 0)),
            pl.BlockSpec((1, 1, seq // tk, HEAD_DIM, tk), lambda b, h, i: (b, h, 0, 0, 0)),
        ],
        out_specs=pl.BlockSpec((tq, HEAD_DIM), lambda b, h, i: (b * (seq // tq) + i, h)),
        out_shape=jax.ShapeDtypeStruct((batch * seq, D_MODEL), BF16),
        scratch_shapes=[pltpu.VMEM((tk, tq), F32), pltpu.VMEM((tk, tq), F32)],
        compiler_params=_cparams("parallel", "parallel", "arbitrary"),
        name="moba_attn",
    )(q_aug_t, k_aug, v_t)


def _merge_kernel(ax_ref, ac_ref, ab_ref, hx_ref, hc_ref, cw_ref, ydn_ref, ymb_ref,
                  ga_ref, gd_ref, gm_ref, x_ref, wa_ref, wd_ref, wm_ref, wo_ref, o_ref, *, seq):
    tm = ax_ref.shape[0]
    f32 = lambda ref: ref[...].astype(F32)
    first = (pl.program_id(0) * tm) % seq == 0
    halo = jnp.where(first, 0.0, (f32(hx_ref) * f32(hc_ref))[HALO_ROWS - SUBLANES:])
    p = f32(ax_ref) * f32(ac_ref)
    y_a = f32(ab_ref) * _causal_conv(p, halo, cw_ref[...], CONV_A_K)
    merged = (jax.nn.sigmoid(f32(ga_ref)) * _dot(y_a.astype(BF16), wa_ref[...])
              + jax.nn.sigmoid(f32(gd_ref)) * _dot(ydn_ref[...], wd_ref[...])
              + jax.nn.sigmoid(f32(gm_ref)) * _dot(ymb_ref[...], wm_ref[...]))
    o_ref[...] = x_ref[...] + _dot(merged.astype(BF16), wo_ref[...])


def _merge(proj_a, proj_b, conv_w, y_dn, y_mb, x, wa, wd, wm, wo, layer, seq, tm):
    t = x.shape[0]
    blk = lambda col: pl.BlockSpec((tm, D_MODEL), lambda i, col=col: (i, col))
    halo = lambda col: pl.BlockSpec(
        (HALO_ROWS, D_MODEL), lambda i, col=col: (jnp.maximum(i * (tm // HALO_ROWS) - 1, 0), col))
    row = pl.BlockSpec((tm, D_MODEL), lambda i: (i, 0))
    wspec = pl.BlockSpec((None, D_MODEL, D_MODEL), lambda i: (layer, 0, 0))
    return pl.pallas_call(
        functools.partial(_merge_kernel, seq=seq),
        grid=(t // tm,),
        in_specs=[blk(COL_AX), blk(COL_AC), blk(COL_AB), halo(COL_AX), halo(COL_AC),
                  pl.BlockSpec((CONV_A_K, D_MODEL), lambda i: (0, 0)),
                  row, row, blk(COL_GATE), blk(COL_GATE + 1), blk(COL_GATE + 2), row,
                  wspec, wspec, wspec, wspec],
        out_specs=row,
        out_shape=jax.ShapeDtypeStruct((t, D_MODEL), F32),
        compiler_params=_cparams("parallel"),
        name="merge",
    )(proj_a, proj_a, proj_a, proj_a, proj_a, conv_w, y_dn, y_mb, proj_b, proj_b, proj_b, x,
      wa, wd, wm, wo)


def _rms_scale(x, gain):
    return x * lax.rsqrt(jnp.mean(x * x, axis=-1, keepdims=True) + NORM_EPS) * gain


def _ffn_kernel(x_ref, g_ref, wg_ref, wu_ref, wd_ref, fg_ref, o_ref, h_ref, acc_ref, *, final_norm):
    j = pl.program_id(1)

    @pl.when(j == 0)
    def _():
        x = x_ref[...]
        h_ref[...] = _rms_scale(x, g_ref[...]).astype(BF16)
        acc_ref[...] = x

    h = h_ref[...]
    act = _silu(_dot(h, wg_ref[...])) * _dot(h, wu_ref[...])
    acc_ref[...] += _dot(act.astype(BF16), wd_ref[...])

    @pl.when(j == pl.num_programs(1) - 1)
    def _():
        y = acc_ref[...]
        o_ref[...] = _rms_scale(y, fg_ref[...]) if final_norm else y


def _ffn(x, gain, w_gate_up, w_down, layer, final_gain, final_norm, tm, th):
    t, d = x.shape
    hid = w_down.shape[1]
    nh = hid // th
    return pl.pallas_call(
        functools.partial(_ffn_kernel, final_norm=final_norm),
        grid=(t // tm, nh),
        in_specs=[
            pl.BlockSpec((tm, d), lambda i, j: (i, 0)),
            pl.BlockSpec((1, d), lambda i, j: (0, 0)),
            pl.BlockSpec((None, d, th), lambda i, j: (layer, 0, j)),
            pl.BlockSpec((None, d, th), lambda i, j: (layer, 0, j + nh)),
            pl.BlockSpec((None, th, d), lambda i, j: (layer, j, 0)),
            pl.BlockSpec((1, d), lambda i, j: (0, 0)),
        ],
        out_specs=pl.BlockSpec((tm, d), lambda i, j: (i, 0)),
        out_shape=jax.ShapeDtypeStruct((t, d), F32),
        scratch_shapes=[pltpu.VMEM((tm, d), BF16), pltpu.VMEM((tm, d), F32)],
        compiler_params=_cparams("parallel", "arbitrary"),
        name="ffn",
    )(x, gain, w_gate_up, w_gate_up, w_down, final_gain)


def _largest_tile(n, cap):
    t = cap
    while n % t:
        t //= 2
    return t


def _prep_w_in(w16, layer):
    o_b = N_PROJ_A + 2 * HEADS
    small = jnp.pad(w16[layer, :, N_PROJ_A:o_b], ((0, 0), (0, N_SMALL - 2 * HEADS)))
    return w16[layer, :, o_b:o_b + N_PROJ_B], small


def _lane_row(vals, offset):
    row = jnp.zeros((1, LANES), F32)
    return row.at[0, offset:offset + vals.shape[0]].set(vals.astype(F32))


def kernel(x, attn_norm, w_in, conv_a_w, dn_conv_w, dn_a_log, dn_dt_bias, dn_norm, w_br_a, w_br_dn,
           w_br_moba, w_out, ffn_norm, w_gate_up, w_down, final_norm):
    batch, seq, d = x.shape
    depth = attn_norm.shape[0]
    assert d == D_MODEL and seq // MOBA_BLOCK <= LANES
    assert seq % (MOBA_QTILE_KEYS * MOBA_GROUP * MOBA_BLOCK) == 0
    t = batch * seq
    tm_big = _largest_tile(t, 1024)
    tm_huge = _largest_tile(t, 2048)
    assert seq % (2 * DN_TILE) == 0 and DN_TILE % DN_PAIR == 0 and (seq // DN_PAIR) % SUBLANES == 0

    inv = 1.0 / (ROPE_THETA ** (jnp.arange(0, HEAD_DIM, 2, dtype=F32) / HEAD_DIM))
    ang = jnp.arange(seq, dtype=F32)[:, None] * inv[None, :]
    cos_full = jnp.concatenate([jnp.cos(ang), jnp.cos(ang)], axis=-1)
    sin_signed = jnp.concatenate([-jnp.sin(ang), jnp.sin(ang)], axis=-1)

    w_in16, w_gate_up16, w_down16 = (w.astype(BF16) for w in (w_in, w_gate_up, w_down))
    w_br16 = [w.astype(BF16) for w in (w_br_a, w_br_dn, w_br_moba, w_out)]

    xf = x.reshape(t, d)
    for l in range(depth):
        w_b, w_small = _prep_w_in(w_in16, l)
        proj_a, proj_b, raw_gates = _in_proj(xf, attn_norm[l][None, :], w_in16, l, w_b, w_small,
                                             tm_huge, 1024)

        gates = _dn_gates(raw_gates, _lane_row(dn_a_log[l], HEADS), _lane_row(dn_dt_bias[l], HEADS), tm_big)
        g_row = (gates[:, HEADS:2 * HEADS].reshape(batch, seq, HEADS).transpose(0, 2, 1)
                 .reshape(batch, HEADS, seq // DN_PAIR, DN_PAIR))
        y_dn = _deltanet(proj_a, gates, g_row, dn_conv_w[l], dn_norm[l][None, :], batch, seq)

        q_aug_t, k_aug, v_t = _moba_prep(proj_b, cos_full, sin_signed, batch, seq)
        y_mb = _moba_attn(q_aug_t, k_aug, v_t, batch, seq)

        x1 = _merge(proj_a, proj_b, conv_a_w[l], y_dn, y_mb, xf, *w_br16, l, seq, _largest_tile(t, 512))
        xf = _ffn(x1, ffn_norm[l][None, :], w_gate_up16, w_down16, l, final_norm[None, :],
                  l == depth - 1, tm_huge, 256)

    return xf.reshape(batch, seq, d)
```

```python
import functools

import jax
import jax.numpy as jnp
from jax import lax
from jax.experimental import pallas as pl
from jax.experimental.pallas import tpu as pltpu

F32 = jnp.float32
BF16 = jnp.bfloat16

D_MODEL = 1024
HEADS = 8
HEAD_DIM = 128
NORM_EPS = 1e-6
CONV_A_K = 3
DN_CONV_K = 4
DN_CHUNK = 64
DN_PAIR = 2 * DN_CHUNK
DN_TILE = 256
DN_CONV_GROUP = 128
MOBA_BLOCK = 256
MOBA_TOPK = 3
MOBA_GROUP = 2
MOBA_QTILE_KEYS = 2
LOG2_E = 1.4426950408889634
ROPE_THETA = 10000.0
FFN_HIDDEN = 2816

COL_AX, COL_AC, COL_AB = 0, 1, 2
COL_DQ, COL_DK, COL_DV, COL_DZ = 3, 4, 5, 6
N_PROJ_A = 7 * D_MODEL
COL_MQ, COL_MK, COL_MV = 0, 1, 2
COL_GATE = 3
N_PROJ_B = 6 * D_MODEL
N_SMALL = 128

MASK_NEG = -1e30
VMEM_LIMIT = 56 * 1024 * 1024
SUBLANES = 8
LANES = 128
HALO_ROWS = 16


def _cparams(*sem, flags=None):
    return pltpu.CompilerParams(dimension_semantics=sem, vmem_limit_bytes=VMEM_LIMIT, flags=flags)


def _silu(x):
    return x * jax.nn.sigmoid(x)


def _dot(a, b):
    return jnp.dot(a, b, preferred_element_type=F32)


def _dot_nt(a, b):
    return lax.dot_general(a, b, (((1,), (1,)), ((), ())), preferred_element_type=F32)


def _dot_tn(a, b):
    return lax.dot_general(a, b, (((0,), (0,)), ((), ())), preferred_element_type=F32)


def _in_proj_kernel(x_ref, g_ref, wa_ref, wb_ref, ws_ref, oa_ref, ob_ref, os_ref, h_ref, *, na):
    j = pl.program_id(1)

    @pl.when(j == 0)
    def _():
        x = x_ref[...]
        ms = jnp.mean(x * x, axis=-1, keepdims=True)
        h_ref[...] = (x * lax.rsqrt(ms + NORM_EPS) * g_ref[...]).astype(BF16)
        os_ref[...] = _dot(h_ref[...], ws_ref[...])

    @pl.when(j < na)
    def _():
        oa_ref[...] = _dot(h_ref[...], wa_ref[...]).astype(BF16)

    @pl.when(j >= na)
    def _():
        ob_ref[...] = _dot(h_ref[...], wb_ref[...]).astype(BF16)


def _in_proj(x, gain, wa, wb, ws, layer, tm, tn):
    t, d = x.shape
    na, nb = wa.shape[2] // tn, wb.shape[2] // tn
    a_idx = lambda j: jnp.minimum(j, na - 1)
    b_idx = lambda j: jnp.maximum(j - na, 0)
    return pl.pallas_call(
        functools.partial(_in_proj_kernel, na=na),
        grid=(t // tm, na + nb),
        in_specs=[
            pl.BlockSpec((tm, d), lambda i, j: (i, 0)),
            pl.BlockSpec((1, d), lambda i, j: (0, 0)),
            pl.BlockSpec((None, d, tn), lambda i, j: (layer, 0, a_idx(j))),
            pl.BlockSpec((None, d, tn), lambda i, j: (layer, 0, b_idx(j))),
            pl.BlockSpec((None, d, N_SMALL), lambda i, j: (layer, 0, 0)),
        ],
        out_specs=[pl.BlockSpec((tm, tn), lambda i, j: (i, a_idx(j))),
                   pl.BlockSpec((tm, tn), lambda i, j: (i, b_idx(j))),
                   pl.BlockSpec((tm, N_SMALL), lambda i, j: (i, 0))],
        out_shape=[jax.ShapeDtypeStruct((t, wa.shape[2]), BF16),
                   jax.ShapeDtypeStruct((t, wb.shape[2]), BF16),
                   jax.ShapeDtypeStruct((t, N_SMALL), F32)],
        scratch_shapes=[pltpu.VMEM((tm, d), BF16)],
        compiler_params=_cparams("parallel", "arbitrary"),
        name="in_proj",
    )(x, gain, wa, wb, ws)


def _dn_gates_kernel(raw_ref, alog_ref, dt_ref, o_ref):
    raw = raw_ref[...]
    tm = raw.shape[0]
    xa = raw + dt_ref[...]
    softplus = jnp.maximum(xa, 0.0) + jnp.log1p(jnp.exp(-jnp.abs(xa)))
    g = -jnp.exp(alog_ref[...]) * softplus
    row = lax.broadcasted_iota(jnp.int32, (tm, LANES), 0) % DN_CHUNK
    shift = 1
    while shift < DN_CHUNK:
        g = g + jnp.where(row >= shift, pltpu.roll(g, shift, axis=0), 0.0)
        shift *= 2
    lane = lax.broadcasted_iota(jnp.int32, (tm, LANES), 1)
    o_ref[...] = jnp.where(lane < HEADS, jax.nn.sigmoid(raw), g)


def _dn_gates(raw, alog_row, dt_row, tm):
    t = raw.shape[0]
    return pl.pallas_call(
        _dn_gates_kernel,
        grid=(t // tm,),
        in_specs=[
            pl.BlockSpec((tm, LANES), lambda i: (i, 0)),
            pl.BlockSpec((1, LANES), lambda i: (0, 0)),
            pl.BlockSpec((1, LANES), lambda i: (0, 0)),
        ],
        out_specs=pl.BlockSpec((tm, LANES), lambda i: (i, 0)),
        out_shape=jax.ShapeDtypeStruct((t, LANES), F32),
        compiler_params=_cparams("parallel"),
        name="dn_gates",
    )(raw, alog_row, dt_row)


def _causal_conv(x, carry, w, k):
    xe = jnp.concatenate([carry, x], axis=0)
    y = x * w[k - 1:k]
    for j in range(1, k):
        y = y + pltpu.roll(xe, j, axis=0)[SUBLANES:] * w[k - 1 - j:k - j]
    return y


def _head_cols(h):
    return slice(h * HEAD_DIM, (h + 1) * HEAD_DIM)


def _interleave(*streams):
    order = [((k + 0.5) / len(s), si, step) for si, s in enumerate(streams) for k, step in enumerate(s)]
    for _, _, step in sorted(order, key=lambda e: e[:2]):
        step()


def _deltanet_kernel(q_ref, k_ref, v_ref, z_ref, qx_ref, kx_ref, vx_ref, gates_ref, grow_ref, cw_ref,
                     nw_ref, o_ref, carry_ref, state_ref, qkv_ref,
                     u_ref, w_ref, qd_ref, kd_ref, qk_ref):
    T = DN_TILE
    C = DN_CHUNK
    P = DN_PAIR
    hrange = range(HEADS)
    step_idx = pl.program_id(1)

    G = DN_CONV_GROUP
    sr = lax.broadcasted_iota(jnp.int32, ((DN_CONV_K - 1) * G, 2 * G), 0)
    sc = lax.broadcasted_iota(jnp.int32, ((DN_CONV_K - 1) * G, 2 * G), 1)
    shift_all = (sc == G + sr % G - (sr // G + 1)).astype(BF16)

    def pre_stream(srcs, r0, buf):
        steps = []
        for idx, src in enumerate(srcs):
            for rg in range(T // G):
                for hp in range(HEADS // 2):
                    def piece(idx=idx, src=src, rg=rg, hp=hp):
                        cols = slice(hp * 2 * HEAD_DIM, (hp + 1) * 2 * HEAD_DIM)
                        cur = src[r0 + rg * G:r0 + (rg + 1) * G, cols]
                        prev = carry_ref[idx, :, cols] if rg == 0 else src[r0 + (rg - 1) * G:r0 + rg * G, cols]
                        sh = _dot(shift_all, jnp.concatenate([prev, cur], axis=0))
                        w = cw_ref[:, idx * D_MODEL + hp * 2 * HEAD_DIM:idx * D_MODEL + (hp + 1) * 2 * HEAD_DIM]
                        y = cur.astype(F32) * w[DN_CONV_K - 1:DN_CONV_K]
                        for j in range(1, DN_CONV_K):
                            y = y + sh[(j - 1) * G:j * G] * w[DN_CONV_K - 1 - j:DN_CONV_K - j]
                        y = _silu(y)
                        for e in range(2):
                            ye = y[:, e * HEAD_DIM:(e + 1) * HEAD_DIM]
                            if idx < 2:
                                scale = HEAD_DIM ** -0.5 if idx == 0 else 1.0
                                ye = ye * (lax.rsqrt(jnp.sum(ye * ye, axis=-1, keepdims=True) + NORM_EPS) * scale)
                            qkv_ref[buf, idx, rg * G:(rg + 1) * G, _head_cols(2 * hp + e)] = ye
                    steps.append(piece)

            def save(idx=idx, src=src):
                carry_ref[idx] = src[r0 + T - G:r0 + T, :]
            steps.append(save)
        return steps

    @pl.when(step_idx == 0)
    def _():
        carry_ref[...] = jnp.zeros_like(carry_ref)
        state_ref[...] = jnp.zeros_like(state_ref)
        for step in pre_stream((q_ref, k_ref, v_ref), 0, 0):
            step()

    ri = lax.broadcasted_iota(jnp.int32, (P, P), 0)
    ci = lax.broadcasted_iota(jnp.int32, (P, P), 1)
    same_chunk = (ri // C) == (ci // C)
    causal = same_chunk & (ci <= ri)
    strict = same_chunk & (ci < ri)
    eye = (ci == ri).astype(F32)
    level_masks = []
    bs = 1
    while bs < C:
        same = (ri // (2 * bs)) == (ci // (2 * bs))
        level_masks.append(same & ((ri % (2 * bs)) >= bs) & ((ci % (2 * bs)) < bs))
        bs *= 2
    level_masks16 = [jnp.where(m, 1.0, 0.0).astype(BF16) for m in level_masks[1:]]
    first_half = lax.broadcasted_iota(jnp.int32, (P, 1), 0) < C
    pair0 = step_idx * (2 * T // P)
    nw = nw_ref[...]

    def a_stream(buf, r0):
        probs = [(sl, h) for sl in range(T // P) for h in hrange]
        n = range(len(probs))
        rows = [slice(sl * P, (sl + 1) * P) for sl in range(T // P)]
        ld = lambda idx, i: qkv_ref[buf, idx, rows[probs[i][0]], _head_cols(probs[i][1])]
        st = {}

        def s_decay():
            gbs = [gates_ref[r0 + sl * P:r0 + (sl + 1) * P, :] for sl in range(T // P)]
            st["beta"] = [gbs[sl][:, h:h + 1] for sl, h in probs]
            st["gc"] = [gbs[sl][:, HEADS + h:HEADS + h + 1] for sl, h in probs]
            gr = [grow_ref[0, h, pl.ds(pair0 + r0 // P + sl, 1), :] for sl, h in probs]
            st["decay"] = [jnp.where(causal, jnp.exp(jnp.where(causal, st["gc"][i] - gr[i], 0.0)), 0.0)
                           for i in n]

        def s_a():
            k = [ld(1, i) for i in n]
            st["kb"] = [k[i] * st["beta"][i] for i in n]
            st["k16"] = [k[i].astype(BF16) for i in n]
            st["a"] = [jnp.where(strict, _dot_nt(st["kb"][i].astype(BF16), st["k16"][i]) * st["decay"][i], 0.0)
                       for i in n]
            st["t"] = [eye - jnp.where(level_masks[0], st["a"][i], 0.0) for i in n]
            st["a16"] = [st["a"][i].astype(BF16) for i in n]

        def s_tx(m16):
            st["t16"] = [st["t"][i].astype(BF16) for i in n]
            st["tx"] = [_dot(st["t16"][i], st["a16"][i] * m16) for i in n]

        def s_t():
            st["t"] = [st["t"][i] - _dot(st["tx"][i].astype(BF16), st["t16"][i]) for i in n]

        def s_uw():
            st["eg"] = [jnp.exp(st["gc"][i]) for i in n]
            rhs = [jnp.concatenate([ld(2, i) * st["beta"][i], st["kb"][i] * st["eg"][i]], axis=1).astype(BF16)
                   for i in n]
            st["uw"] = [_dot(st["t"][i].astype(BF16), rhs[i]) for i in n]

        def s_qk():
            st["qk"] = [(_dot_nt(ld(0, i).astype(BF16), st["k16"][i]) * st["decay"][i]).astype(BF16) for i in n]

        def s_store():
            for i, (sl, h) in enumerate(probs):
                r, gc = rows[sl], st["gc"][i]
                glast = jnp.where(first_half, gc[C - 1:C, :], gc[P - 1:P, :])
                u_ref[buf, r, _head_cols(h)] = st["uw"][i][:, :HEAD_DIM]
                w_ref[buf, r, _head_cols(h)] = st["uw"][i][:, HEAD_DIM:].astype(BF16)
                qd_ref[buf, r, _head_cols(h)] = (ld(0, i) * st["eg"][i]).astype(BF16)
                kd_ref[buf, r, _head_cols(h)] = (ld(1, i) * jnp.exp(glast - gc)).astype(BF16)
                qk_ref[buf, sl * P:sl * P + C, h * C:(h + 1) * C] = st["qk"][i][:C, :C]
                qk_ref[buf, sl * P + C:(sl + 1) * P, h * C:(h + 1) * C] = st["qk"][i][C:, C:]

        steps = [s_decay, s_a]
        for m16 in level_masks16:
            steps += [functools.partial(s_tx, m16), s_t]
        return steps + [s_uw, s_qk, s_store]

    def b_stream(buf, r0):
        st = {}
        steps = []
        for c in range(T // C):
            rc = slice(c * C, (c + 1) * C)
            rg = slice(r0 + c * C, r0 + (c + 1) * C)

            def s_ws(rc=rc):
                st["s"] = [state_ref[h] for h in hrange]
                lhs = [jnp.concatenate([w_ref[buf, rc, _head_cols(h)], qd_ref[buf, rc, _head_cols(h)]], axis=0)
                       for h in hrange]
                st["ws"] = [_dot(lhs[h], st["s"][h].astype(BF16)) for h in hrange]

            def s_state(rc=rc, rg=rg):
                gb = gates_ref[rg, :]
                vn16 = [(u_ref[buf, rc, _head_cols(h)] - st["ws"][h][:C]).astype(BF16) for h in hrange]
                st["o"] = [st["ws"][h][C:] + _dot(qk_ref[buf, rc, h * C:(h + 1) * C], vn16[h]) for h in hrange]
                for h in hrange:
                    cd = jnp.exp(gb[C - 1:C, HEADS + h:HEADS + h + 1])
                    state_ref[h] = st["s"][h] * cd + _dot_tn(kd_ref[buf, rc, _head_cols(h)], vn16[h])

            def s_out(rg=rg):
                for h in hrange:
                    o = st["o"][h]
                    on = o * lax.rsqrt(jnp.mean(o * o, axis=-1, keepdims=True) + NORM_EPS) * nw
                    o_ref[rg, _head_cols(h)] = (on * _silu(z_ref[rg, _head_cols(h)].astype(F32))).astype(BF16)

            steps += [s_ws, s_state, s_out]
        return steps

    _interleave(a_stream(0, 0), pre_stream((q_ref, k_ref, v_ref), T, 1))
    _interleave(a_stream(1, T), b_stream(0, 0))
    _interleave(b_stream(1, T), pre_stream((qx_ref, kx_ref, vx_ref), 0, 0))


def _deltanet(proj, gates, g_row, conv_w, norm_w, batch, seq):
    t = proj.shape[0]
    T = DN_TILE
    L = 2 * T
    nl = seq // L
    blk = lambda col: pl.BlockSpec((L, D_MODEL), lambda b, s, col=col: (b * nl + s, col))
    nxt = lambda col: pl.BlockSpec(
        (T, D_MODEL), lambda b, s, col=col: (2 * b * nl + jnp.minimum(2 * s + 2, 2 * nl - 1), col))
    return pl.pallas_call(
        _deltanet_kernel,
        grid=(batch, nl),
        in_specs=[
            blk(COL_DQ), blk(COL_DK), blk(COL_DV), blk(COL_DZ),
            nxt(COL_DQ), nxt(COL_DK), nxt(COL_DV),
            pl.BlockSpec((L, LANES), lambda b, s: (b * nl + s, 0)),
            pl.BlockSpec((1, HEADS, seq // DN_PAIR, DN_PAIR), lambda b, s: (b, 0, 0, 0)),
            pl.BlockSpec((DN_CONV_K, 3 * D_MODEL), lambda b, s: (0, 0)),
            pl.BlockSpec((1, HEAD_DIM), lambda b, s: (0, 0)),
        ],
        out_specs=pl.BlockSpec((L, D_MODEL), lambda b, s: (b * nl + s, 0)),
        out_shape=jax.ShapeDtypeStruct((t, D_MODEL), BF16),
        scratch_shapes=[
            pltpu.VMEM((3, DN_CONV_GROUP, D_MODEL), BF16),
            pltpu.VMEM((HEADS, HEAD_DIM, HEAD_DIM), F32),
            pltpu.VMEM((2, 3, T, D_MODEL), F32),
            pltpu.VMEM((2, T, D_MODEL), F32),
            pltpu.VMEM((2, T, D_MODEL), BF16),
            pltpu.VMEM((2, T, D_MODEL), BF16),
            pltpu.VMEM((2, T, D_MODEL), BF16),
            pltpu.VMEM((2, T, HEADS * DN_CHUNK), BF16),
        ],
        compiler_params=_cparams("arbitrary", "arbitrary"),
        name="deltanet",
    )(proj, proj, proj, proj, proj, proj, proj, gates, g_row, conv_w, norm_w)


def _moba_prep_kernel(q_ref, k_ref, v_ref, cos_ref, sin_ref, qa_ref, ka_ref, vt_ref, kmean_ref):
    i = pl.program_id(1)
    BS = MOBA_BLOCK
    nbp = kmean_ref.shape[1]
    hrange = range(HEADS)

    @pl.when(i == 0)
    def _():
        kmean_ref[...] = jnp.zeros_like(kmean_ref)

    cos = cos_ref[...]
    sin = sin_ref[...]
    rope = lambda x: x * cos + pltpu.roll(x, HEAD_DIM // 2, axis=1) * sin
    q = [rope(q_ref[:, _head_cols(h)].astype(F32)) for h in hrange]
    k = [rope(k_ref[:, _head_cols(h)].astype(F32)) for h in hrange]

    gate = [lax.dot_general(kmean_ref[h], q[h], (((1,), (1,)), ((), ())),
                            precision=lax.Precision.HIGHEST, preferred_element_type=F32)
            for h in hrange]
    blk = lax.broadcasted_iota(jnp.int32, (nbp, BS), 0)
    blk_f = blk.astype(F32)
    neg_inf = jnp.float32(-jnp.inf)
    g = [jnp.where(blk < i, gate[h], neg_inf) for h in hrange]
    sel = [blk == i for h in hrange]
    for _ in range(MOBA_TOPK):
        m = [jnp.max(g[h], axis=0, keepdims=True) for h in hrange]
        first = [jnp.min(jnp.where((g[h] == m[h]) & (g[h] > neg_inf), blk_f, float(nbp)),
                         axis=0, keepdims=True) for h in hrange]
        pick = [blk_f == first[h] for h in hrange]
        sel = [sel[h] | pick[h] for h in hrange]
        g = [jnp.where(pick[h], neg_inf, g[h]) for h in hrange]

    lane = lax.broadcasted_iota(jnp.int32, (BS, LANES), 1)
    onehot = (lane == i).astype(BF16)
    qt = [(q[h] * (HEAD_DIM ** -0.5 * LOG2_E)).T.astype(BF16) for h in hrange]
    vt = [v_ref[:, _head_cols(h)].astype(F32).T.astype(BF16) for h in hrange]
    for h in hrange:
        qa_ref[0, h, :HEAD_DIM, :] = qt[h]
        qa_ref[0, h, HEAD_DIM:HEAD_DIM + nbp, :] = jnp.where(sel[h], 0.0, MASK_NEG).astype(BF16)
        if nbp < HEAD_DIM:
            qa_ref[0, h, HEAD_DIM + nbp:, :] = jnp.zeros((HEAD_DIM - nbp, BS), BF16)
        ka_ref[0, h, :, :HEAD_DIM] = k[h].astype(BF16)
        ka_ref[0, h, :, HEAD_DIM:] = onehot
        vt_ref[0, h, 0] = vt[h]
        kmean_ref[h, pl.ds(i, 1), :] = jnp.mean(k[h], axis=0, keepdims=True)


def _moba_prep(proj, cos_full, sin_signed, batch, seq):
    nb = seq // MOBA_BLOCK
    nbp = -(-nb // 16) * 16
    blk = lambda col: pl.BlockSpec((MOBA_BLOCK, D_MODEL), lambda b, i, col=col: (b * nb + i, col))
    tab = pl.BlockSpec((MOBA_BLOCK, HEAD_DIM), lambda b, i: (i, 0))
    return pl.pallas_call(
        _moba_prep_kernel,
        grid=(batch, nb),
        in_specs=[blk(COL_MQ), blk(COL_MK), blk(COL_MV), tab, tab],
        out_specs=[pl.BlockSpec((1, HEADS, 2 * HEAD_DIM, MOBA_BLOCK), lambda b, i: (b, 0, 0, i)),
                   pl.BlockSpec((1, HEADS, MOBA_BLOCK, 2 * HEAD_DIM), lambda b, i: (b, 0, i, 0)),
                   pl.BlockSpec((1, HEADS, 1, HEAD_DIM, MOBA_BLOCK),
                                lambda b, i: (b, 0, i // MOBA_GROUP, 0, i % MOBA_GROUP))],
        out_shape=[jax.ShapeDtypeStruct((batch, HEADS, 2 * HEAD_DIM, seq), BF16),
                   jax.ShapeDtypeStruct((batch, HEADS, seq, 2 * HEAD_DIM), BF16),
                   jax.ShapeDtypeStruct((batch, HEADS, nb // MOBA_GROUP, HEAD_DIM,
                                         MOBA_GROUP * MOBA_BLOCK), BF16)],
        scratch_shapes=[pltpu.VMEM((HEADS, nbp, HEAD_DIM), F32)],
        compiler_params=_cparams("parallel", "arbitrary"),
        name="moba_prep",
    )(proj, proj, proj, cos_full, sin_signed)


def _moba_attn_kernel(qa_ref, ka_ref, vt_ref, o_ref, sa_ref, sb_ref):
    tq_idx = pl.program_id(2)
    TK = MOBA_GROUP * MOBA_BLOCK
    TQ = MOBA_QTILE_KEYS * TK
    qt = qa_ref[0, 0]

    def scores(j):
        return _dot(ka_ref[0, 0, pl.ds(pl.multiple_of(j * TK, TK), TK), :], qt)

    def attend(s, j, m, l, acc):
        m_new = jnp.maximum(m, jnp.max(s, axis=0, keepdims=True))
        alpha = jnp.exp2(m - m_new)
        p = jnp.exp2(s - m_new)
        l = alpha * l + jnp.sum(p, axis=0, keepdims=True)
        acc = alpha * acc + _dot(vt_ref[0, 0, j], p.astype(BF16))
        return m_new, l, acc

    def step(g, cur_ref, nxt_ref, carry):
        nxt_ref[...] = scores(g + 1)
        return attend(cur_ref[...], g, *carry)

    assert MOBA_QTILE_KEYS == 2
    sa_ref[...] = scores(0)
    carry = (jnp.full((1, TQ), MASK_NEG, F32), jnp.zeros((1, TQ), F32), jnp.zeros((HEAD_DIM, TQ), F32))

    def two_steps(h, carry):
        carry = step(2 * h, sa_ref, sb_ref, carry)
        return step(2 * h + 1, sb_ref, sa_ref, carry)

    carry = lax.fori_loop(0, tq_idx, two_steps, carry)

    own0, own1 = 2 * tq_idx, 2 * tq_idx + 1
    half = slice(TK, TQ)
    tri = (lax.broadcasted_iota(jnp.int32, (TK, TK), 0) <= lax.broadcasted_iota(jnp.int32, (TK, TK), 1))
    sb_ref[:, half] = _dot(ka_ref[0, 0, pl.ds(pl.multiple_of(own1 * TK, TK), TK), :], qt[:, half])
    s0 = sa_ref[...]
    s0 = jnp.concatenate([jnp.where(tri, s0[:, :TK], MASK_NEG), s0[:, half]], axis=1)
    m, l, acc = attend(s0, own0, *carry)
    m2, l2, acc2 = attend(jnp.where(tri, sb_ref[:, half], MASK_NEG), own1, m[:, half], l[:, half], acc[:, half])
    l = jnp.concatenate([l[:, :TK], l2], axis=1)
    acc = jnp.concatenate([acc[:, :TK], acc2], axis=1)
    o_ref[...] = (acc / l).T.astype(BF16)


def _moba_attn(q_aug_t, k_aug, v_t, batch, seq):
    tk = MOBA_GROUP * MOBA_BLOCK
    tq = MOBA_QTILE_KEYS * tk
    return pl.pallas_call(
        _moba_attn_kernel,
        grid=(batch, HEADS, seq // tq),
        in_specs=[
            pl.BlockSpec((1, 1, 2 * HEAD_DIM, tq), lambda b, h, i: (b, h, 0, i)),
            pl.BlockSpec((1, 1, seq, 2 * HEAD_DIM), lambda b, h, i: (b, h, 0, 0)),
            pl.BlockSpec((1, 1, seq // tk, HEAD_DIM, tk), lambda b, h, i: (b, h, 0, 0, 0)),
        ],
        out_specs=pl.BlockSpec((tq, HEAD_DIM), lambda b, h, i: (b * (seq // tq) + i, h)),
        out_shape=jax.ShapeDtypeStruct((batch * seq, D_MODEL), BF16),
        scratch_shapes=[pltpu.VMEM((tk, tq), F32), pltpu.VMEM((tk, tq), F32)],
        compiler_params=_cparams("parallel", "parallel", "arbitrary"),
        name="moba_attn",
    )(q_aug_t, k_aug, v_t)


def _merge_kernel(ax_ref, ac_ref, ab_ref, hx_ref, hc_ref, cw_ref, ydn_ref, ymb_ref,
                  ga_ref, gd_ref, gm_ref, x_ref, wa_ref, wd_ref, wm_ref, wo_ref, o_ref, *, seq):
    tm = ax_ref.shape[0]
    f32 = lambda ref: ref[...].astype(F32)
    first = (pl.program_id(0) * tm) % seq == 0
    halo = jnp.where(first, 0.0, (f32(hx_ref) * f32(hc_ref))[HALO_ROWS - SUBLANES:])
    p = f32(ax_ref) * f32(ac_ref)
    y_a = f32(ab_ref) * _causal_conv(p, halo, cw_ref[...], CONV_A_K)
    merged = (jax.nn.sigmoid(f32(ga_ref)) * _dot(y_a.astype(BF16), wa_ref[...])
              + jax.nn.sigmoid(f32(gd_ref)) * _dot(ydn_ref[...], wd_ref[...])
              + jax.nn.sigmoid(f32(gm_ref)) * _dot(ymb_ref[...], wm_ref[...]))
    o_ref[...] = x_ref[...] + _dot(merged.astype(BF16), wo_ref[...])


def _merge(proj_a, proj_b, conv_w, y_dn, y_mb, x, wa, wd, wm, wo, layer, seq, tm):
    t = x.shape[0]
    blk = lambda col: pl.BlockSpec((tm, D_MODEL), lambda i, col=col: (i, col))
    halo = lambda col: pl.BlockSpec(
        (HALO_ROWS, D_MODEL), lambda i, col=col: (jnp.maximum(i * (tm // HALO_ROWS) - 1, 0), col))
    row = pl.BlockSpec((tm, D_MODEL), lambda i: (i, 0))
    wspec = pl.BlockSpec((None, D_MODEL, D_MODEL), lambda i: (layer, 0, 0))
    return pl.pallas_call(
        functools.partial(_merge_kernel, seq=seq),
        grid=(t // tm,),
        in_specs=[blk(COL_AX), blk(COL_AC), blk(COL_AB), halo(COL_AX), halo(COL_AC),
                  pl.BlockSpec((CONV_A_K, D_MODEL), lambda i: (0, 0)),
                  row, row, blk(COL_GATE), blk(COL_GATE + 1), blk(COL_GATE + 2), row,
                  wspec, wspec, wspec, wspec],
        out_specs=row,
        out_shape=jax.ShapeDtypeStruct((t, D_MODEL), F32),
        compiler_params=_cparams("parallel"),
        name="merge",
    )(proj_a, proj_a, proj_a, proj_a, proj_a, conv_w, y_dn, y_mb, proj_b, proj_b, proj_b, x,
      wa, wd, wm, wo)


def _rms_scale(x, gain):
    return x * lax.rsqrt(jnp.mean(x * x, axis=-1, keepdims=True) + NORM_EPS) * gain


def _ffn_kernel(x_ref, g_ref, wg_ref, wu_ref, wd_ref, fg_ref, o_ref, h_ref, acc_ref, *, final_norm):
    j = pl.program_id(1)

    @pl.when(j == 0)
    def _():
        x = x_ref[...]
        h_ref[...] = _rms_scale(x, g_ref[...]).astype(BF16)
        acc_ref[...] = x

    h = h_ref[...]
    act = _silu(_dot(h, wg_ref[...])) * _dot(h, wu_ref[...])
    acc_ref[...] += _dot(act.astype(BF16), wd_ref[...])

    @pl.when(j == pl.num_programs(1) - 1)
    def _():
        y = acc_ref[...]
        o_ref[...] = _rms_scale(y, fg_ref[...]) if final_norm else y


def _ffn(x, gain, w_gate_up, w_down, layer, final_gain, final_norm, tm, th):
    t, d = x.shape
    hid = w_down.shape[1]
    nh = hid // th
    return pl.pallas_call(
        functools.partial(_ffn_kernel, final_norm=final_norm),
        grid=(t // tm, nh),
        in_specs=[
            pl.BlockSpec((tm, d), lambda i, j: (i, 0)),
            pl.BlockSpec((1, d), lambda i, j: (0, 0)),
            pl.BlockSpec((None, d, th), lambda i, j: (layer, 0, j)),
            pl.BlockSpec((None, d, th), lambda i, j: (layer, 0, j + nh)),
            pl.BlockSpec((None, th, d), lambda i, j: (layer, j, 0)),
            pl.BlockSpec((1, d), lambda i, j: (0, 0)),
        ],
        out_specs=pl.BlockSpec((tm, d), lambda i, j: (i, 0)),
        out_shape=jax.ShapeDtypeStruct((t, d), F32),
        scratch_shapes=[pltpu.VMEM((tm, d), BF16), pltpu.VMEM((tm, d), F32)],
        compiler_params=_cparams("parallel", "arbitrary"),
        name="ffn",
    )(x, gain, w_gate_up, w_gate_up, w_down, final_gain)


def _largest_tile(n, cap):
    t = cap
    while n % t:
        t //= 2
    return t


def _prep_w_in(w):
    o_b = N_PROJ_A + 2 * HEADS
    small = jnp.pad(w[:, :, N_PROJ_A:o_b], ((0, 0), (0, 0), (0, N_SMALL - 2 * HEADS)))
    return w[:, :, :N_PROJ_A].astype(BF16), w[:, :, o_b:o_b + N_PROJ_B].astype(BF16), small.astype(BF16)


def _lane_row(vals, offset):
    row = jnp.zeros((1, LANES), F32)
    return row.at[0, offset:offset + vals.shape[0]].set(vals.astype(F32))


def kernel(x, attn_norm, w_in, conv_a_w, dn_conv_w, dn_a_log, dn_dt_bias, dn_norm, w_br_a, w_br_dn,
           w_br_moba, w_out, ffn_norm, w_gate_up, w_down, final_norm):
    batch, seq, d = x.shape
    depth = attn_norm.shape[0]
    assert d == D_MODEL and seq // MOBA_BLOCK <= LANES
    assert seq % (MOBA_QTILE_KEYS * MOBA_GROUP * MOBA_BLOCK) == 0
    t = batch * seq
    tm_big = _largest_tile(t, 1024)
    tm_huge = _largest_tile(t, 2048)
    assert seq % (2 * DN_TILE) == 0 and DN_TILE % DN_PAIR == 0 and (seq // DN_PAIR) % SUBLANES == 0

    inv = 1.0 / (ROPE_THETA ** (jnp.arange(0, HEAD_DIM, 2, dtype=F32) / HEAD_DIM))
    ang = jnp.arange(seq, dtype=F32)[:, None] * inv[None, :]
    cos_full = jnp.concatenate([jnp.cos(ang), jnp.cos(ang)], axis=-1)
    sin_signed = jnp.concatenate([-jnp.sin(ang), jnp.sin(ang)], axis=-1)

    w_gate_up16, w_down16 = w_gate_up.astype(BF16), w_down.astype(BF16)
    w_br16 = [w.astype(BF16) for w in (w_br_a, w_br_dn, w_br_moba, w_out)]
    w_in16 = _prep_w_in(w_in)

    xf = x.reshape(t, d)
    for l in range(depth):
        proj_a, proj_b, raw_gates = _in_proj(xf, attn_norm[l][None, :], *w_in16, l, tm_huge, 1024)

        gates = _dn_gates(raw_gates, _lane_row(dn_a_log[l], HEADS), _lane_row(dn_dt_bias[l], HEADS), tm_big)
        g_row = (gates[:, HEADS:2 * HEADS].reshape(batch, seq, HEADS).transpose(0, 2, 1)
                 .reshape(batch, HEADS, seq // DN_PAIR, DN_PAIR))
        y_dn = _deltanet(proj_a, gates, g_row, dn_conv_w[l], dn_norm[l][None, :], batch, seq)

        q_aug_t, k_aug, v_t = _moba_prep(proj_b, cos_full, sin_signed, batch, seq)
        y_mb = _moba_attn(q_aug_t, k_aug, v_t, batch, seq)

        x1 = _merge(proj_a, proj_b, conv_a_w[l], y_dn, y_mb, xf, *w_br16, l, seq, _largest_tile(t, 512))
        xf = _ffn(x1, ffn_norm[l][None, :], w_gate_up16, w_down16, l, final_norm[None, :],
                  l == depth - 1, tm_huge, 256)

    return xf.reshape(batch, seq, d)
```

```python
import functools

import jax
import jax.numpy as jnp
from jax import lax
from jax.experimental import pallas as pl
from jax.experimental.pallas import tpu as pltpu

F32 = jnp.float32
BF16 = jnp.bfloat16

D_MODEL = 1024
HEADS = 8
HEAD_DIM = 128
NORM_EPS = 1e-6
CONV_A_K = 3
DN_CONV_K = 4
DN_CHUNK = 64
DN_PAIR = 2 * DN_CHUNK
DN_TILE = 256
DN_CONV_GROUP = 128
MOBA_BLOCK = 256
MOBA_TOPK = 3
MOBA_GROUP = 2
MOBA_QTILE_KEYS = 2
LOG2_E = 1.4426950408889634
ROPE_THETA = 10000.0
FFN_HIDDEN = 2816

COL_AX, COL_AC, COL_AB = 0, 1, 2
COL_DQ, COL_DK, COL_DV, COL_DZ = 3, 4, 5, 6
N_PROJ_A = 7 * D_MODEL
COL_MQ, COL_MK, COL_MV = 0, 1, 2
COL_GATE = 3
N_PROJ_B = 6 * D_MODEL
N_SMALL = 128

MASK_NEG = -1e30
VMEM_LIMIT = 56 * 1024 * 1024
SUBLANES = 8
LANES = 128
HALO_ROWS = 16


def _cparams(*sem, flags=None):
    return pltpu.CompilerParams(dimension_semantics=sem, vmem_limit_bytes=VMEM_LIMIT, flags=flags)


def _silu(x):
    return x * jax.nn.sigmoid(x)


def _dot(a, b):
    return jnp.dot(a, b, preferred_element_type=F32)


def _dot_nt(a, b):
    return lax.dot_general(a, b, (((1,), (1,)), ((), ())), preferred_element_type=F32)


def _dot_tn(a, b):
    return lax.dot_general(a, b, (((0,), (0,)), ((), ())), preferred_element_type=F32)


def _dn_gates(raw, alog_row, dt_row):
    tm = raw.shape[0]
    xa = raw + dt_row
    softplus = jnp.maximum(xa, 0.0) + jnp.log1p(jnp.exp(-jnp.abs(xa)))
    g = -jnp.exp(alog_row) * softplus
    row = lax.broadcasted_iota(jnp.int32, (tm, LANES), 0) % DN_CHUNK
    shift = 1
    while shift < DN_CHUNK:
        g = g + jnp.where(row >= shift, pltpu.roll(g, shift, axis=0), 0.0)
        shift *= 2
    lane = lax.broadcasted_iota(jnp.int32, (tm, LANES), 1)
    return jnp.where(lane < HEADS, jax.nn.sigmoid(raw), g)


def _in_proj_kernel(x_ref, g_ref, wa_ref, wb_ref, ws_ref, alog_ref, dt_ref, oa_ref, ob_ref, os_ref, h_ref,
                    *, na):
    j = pl.program_id(1)

    @pl.when(j == 0)
    def _():
        x = x_ref[...]
        ms = jnp.mean(x * x, axis=-1, keepdims=True)
        h_ref[...] = (x * lax.rsqrt(ms + NORM_EPS) * g_ref[...]).astype(BF16)
        os_ref[...] = _dn_gates(_dot(h_ref[...], ws_ref[...]), alog_ref[...], dt_ref[...])

    @pl.when(j < na)
    def _():
        oa_ref[...] = _dot(h_ref[...], wa_ref[...]).astype(BF16)

    @pl.when(j >= na)
    def _():
        ob_ref[...] = _dot(h_ref[...], wb_ref[...]).astype(BF16)


def _in_proj(x, gain, wa, wb, ws, alog_row, dt_row, layer, tm, tn):
    t, d = x.shape
    assert tm % DN_CHUNK == 0
    lane_row = pl.BlockSpec((1, LANES), lambda i, j: (0, 0))
    na, nb = wa.shape[2] // tn, wb.shape[2] // tn
    a_idx = lambda j: jnp.minimum(j, na - 1)
    b_idx = lambda j: jnp.maximum(j - na, 0)
    return pl.pallas_call(
        functools.partial(_in_proj_kernel, na=na),
        grid=(t // tm, na + nb),
        in_specs=[
            pl.BlockSpec((tm, d), lambda i, j: (i, 0)),
            pl.BlockSpec((1, d), lambda i, j: (0, 0)),
            pl.BlockSpec((None, d, tn), lambda i, j: (layer, 0, a_idx(j))),
            pl.BlockSpec((None, d, tn), lambda i, j: (layer, 0, b_idx(j))),
            pl.BlockSpec((None, d, N_SMALL), lambda i, j: (layer, 0, 0)),
            lane_row, lane_row,
        ],
        out_specs=[pl.BlockSpec((tm, tn), lambda i, j: (i, a_idx(j))),
                   pl.BlockSpec((tm, tn), lambda i, j: (i, b_idx(j))),
                   pl.BlockSpec((tm, N_SMALL), lambda i, j: (i, 0))],
        out_shape=[jax.ShapeDtypeStruct((t, wa.shape[2]), BF16),
                   jax.ShapeDtypeStruct((t, wb.shape[2]), BF16),
                   jax.ShapeDtypeStruct((t, N_SMALL), F32)],
        scratch_shapes=[pltpu.VMEM((tm, d), BF16)],
        compiler_params=_cparams("parallel", "arbitrary"),
        name="in_proj",
    )(x, gain, wa, wb, ws, alog_row, dt_row)


def _causal_conv(x, carry, w, k):
    xe = jnp.concatenate([carry, x], axis=0)
    y = x * w[k - 1:k]
    for j in range(1, k):
        y = y + pltpu.roll(xe, j, axis=0)[SUBLANES:] * w[k - 1 - j:k - j]
    return y


def _head_cols(h):
    return slice(h * HEAD_DIM, (h + 1) * HEAD_DIM)


def _interleave(*streams):
    order = [((k + 0.5) / len(s), si, step) for si, s in enumerate(streams) for k, step in enumerate(s)]
    for _, _, step in sorted(order, key=lambda e: e[:2]):
        step()


def _deltanet_kernel(q_ref, k_ref, v_ref, z_ref, qx_ref, kx_ref, vx_ref, gates_ref, grow_ref, cw_ref,
                     nw_ref, o_ref, carry_ref, state_ref, qkv_ref,
                     u_ref, w_ref, qd_ref, kd_ref, qk_ref):
    T = DN_TILE
    C = DN_CHUNK
    P = DN_PAIR
    hrange = range(HEADS)
    step_idx = pl.program_id(1)

    G = DN_CONV_GROUP
    sr = lax.broadcasted_iota(jnp.int32, ((DN_CONV_K - 1) * G, 2 * G), 0)
    sc = lax.broadcasted_iota(jnp.int32, ((DN_CONV_K - 1) * G, 2 * G), 1)
    shift_all = (sc == G + sr % G - (sr // G + 1)).astype(BF16)

    def pre_stream(srcs, r0, buf):
        steps = []
        for idx, src in enumerate(srcs):
            for rg in range(T // G):
                for hp in range(HEADS // 2):
                    def piece(idx=idx, src=src, rg=rg, hp=hp):
                        cols = slice(hp * 2 * HEAD_DIM, (hp + 1) * 2 * HEAD_DIM)
                        cur = src[r0 + rg * G:r0 + (rg + 1) * G, cols]
                        prev = carry_ref[idx, :, cols] if rg == 0 else src[r0 + (rg - 1) * G:r0 + rg * G, cols]
                        sh = _dot(shift_all, jnp.concatenate([prev, cur], axis=0))
                        w = cw_ref[:, idx * D_MODEL + hp * 2 * HEAD_DIM:idx * D_MODEL + (hp + 1) * 2 * HEAD_DIM]
                        y = cur.astype(F32) * w[DN_CONV_K - 1:DN_CONV_K]
                        for j in range(1, DN_CONV_K):
                            y = y + sh[(j - 1) * G:j * G] * w[DN_CONV_K - 1 - j:DN_CONV_K - j]
                        y = _silu(y)
                        for e in range(2):
                            ye = y[:, e * HEAD_DIM:(e + 1) * HEAD_DIM]
                            if idx < 2:
                                scale = HEAD_DIM ** -0.5 if idx == 0 else 1.0
                                ye = ye * (lax.rsqrt(jnp.sum(ye * ye, axis=-1, keepdims=True) + NORM_EPS) * scale)
                            qkv_ref[buf, idx, rg * G:(rg + 1) * G, _head_cols(2 * hp + e)] = ye
                    steps.append(piece)

            def save(idx=idx, src=src):
                carry_ref[idx] = src[r0 + T - G:r0 + T, :]
            steps.append(save)
        return steps

    @pl.when(step_idx == 0)
    def _():
        carry_ref[...] = jnp.zeros_like(carry_ref)
        state_ref[...] = jnp.zeros_like(state_ref)
        for step in pre_stream((q_ref, k_ref, v_ref), 0, 0):
            step()

    ri = lax.broadcasted_iota(jnp.int32, (P, P), 0)
    ci = lax.broadcasted_iota(jnp.int32, (P, P), 1)
    same_chunk = (ri // C) == (ci // C)
    causal = same_chunk & (ci <= ri)
    strict = same_chunk & (ci < ri)
    eye = (ci == ri).astype(F32)
    level_masks = []
    bs = 1
    while bs < C:
        same = (ri // (2 * bs)) == (ci // (2 * bs))
        level_masks.append(same & ((ri % (2 * bs)) >= bs) & ((ci % (2 * bs)) < bs))
        bs *= 2
    level_masks16 = [jnp.where(m, 1.0, 0.0).astype(BF16) for m in level_masks[1:]]
    first_half = lax.broadcasted_iota(jnp.int32, (P, 1), 0) < C
    pair0 = step_idx * (2 * T // P)
    nw = nw_ref[...]

    def a_stream(buf, r0):
        probs = [(sl, h) for sl in range(T // P) for h in hrange]
        n = range(len(probs))
        rows = [slice(sl * P, (sl + 1) * P) for sl in range(T // P)]
        ld = lambda idx, i: qkv_ref[buf, idx, rows[probs[i][0]], _head_cols(probs[i][1])]
        st = {}

        def s_decay():
            gbs = [gates_ref[r0 + sl * P:r0 + (sl + 1) * P, :] for sl in range(T // P)]
            st["beta"] = [gbs[sl][:, h:h + 1] for sl, h in probs]
            st["gc"] = [gbs[sl][:, HEADS + h:HEADS + h + 1] for sl, h in probs]
            gr = [grow_ref[0, h, pl.ds(pair0 + r0 // P + sl, 1), :] for sl, h in probs]
            st["decay"] = [jnp.where(causal, jnp.exp(jnp.where(causal, st["gc"][i] - gr[i], 0.0)), 0.0)
                           for i in n]

        def s_a():
            k = [ld(1, i) for i in n]
            st["kb"] = [k[i] * st["beta"][i] for i in n]
            st["k16"] = [k[i].astype(BF16) for i in n]
            st["a"] = [jnp.where(strict, _dot_nt(st["kb"][i].astype(BF16), st["k16"][i]) * st["decay"][i], 0.0)
                       for i in n]
            st["t"] = [eye - jnp.where(level_masks[0], st["a"][i], 0.0) for i in n]
            st["a16"] = [st["a"][i].astype(BF16) for i in n]

        def s_tx(m16):
            st["t16"] = [st["t"][i].astype(BF16) for i in n]
            st["tx"] = [_dot(st["t16"][i], st["a16"][i] * m16) for i in n]

        def s_t():
            st["t"] = [st["t"][i] - _dot(st["tx"][i].astype(BF16), st["t16"][i]) for i in n]

        def s_uw():
            st["eg"] = [jnp.exp(st["gc"][i]) for i in n]
            rhs = [jnp.concatenate([ld(2, i) * st["beta"][i], st["kb"][i] * st["eg"][i]], axis=1).astype(BF16)
                   for i in n]
            st["uw"] = [_dot(st["t"][i].astype(BF16), rhs[i]) for i in n]

        def s_qk():
            st["qk"] = [(_dot_nt(ld(0, i).astype(BF16), st["k16"][i]) * st["decay"][i]).astype(BF16) for i in n]

        def s_store():
            for i, (sl, h) in enumerate(probs):
                r, gc = rows[sl], st["gc"][i]
                glast = jnp.where(first_half, gc[C - 1:C, :], gc[P - 1:P, :])
                u_ref[buf, r, _head_cols(h)] = st["uw"][i][:, :HEAD_DIM]
                w_ref[buf, r, _head_cols(h)] = st["uw"][i][:, HEAD_DIM:].astype(BF16)
                qd_ref[buf, r, _head_cols(h)] = (ld(0, i) * st["eg"][i]).astype(BF16)
                kd_ref[buf, r, _head_cols(h)] = (ld(1, i) * jnp.exp(glast - gc)).astype(BF16)
                qk_ref[buf, sl * P:sl * P + C, h * C:(h + 1) * C] = st["qk"][i][:C, :C]
                qk_ref[buf, sl * P + C:(sl + 1) * P, h * C:(h + 1) * C] = st["qk"][i][C:, C:]

        steps = [s_decay, s_a]
        for m16 in level_masks16:
            steps += [functools.partial(s_tx, m16), s_t]
        return steps + [s_uw, s_qk, s_store]

    def b_stream(buf, r0):
        st = {}
        steps = []
        for c in range(T // C):
            rc = slice(c * C, (c + 1) * C)
            rg = slice(r0 + c * C, r0 + (c + 1) * C)

            def s_ws(rc=rc):
                st["s"] = [state_ref[h] for h in hrange]
                lhs = [jnp.concatenate([w_ref[buf, rc, _head_cols(h)], qd_ref[buf, rc, _head_cols(h)]], axis=0)
                       for h in hrange]
                st["ws"] = [_dot(lhs[h], st["s"][h].astype(BF16)) for h in hrange]

            def s_state(rc=rc, rg=rg):
                gb = gates_ref[rg, :]
                vn16 = [(u_ref[buf, rc, _head_cols(h)] - st["ws"][h][:C]).astype(BF16) for h in hrange]
                st["o"] = [st["ws"][h][C:] + _dot(qk_ref[buf, rc, h * C:(h + 1) * C], vn16[h]) for h in hrange]
                for h in hrange:
                    cd = jnp.exp(gb[C - 1:C, HEADS + h:HEADS + h + 1])
                    state_ref[h] = st["s"][h] * cd + _dot_tn(kd_ref[buf, rc, _head_cols(h)], vn16[h])

            def s_out(rg=rg):
                for h in hrange:
                    o = st["o"][h]
                    on = o * lax.rsqrt(jnp.mean(o * o, axis=-1, keepdims=True) + NORM_EPS) * nw
                    o_ref[rg, _head_cols(h)] = (on * _silu(z_ref[rg, _head_cols(h)].astype(F32))).astype(BF16)

            steps += [s_ws, s_state, s_out]
        return steps

    _interleave(a_stream(0, 0), pre_stream((q_ref, k_ref, v_ref), T, 1))
    _interleave(a_stream(1, T), b_stream(0, 0))
    _interleave(b_stream(1, T), pre_stream((qx_ref, kx_ref, vx_ref), 0, 0))


def _deltanet(proj, gates, g_row, conv_w, norm_w, batch, seq):
    t = proj.shape[0]
    T = DN_TILE
    L = 2 * T
    nl = seq // L
    blk = lambda col: pl.BlockSpec((L, D_MODEL), lambda b, s, col=col: (b * nl + s, col))
    nxt = lambda col: pl.BlockSpec(
        (T, D_MODEL), lambda b, s, col=col: (2 * b * nl + jnp.minimum(2 * s + 2, 2 * nl - 1), col))
    return pl.pallas_call(
        _deltanet_kernel,
        grid=(batch, nl),
        in_specs=[
            blk(COL_DQ), blk(COL_DK), blk(COL_DV), blk(COL_DZ),
            nxt(COL_DQ), nxt(COL_DK), nxt(COL_DV),
            pl.BlockSpec((L, LANES), lambda b, s: (b * nl + s, 0)),
            pl.BlockSpec((1, HEADS, seq // DN_PAIR, DN_PAIR), lambda b, s: (b, 0, 0, 0)),
            pl.BlockSpec((DN_CONV_K, 3 * D_MODEL), lambda b, s: (0, 0)),
            pl.BlockSpec((1, HEAD_DIM), lambda b, s: (0, 0)),
        ],
        out_specs=pl.BlockSpec((L, D_MODEL), lambda b, s: (b * nl + s, 0)),
        out_shape=jax.ShapeDtypeStruct((t, D_MODEL), BF16),
        scratch_shapes=[
            pltpu.VMEM((3, DN_CONV_GROUP, D_MODEL), BF16),
            pltpu.VMEM((HEADS, HEAD_DIM, HEAD_DIM), F32),
            pltpu.VMEM((2, 3, T, D_MODEL), F32),
            pltpu.VMEM((2, T, D_MODEL), F32),
            pltpu.VMEM((2, T, D_MODEL), BF16),
            pltpu.VMEM((2, T, D_MODEL), BF16),
            pltpu.VMEM((2, T, D_MODEL), BF16),
            pltpu.VMEM((2, T, HEADS * DN_CHUNK), BF16),
        ],
        compiler_params=_cparams("arbitrary", "arbitrary"),
        name="deltanet",
    )(proj, proj, proj, proj, proj, proj, proj, gates, g_row, conv_w, norm_w)


def _moba_prep_kernel(q_ref, k_ref, v_ref, cos_ref, sin_ref, qt_ref, bias_ref, k16_ref, vt_ref, kmean_ref):
    i = pl.program_id(1)
    BS = MOBA_BLOCK
    nbp = kmean_ref.shape[1]
    hrange = range(HEADS)

    @pl.when(i == 0)
    def _():
        kmean_ref[...] = jnp.zeros_like(kmean_ref)

    cos = cos_ref[...]
    sin = sin_ref[...]
    rope = lambda x: x * cos + pltpu.roll(x, HEAD_DIM // 2, axis=1) * sin
    q = [rope(q_ref[:, _head_cols(h)].astype(F32)) for h in hrange]
    k = [rope(k_ref[:, _head_cols(h)].astype(F32)) for h in hrange]

    def split(x):
        hi = x.astype(BF16)
        return hi, (x - hi.astype(F32)).astype(BF16)

    qs = [split(q[h]) for h in hrange]
    ms = [split(kmean_ref[h]) for h in hrange]
    gate = [_dot_nt(ms[h][0], qs[h][0]) + (_dot_nt(ms[h][0], qs[h][1]) + _dot_nt(ms[h][1], qs[h][0]))
            for h in hrange]
    blk = lax.broadcasted_iota(jnp.int32, (nbp, BS), 0)
    blk_f = blk.astype(F32)
    neg_inf = jnp.float32(-jnp.inf)
    g = [jnp.where(blk < i, gate[h], neg_inf) for h in hrange]
    sel = [blk == i for h in hrange]
    for _ in range(MOBA_TOPK):
        m = [jnp.max(g[h], axis=0, keepdims=True) for h in hrange]
        first = [jnp.min(jnp.where((g[h] == m[h]) & (g[h] > neg_inf), blk_f, float(nbp)),
                         axis=0, keepdims=True) for h in hrange]
        pick = [blk_f == first[h] for h in hrange]
        sel = [sel[h] | pick[h] for h in hrange]
        g = [jnp.where(pick[h], neg_inf, g[h]) for h in hrange]

    qt = [(q[h] * (HEAD_DIM ** -0.5 * LOG2_E)).T.astype(BF16) for h in hrange]
    vt = [v_ref[:, _head_cols(h)].astype(F32).T.astype(BF16) for h in hrange]
    for h in hrange:
        qt_ref[0, h] = qt[h]
        bias_ref[0, h] = jnp.where(sel[h], 0.0, MASK_NEG).astype(BF16)
        k16_ref[0, h] = k[h].astype(BF16)
        vt_ref[0, h, 0] = vt[h]
        kmean_ref[h, pl.ds(i, 1), :] = jnp.mean(k[h], axis=0, keepdims=True)


def _moba_prep(proj, cos_full, sin_signed, batch, seq):
    nb = seq // MOBA_BLOCK
    nbp = -(-nb // 16) * 16
    blk = lambda col: pl.BlockSpec((MOBA_BLOCK, D_MODEL), lambda b, i, col=col: (b * nb + i, col))
    tab = pl.BlockSpec((MOBA_BLOCK, HEAD_DIM), lambda b, i: (i, 0))
    return pl.pallas_call(
        _moba_prep_kernel,
        grid=(batch, nb),
        in_specs=[blk(COL_MQ), blk(COL_MK), blk(COL_MV), tab, tab],
        out_specs=[pl.BlockSpec((1, HEADS, HEAD_DIM, MOBA_BLOCK), lambda b, i: (b, 0, 0, i)),
                   pl.BlockSpec((1, HEADS, nbp, MOBA_BLOCK), lambda b, i: (b, 0, 0, i)),
                   pl.BlockSpec((1, HEADS, MOBA_BLOCK, HEAD_DIM), lambda b, i: (b, 0, i, 0)),
                   pl.BlockSpec((1, HEADS, 1, HEAD_DIM, MOBA_BLOCK),
                                lambda b, i: (b, 0, i // MOBA_GROUP, 0, i % MOBA_GROUP))],
        out_shape=[jax.ShapeDtypeStruct((batch, HEADS, HEAD_DIM, seq), BF16),
                   jax.ShapeDtypeStruct((batch, HEADS, nbp, seq), BF16),
                   jax.ShapeDtypeStruct((batch, HEADS, seq, HEAD_DIM), BF16),
                   jax.ShapeDtypeStruct((batch, HEADS, nb // MOBA_GROUP, HEAD_DIM,
                                         MOBA_GROUP * MOBA_BLOCK), BF16)],
        scratch_shapes=[pltpu.VMEM((HEADS, nbp, HEAD_DIM), F32)],
        compiler_params=_cparams("parallel", "arbitrary"),
        name="moba_prep",
    )(proj, proj, proj, cos_full, sin_signed)


def _moba_attn_kernel(q_ref, bias_ref, k_ref, oh_ref, vt_ref, o_ref, sa_ref, sb_ref):
    tq_idx = pl.program_id(2)
    TK = MOBA_GROUP * MOBA_BLOCK
    TQ = MOBA_QTILE_KEYS * TK
    nbp = bias_ref.shape[2]
    qt = jnp.concatenate([q_ref[0, 0], bias_ref[0, 0], jnp.zeros((HEAD_DIM - nbp, TQ), BF16)], axis=0)

    def k_aug(j):
        rows = pl.ds(pl.multiple_of(j * TK, TK), TK)
        return jnp.concatenate([k_ref[0, 0, rows, :], oh_ref[rows, :]], axis=1)

    def scores(j):
        return _dot(k_aug(j), qt)

    def attend(s, j, m, l, acc):
        m_new = jnp.maximum(m, jnp.max(s, axis=0, keepdims=True))
        alpha = jnp.exp2(m - m_new)
        p = jnp.exp2(s - m_new)
        l = alpha * l + jnp.sum(p, axis=0, keepdims=True)
        acc = alpha * acc + _dot(vt_ref[0, 0, j], p.astype(BF16))
        return m_new, l, acc

    def step(g, cur_ref, nxt_ref, carry):
        nxt_ref[...] = scores(g + 1)
        return attend(cur_ref[...], g, *carry)

    assert MOBA_QTILE_KEYS == 2
    sa_ref[...] = scores(0)
    carry = (jnp.full((1, TQ), MASK_NEG, F32), jnp.zeros((1, TQ), F32), jnp.zeros((HEAD_DIM, TQ), F32))

    def two_steps(h, carry):
        carry = step(2 * h, sa_ref, sb_ref, carry)
        return step(2 * h + 1, sb_ref, sa_ref, carry)

    carry = lax.fori_loop(0, tq_idx, two_steps, carry)

    own0, own1 = 2 * tq_idx, 2 * tq_idx + 1
    half = slice(TK, TQ)
    tri = (lax.broadcasted_iota(jnp.int32, (TK, TK), 0) <= lax.broadcasted_iota(jnp.int32, (TK, TK), 1))
    sb_ref[:, half] = _dot(k_aug(own1), qt[:, half])
    s0 = sa_ref[...]
    s0 = jnp.concatenate([jnp.where(tri, s0[:, :TK], MASK_NEG), s0[:, half]], axis=1)
    m, l, acc = attend(s0, own0, *carry)
    m2, l2, acc2 = attend(jnp.where(tri, sb_ref[:, half], MASK_NEG), own1, m[:, half], l[:, half], acc[:, half])
    l = jnp.concatenate([l[:, :TK], l2], axis=1)
    acc = jnp.concatenate([acc[:, :TK], acc2], axis=1)
    o_ref[...] = (acc / l).T.astype(BF16)


def _moba_attn(q_t, bias_t, k16, block_onehot, v_t, batch, seq):
    tk = MOBA_GROUP * MOBA_BLOCK
    tq = MOBA_QTILE_KEYS * tk
    nbp = bias_t.shape[2]
    return pl.pallas_call(
        _moba_attn_kernel,
        grid=(batch, HEADS, seq // tq),
        in_specs=[
            pl.BlockSpec((1, 1, HEAD_DIM, tq), lambda b, h, i: (b, h, 0, i)),
            pl.BlockSpec((1, 1, nbp, tq), lambda b, h, i: (b, h, 0, i)),
            pl.BlockSpec((1, 1, seq, HEAD_DIM), lambda b, h, i: (b, h, 0, 0)),
            pl.BlockSpec((seq, HEAD_DIM), lambda b, h, i: (0, 0)),
            pl.BlockSpec((1, 1, seq // tk, HEAD_DIM, tk), lambda b, h, i: (b, h, 0, 0, 0)),
        ],
        out_specs=pl.BlockSpec((tq, HEAD_DIM), lambda b, h, i: (b * (seq // tq) + i, h)),
        out_shape=jax.ShapeDtypeStruct((batch * seq, D_MODEL), BF16),
        scratch_shapes=[pltpu.VMEM((tk, tq), F32), pltpu.VMEM((tk, tq), F32)],
        compiler_params=_cparams("parallel", "parallel", "arbitrary"),
        name="moba_attn",
    )(q_t, bias_t, k16, block_onehot, v_t)


def _merge_kernel(ax_ref, ac_ref, ab_ref, hx_ref, hc_ref, cw_ref, ydn_ref, ymb_ref,
                  ga_ref, gd_ref, gm_ref, x_ref, wa_ref, wd_ref, wm_ref, wo_ref, o_ref, *, seq):
    tm = ax_ref.shape[0]
    f32 = lambda ref: ref[...].astype(F32)
    first = (pl.program_id(0) * tm) % seq == 0
    halo = jnp.where(first, 0.0, (f32(hx_ref) * f32(hc_ref))[HALO_ROWS - SUBLANES:])
    p = f32(ax_ref) * f32(ac_ref)
    y_a = f32(ab_ref) * _causal_conv(p, halo, cw_ref[...], CONV_A_K)
    merged = (jax.nn.sigmoid(f32(ga_ref)) * _dot(y_a.astype(BF16), wa_ref[...])
              + jax.nn.sigmoid(f32(gd_ref)) * _dot(ydn_ref[...], wd_ref[...])
              + jax.nn.sigmoid(f32(gm_ref)) * _dot(ymb_ref[...], wm_ref[...]))
    o_ref[...] = x_ref[...] + _dot(merged.astype(BF16), wo_ref[...])


def _merge(proj_a, proj_b, conv_w, y_dn, y_mb, x, wa, wd, wm, wo, layer, seq, tm):
    t = x.shape[0]
    blk = lambda col: pl.BlockSpec((tm, D_MODEL), lambda i, col=col: (i, col))
    halo = lambda col: pl.BlockSpec(
        (HALO_ROWS, D_MODEL), lambda i, col=col: (jnp.maximum(i * (tm // HALO_ROWS) - 1, 0), col))
    row = pl.BlockSpec((tm, D_MODEL), lambda i: (i, 0))
    wspec = pl.BlockSpec((None, D_MODEL, D_MODEL), lambda i: (layer, 0, 0))
    return pl.pallas_call(
        functools.partial(_merge_kernel, seq=seq),
        grid=(t // tm,),
        in_specs=[blk(COL_AX), blk(COL_AC), blk(COL_AB), halo(COL_AX), halo(COL_AC),
                  pl.BlockSpec((CONV_A_K, D_MODEL), lambda i: (0, 0)),
                  row, row, blk(COL_GATE), blk(COL_GATE + 1), blk(COL_GATE + 2), row,
                  wspec, wspec, wspec, wspec],
        out_specs=row,
        out_shape=jax.ShapeDtypeStruct((t, D_MODEL), F32),
        compiler_params=_cparams("parallel"),
        name="merge",
    )(proj_a, proj_a, proj_a, proj_a, proj_a, conv_w, y_dn, y_mb, proj_b, proj_b, proj_b, x,
      wa, wd, wm, wo)


def _rms_scale(x, gain):
    return x * lax.rsqrt(jnp.mean(x * x, axis=-1, keepdims=True) + NORM_EPS) * gain


def _ffn_kernel(x_ref, g_ref, wg_ref, wu_ref, wd_ref, fg_ref, o_ref, h_ref, acc_ref, *, final_norm):
    j = pl.program_id(1)

    @pl.when(j == 0)
    def _():
        x = x_ref[...]
        h_ref[...] = _rms_scale(x, g_ref[...]).astype(BF16)
        acc_ref[...] = x

    h = h_ref[...]
    act = _silu(_dot(h, wg_ref[...])) * _dot(h, wu_ref[...])
    acc_ref[...] += _dot(act.astype(BF16), wd_ref[...])

    @pl.when(j == pl.num_programs(1) - 1)
    def _():
        y = acc_ref[...]
        o_ref[...] = _rms_scale(y, fg_ref[...]) if final_norm else y


def _ffn(x, gain, w_gate_up, w_down, layer, final_gain, final_norm, tm, th):
    t, d = x.shape
    hid = w_down.shape[1]
    nh = hid // th
    return pl.pallas_call(
        functools.partial(_ffn_kernel, final_norm=final_norm),
        grid=(t // tm, nh),
        in_specs=[
            pl.BlockSpec((tm, d), lambda i, j: (i, 0)),
            pl.BlockSpec((1, d), lambda i, j: (0, 0)),
            pl.BlockSpec((None, d, th), lambda i, j: (layer, 0, j)),
            pl.BlockSpec((None, d, th), lambda i, j: (layer, 0, j + nh)),
            pl.BlockSpec((None, th, d), lambda i, j: (layer, j, 0)),
            pl.BlockSpec((1, d), lambda i, j: (0, 0)),
        ],
        out_specs=pl.BlockSpec((tm, d), lambda i, j: (i, 0)),
        out_shape=jax.ShapeDtypeStruct((t, d), F32),
        scratch_shapes=[pltpu.VMEM((tm, d), BF16), pltpu.VMEM((tm, d), F32)],
        compiler_params=_cparams("parallel", "arbitrary"),
        name="ffn",
    )(x, gain, w_gate_up, w_gate_up, w_down, final_gain)


def _largest_tile(n, cap):
    t = cap
    while n % t:
        t //= 2
    return t


def _prep_w_in(w):
    o_b = N_PROJ_A + 2 * HEADS
    small = jnp.pad(w[:, :, N_PROJ_A:o_b], ((0, 0), (0, 0), (0, N_SMALL - 2 * HEADS)))
    return w[:, :, :N_PROJ_A].astype(BF16), w[:, :, o_b:o_b + N_PROJ_B].astype(BF16), small.astype(BF16)


def _lane_row(vals, offset):
    row = jnp.zeros((1, LANES), F32)
    return row.at[0, offset:offset + vals.shape[0]].set(vals.astype(F32))


def kernel(x, attn_norm, w_in, conv_a_w, dn_conv_w, dn_a_log, dn_dt_bias, dn_norm, w_br_a, w_br_dn,
           w_br_moba, w_out, ffn_norm, w_gate_up, w_down, final_norm):
    batch, seq, d = x.shape
    depth = attn_norm.shape[0]
    assert d == D_MODEL and seq // MOBA_BLOCK <= LANES
    assert seq % (MOBA_QTILE_KEYS * MOBA_GROUP * MOBA_BLOCK) == 0
    t = batch * seq
    tm_huge = _largest_tile(t, 2048)
    assert seq % (2 * DN_TILE) == 0 and DN_TILE % DN_PAIR == 0 and (seq // DN_PAIR) % SUBLANES == 0

    inv = 1.0 / (ROPE_THETA ** (jnp.arange(0, HEAD_DIM, 2, dtype=F32) / HEAD_DIM))
    ang = jnp.arange(seq, dtype=F32)[:, None] * inv[None, :]
    cos_full = jnp.concatenate([jnp.cos(ang), jnp.cos(ang)], axis=-1)
    sin_signed = jnp.concatenate([-jnp.sin(ang), jnp.sin(ang)], axis=-1)
    block_onehot = (jnp.arange(seq)[:, None] // MOBA_BLOCK == jnp.arange(HEAD_DIM)[None, :]).astype(BF16)

    w_gate_up16, w_down16 = w_gate_up.astype(BF16), w_down.astype(BF16)
    w_br16 = [w.astype(BF16) for w in (w_br_a, w_br_dn, w_br_moba, w_out)]
    w_in16 = _prep_w_in(w_in)

    xf = x.reshape(t, d)
    for l in range(depth):
        proj_a, proj_b, gates = _in_proj(xf, attn_norm[l][None, :], *w_in16, _lane_row(dn_a_log[l], HEADS),
                                         _lane_row(dn_dt_bias[l], HEADS), l, tm_huge, 1024)
        g_row = (gates[:, HEADS:2 * HEADS].reshape(batch, seq, HEADS).transpose(0, 2, 1)
                 .reshape(batch, HEADS, seq // DN_PAIR, DN_PAIR))
        y_dn = _deltanet(proj_a, gates, g_row, dn_conv_w[l], dn_norm[l][None, :], batch, seq)

        q_t, bias_t, k16, v_t = _moba_prep(proj_b, cos_full, sin_signed, batch, seq)
        y_mb = _moba_attn(q_t, bias_t, k16, block_onehot, v_t, batch, seq)

        x1 = _merge(proj_a, proj_b, conv_a_w[l], y_dn, y_mb, xf, *w_br16, l, seq, _largest_tile(t, 512))
        xf = _ffn(x1, ffn_norm[l][None, :], w_gate_up16, w_down16, l, final_norm[None, :],
                  l == depth - 1, tm_huge, 256)

    return xf.reshape(batch, seq, d)
```

```python
import functools

import jax
import jax.numpy as jnp
from jax import lax
from jax.experimental import pallas as pl
from jax.experimental.pallas import tpu as pltpu

F32 = jnp.float32
BF16 = jnp.bfloat16

D_MODEL = 1024
HEADS = 8
HEAD_DIM = 128
NORM_EPS = 1e-6
CONV_A_K = 3
DN_CONV_K = 4
DN_CHUNK = 64
DN_PAIR = 2 * DN_CHUNK
DN_TILE = 256
DN_CONV_GROUP = 128
MOBA_BLOCK = 256
MOBA_TOPK = 3
MOBA_GROUP = 2
MOBA_QTILE_KEYS = 2
LOG2_E = 1.4426950408889634
ROPE_THETA = 10000.0
FFN_HIDDEN = 2816

COL_AX, COL_AC, COL_AB = 0, 1, 2
COL_DQ, COL_DK, COL_DV, COL_DZ = 3, 4, 5, 6
N_PROJ_A = 7 * D_MODEL
COL_MQ, COL_MK, COL_MV = 0, 1, 2
COL_GATE = 3
N_PROJ_B = 6 * D_MODEL
N_SMALL = 128

MASK_NEG = -1e30
VMEM_LIMIT = 56 * 1024 * 1024
SUBLANES = 8
LANES = 128
HALO_ROWS = 16


def _cparams(*sem, flags=None):
    return pltpu.CompilerParams(dimension_semantics=sem, vmem_limit_bytes=VMEM_LIMIT, flags=flags)


def _silu(x):
    return x * jax.nn.sigmoid(x)


def _dot(a, b):
    return jnp.dot(a, b, preferred_element_type=F32)


def _dot_nt(a, b):
    return lax.dot_general(a, b, (((1,), (1,)), ((), ())), preferred_element_type=F32)


def _dot_tn(a, b):
    return lax.dot_general(a, b, (((0,), (0,)), ((), ())), preferred_element_type=F32)


def _dn_gates(raw, alog_row, dt_row):
    tm = raw.shape[0]
    xa = raw + dt_row
    softplus = jnp.maximum(xa, 0.0) + jnp.log1p(jnp.exp(-jnp.abs(xa)))
    g = -jnp.exp(alog_row) * softplus
    row = lax.broadcasted_iota(jnp.int32, (tm, LANES), 0) % DN_CHUNK
    shift = 1
    while shift < DN_CHUNK:
        g = g + jnp.where(row >= shift, pltpu.roll(g, shift, axis=0), 0.0)
        shift *= 2
    lane = lax.broadcasted_iota(jnp.int32, (tm, LANES), 1)
    return jnp.where(lane < HEADS, jax.nn.sigmoid(raw), g)


def _in_proj_kernel(x_ref, g_ref, wa_ref, wb_ref, ws_ref, alog_ref, dt_ref, oa_ref, ob_ref, os_ref, h_ref,
                    *, na):
    j = pl.program_id(1)

    @pl.when(j == 0)
    def _():
        x = x_ref[...]
        ms = jnp.mean(x * x, axis=-1, keepdims=True)
        h_ref[...] = (x * lax.rsqrt(ms + NORM_EPS) * g_ref[...]).astype(BF16)
        os_ref[...] = _dn_gates(_dot(h_ref[...], ws_ref[...]), alog_ref[...], dt_ref[...])

    @pl.when(j < na)
    def _():
        oa_ref[...] = _dot(h_ref[...], wa_ref[...]).astype(BF16)

    @pl.when(j >= na)
    def _():
        ob_ref[...] = _dot(h_ref[...], wb_ref[...]).astype(BF16)


def _in_proj(x, gain, wa, wb, ws, alog_row, dt_row, layer, tm, tn):
    t, d = x.shape
    assert tm % DN_CHUNK == 0
    lane_row = pl.BlockSpec((1, LANES), lambda i, j: (0, 0))
    na, nb = wa.shape[2] // tn, wb.shape[2] // tn
    a_idx = lambda j: jnp.minimum(j, na - 1)
    b_idx = lambda j: jnp.maximum(j - na, 0)
    return pl.pallas_call(
        functools.partial(_in_proj_kernel, na=na),
        grid=(t // tm, na + nb),
        in_specs=[
            pl.BlockSpec((tm, d), lambda i, j: (i, 0)),
            pl.BlockSpec((1, d), lambda i, j: (0, 0)),
            pl.BlockSpec((None, d, tn), lambda i, j: (layer, 0, a_idx(j))),
            pl.BlockSpec((None, d, tn), lambda i, j: (layer, 0, b_idx(j))),
            pl.BlockSpec((None, d, N_SMALL), lambda i, j: (layer, 0, 0)),
            lane_row, lane_row,
        ],
        out_specs=[pl.BlockSpec((tm, tn), lambda i, j: (i, a_idx(j))),
                   pl.BlockSpec((tm, tn), lambda i, j: (i, b_idx(j))),
                   pl.BlockSpec((tm, N_SMALL), lambda i, j: (i, 0))],
        out_shape=[jax.ShapeDtypeStruct((t, wa.shape[2]), BF16),
                   jax.ShapeDtypeStruct((t, wb.shape[2]), BF16),
                   jax.ShapeDtypeStruct((t, N_SMALL), F32)],
        scratch_shapes=[pltpu.VMEM((tm, d), BF16)],
        compiler_params=_cparams("parallel", "arbitrary"),
        name="in_proj",
    )(x, gain, wa, wb, ws, alog_row, dt_row)


def _causal_conv(x, carry, w, k):
    xe = jnp.concatenate([carry, x], axis=0)
    y = x * w[k - 1:k]
    for j in range(1, k):
        y = y + pltpu.roll(xe, j, axis=0)[SUBLANES:] * w[k - 1 - j:k - j]
    return y


def _head_cols(h):
    return slice(h * HEAD_DIM, (h + 1) * HEAD_DIM)


def _interleave(*streams):
    order = [((k + 0.5) / len(s), si, step) for si, s in enumerate(streams) for k, step in enumerate(s)]
    for _, _, step in sorted(order, key=lambda e: e[:2]):
        step()


def _deltanet_kernel(q_ref, k_ref, v_ref, z_ref, qx_ref, kx_ref, vx_ref, gates_ref, grow_ref, cw_ref,
                     nw_ref, o_ref, carry_ref, state_ref, qkv_ref,
                     u_ref, w_ref, qd_ref, kd_ref, qk_ref):
    T = DN_TILE
    C = DN_CHUNK
    P = DN_PAIR
    hrange = range(HEADS)
    step_idx = pl.program_id(1)

    G = DN_CONV_GROUP
    sr = lax.broadcasted_iota(jnp.int32, ((DN_CONV_K - 1) * G, 2 * G), 0)
    sc = lax.broadcasted_iota(jnp.int32, ((DN_CONV_K - 1) * G, 2 * G), 1)
    shift_all = (sc == G + sr % G - (sr // G + 1)).astype(BF16)

    def pre_stream(srcs, r0, buf):
        steps = []
        for idx, src in enumerate(srcs):
            for rg in range(T // G):
                for hp in range(HEADS // 2):
                    def piece(idx=idx, src=src, rg=rg, hp=hp):
                        cols = slice(hp * 2 * HEAD_DIM, (hp + 1) * 2 * HEAD_DIM)
                        cur = src[r0 + rg * G:r0 + (rg + 1) * G, cols]
                        prev = carry_ref[idx, :, cols] if rg == 0 else src[r0 + (rg - 1) * G:r0 + rg * G, cols]
                        sh = _dot(shift_all, jnp.concatenate([prev, cur], axis=0))
                        w = cw_ref[:, idx * D_MODEL + hp * 2 * HEAD_DIM:idx * D_MODEL + (hp + 1) * 2 * HEAD_DIM]
                        y = cur.astype(F32) * w[DN_CONV_K - 1:DN_CONV_K]
                        for j in range(1, DN_CONV_K):
                            y = y + sh[(j - 1) * G:j * G] * w[DN_CONV_K - 1 - j:DN_CONV_K - j]
                        y = _silu(y)
                        for e in range(2):
                            ye = y[:, e * HEAD_DIM:(e + 1) * HEAD_DIM]
                            if idx < 2:
                                scale = HEAD_DIM ** -0.5 if idx == 0 else 1.0
                                ye = ye * (lax.rsqrt(jnp.sum(ye * ye, axis=-1, keepdims=True) + NORM_EPS) * scale)
                            qkv_ref[buf, idx, rg * G:(rg + 1) * G, _head_cols(2 * hp + e)] = ye
                    steps.append(piece)

            def save(idx=idx, src=src):
                carry_ref[idx] = src[r0 + T - G:r0 + T, :]
            steps.append(save)
        return steps

    @pl.when(step_idx == 0)
    def _():
        carry_ref[...] = jnp.zeros_like(carry_ref)
        state_ref[...] = jnp.zeros_like(state_ref)
        for step in pre_stream((q_ref, k_ref, v_ref), 0, 0):
            step()

    ri = lax.broadcasted_iota(jnp.int32, (P, P), 0)
    ci = lax.broadcasted_iota(jnp.int32, (P, P), 1)
    same_chunk = (ri // C) == (ci // C)
    causal = same_chunk & (ci <= ri)
    strict = same_chunk & (ci < ri)
    eye = (ci == ri).astype(F32)
    level_masks = []
    bs = 1
    while bs < C:
        same = (ri // (2 * bs)) == (ci // (2 * bs))
        level_masks.append(same & ((ri % (2 * bs)) >= bs) & ((ci % (2 * bs)) < bs))
        bs *= 2
    level_masks16 = [jnp.where(m, 1.0, 0.0).astype(BF16) for m in level_masks[1:]]
    first_half = lax.broadcasted_iota(jnp.int32, (P, 1), 0) < C
    pair0 = step_idx * (2 * T // P)
    nw = nw_ref[...]

    def a_stream(buf, r0):
        probs = [(sl, h) for sl in range(T // P) for h in hrange]
        n = range(len(probs))
        rows = [slice(sl * P, (sl + 1) * P) for sl in range(T // P)]
        ld = lambda idx, i: qkv_ref[buf, idx, rows[probs[i][0]], _head_cols(probs[i][1])]
        st = {}

        def s_decay():
            gbs = [gates_ref[r0 + sl * P:r0 + (sl + 1) * P, :] for sl in range(T // P)]
            st["beta"] = [gbs[sl][:, h:h + 1] for sl, h in probs]
            st["gc"] = [gbs[sl][:, HEADS + h:HEADS + h + 1] for sl, h in probs]
            gr = [grow_ref[0, h, pl.ds(pair0 + r0 // P + sl, 1), :] for sl, h in probs]
            st["decay"] = [jnp.where(causal, jnp.exp(jnp.where(causal, st["gc"][i] - gr[i], 0.0)), 0.0)
                           for i in n]

        def s_a():
            k = [ld(1, i) for i in n]
            st["kb"] = [k[i] * st["beta"][i] for i in n]
            st["k16"] = [k[i].astype(BF16) for i in n]
            st["a"] = [jnp.where(strict, _dot_nt(st["kb"][i].astype(BF16), st["k16"][i]) * st["decay"][i], 0.0)
                       for i in n]
            st["t"] = [eye - jnp.where(level_masks[0], st["a"][i], 0.0) for i in n]
            st["a16"] = [st["a"][i].astype(BF16) for i in n]

        def s_tx(m16):
            st["t16"] = [st["t"][i].astype(BF16) for i in n]
            st["tx"] = [_dot(st["t16"][i], st["a16"][i] * m16) for i in n]

        def s_t():
            st["t"] = [st["t"][i] - _dot(st["tx"][i].astype(BF16), st["t16"][i]) for i in n]

        def s_uw():
            st["eg"] = [jnp.exp(st["gc"][i]) for i in n]
            rhs = [jnp.concatenate([ld(2, i) * st["beta"][i], st["kb"][i] * st["eg"][i]], axis=1).astype(BF16)
                   for i in n]
            st["uw"] = [_dot(st["t"][i].astype(BF16), rhs[i]) for i in n]

        def s_qk():
            st["qk"] = [(_dot_nt(ld(0, i).astype(BF16), st["k16"][i]) * st["decay"][i]).astype(BF16) for i in n]

        def s_store():
            for i, (sl, h) in enumerate(probs):
                r, gc = rows[sl], st["gc"][i]
                glast = jnp.where(first_half, gc[C - 1:C, :], gc[P - 1:P, :])
                u_ref[buf, r, _head_cols(h)] = st["uw"][i][:, :HEAD_DIM]
                w_ref[buf, r, _head_cols(h)] = st["uw"][i][:, HEAD_DIM:].astype(BF16)
                qd_ref[buf, r, _head_cols(h)] = (ld(0, i) * st["eg"][i]).astype(BF16)
                kd_ref[buf, r, _head_cols(h)] = (ld(1, i) * jnp.exp(glast - gc)).astype(BF16)
                qk_ref[buf, sl * P:sl * P + C, h * C:(h + 1) * C] = st["qk"][i][:C, :C]
                qk_ref[buf, sl * P + C:(sl + 1) * P, h * C:(h + 1) * C] = st["qk"][i][C:, C:]

        steps = [s_decay, s_a]
        for m16 in level_masks16:
            steps += [functools.partial(s_tx, m16), s_t]
        return steps + [s_uw, s_qk, s_store]

    def b_stream(buf, r0):
        st = {}
        steps = []
        for c in range(T // C):
            rc = slice(c * C, (c + 1) * C)
            rg = slice(r0 + c * C, r0 + (c + 1) * C)

            def s_ws(rc=rc):
                st["s"] = [state_ref[h] for h in hrange]
                lhs = [jnp.concatenate([w_ref[buf, rc, _head_cols(h)], qd_ref[buf, rc, _head_cols(h)]], axis=0)
                       for h in hrange]
                st["ws"] = [_dot(lhs[h], st["s"][h].astype(BF16)) for h in hrange]

            def s_state(rc=rc, rg=rg):
                gb = gates_ref[rg, :]
                vn16 = [(u_ref[buf, rc, _head_cols(h)] - st["ws"][h][:C]).astype(BF16) for h in hrange]
                st["o"] = [st["ws"][h][C:] + _dot(qk_ref[buf, rc, h * C:(h + 1) * C], vn16[h]) for h in hrange]
                for h in hrange:
                    cd = jnp.exp(gb[C - 1:C, HEADS + h:HEADS + h + 1])
                    state_ref[h] = st["s"][h] * cd + _dot_tn(kd_ref[buf, rc, _head_cols(h)], vn16[h])

            def s_out(rg=rg):
                for h in hrange:
                    o = st["o"][h]
                    on = o * lax.rsqrt(jnp.mean(o * o, axis=-1, keepdims=True) + NORM_EPS) * nw
                    o_ref[rg, _head_cols(h)] = (on * _silu(z_ref[rg, _head_cols(h)].astype(F32))).astype(BF16)

            steps += [s_ws, s_state, s_out]
        return steps

    _interleave(a_stream(0, 0), pre_stream((q_ref, k_ref, v_ref), T, 1))
    _interleave(a_stream(1, T), b_stream(0, 0))
    _interleave(b_stream(1, T), pre_stream((qx_ref, kx_ref, vx_ref), 0, 0))


def _deltanet(proj, gates, g_row, conv_w, norm_w, batch, seq):
    t = proj.shape[0]
    T = DN_TILE
    L = 2 * T
    nl = seq // L
    blk = lambda col: pl.BlockSpec((L, D_MODEL), lambda b, s, col=col: (b * nl + s, col))
    nxt = lambda col: pl.BlockSpec(
        (T, D_MODEL), lambda b, s, col=col: (2 * b * nl + jnp.minimum(2 * s + 2, 2 * nl - 1), col))
    return pl.pallas_call(
        _deltanet_kernel,
        grid=(batch, nl),
        in_specs=[
            blk(COL_DQ), blk(COL_DK), blk(COL_DV), blk(COL_DZ),
            nxt(COL_DQ), nxt(COL_DK), nxt(COL_DV),
            pl.BlockSpec((L, LANES), lambda b, s: (b * nl + s, 0)),
            pl.BlockSpec((1, HEADS, seq // DN_PAIR, DN_PAIR), lambda b, s: (b, 0, 0, 0)),
            pl.BlockSpec((DN_CONV_K, 3 * D_MODEL), lambda b, s: (0, 0)),
            pl.BlockSpec((1, HEAD_DIM), lambda b, s: (0, 0)),
        ],
        out_specs=pl.BlockSpec((L, D_MODEL), lambda b, s: (b * nl + s, 0)),
        out_shape=jax.ShapeDtypeStruct((t, D_MODEL), BF16),
        scratch_shapes=[
            pltpu.VMEM((3, DN_CONV_GROUP, D_MODEL), BF16),
            pltpu.VMEM((HEADS, HEAD_DIM, HEAD_DIM), F32),
            pltpu.VMEM((2, 3, T, D_MODEL), F32),
            pltpu.VMEM((2, T, D_MODEL), F32),
            pltpu.VMEM((2, T, D_MODEL), BF16),
            pltpu.VMEM((2, T, D_MODEL), BF16),
            pltpu.VMEM((2, T, D_MODEL), BF16),
            pltpu.VMEM((2, T, HEADS * DN_CHUNK), BF16),
        ],
        compiler_params=_cparams("arbitrary", "arbitrary"),
        name="deltanet",
    )(proj, proj, proj, proj, proj, proj, proj, gates, g_row, conv_w, norm_w)


def _moba_prep_kernel(q_ref, k_ref, v_ref, cos_ref, sin_ref, qt_ref, bias_ref, k16_ref, vt_ref, kmean_ref):
    i = pl.program_id(1)
    BS = MOBA_BLOCK
    nbp = kmean_ref.shape[1]
    hrange = range(HEADS)

    @pl.when(i == 0)
    def _():
        kmean_ref[...] = jnp.zeros_like(kmean_ref)

    cos = cos_ref[...]
    sin = sin_ref[...]
    rope = lambda x: x * cos + pltpu.roll(x, HEAD_DIM // 2, axis=1) * sin
    q = [rope(q_ref[:, _head_cols(h)].astype(F32)) for h in hrange]
    k = [rope(k_ref[:, _head_cols(h)].astype(F32)) for h in hrange]

    def split(x):
        hi = x.astype(BF16)
        return hi, (x - hi.astype(F32)).astype(BF16)

    qs = [split(q[h]) for h in hrange]
    ms = [split(kmean_ref[h]) for h in hrange]
    gate = [_dot_nt(ms[h][0], qs[h][0]) + (_dot_nt(ms[h][0], qs[h][1]) + _dot_nt(ms[h][1], qs[h][0]))
            for h in hrange]
    blk = lax.broadcasted_iota(jnp.int32, (nbp, BS), 0)
    blk_f = blk.astype(F32)
    neg_inf = jnp.float32(-jnp.inf)
    g = [jnp.where(blk < i, gate[h], neg_inf) for h in hrange]
    sel = [blk == i for h in hrange]
    for _ in range(MOBA_TOPK):
        m = [jnp.max(g[h], axis=0, keepdims=True) for h in hrange]
        first = [jnp.min(jnp.where((g[h] == m[h]) & (g[h] > neg_inf), blk_f, float(nbp)),
                         axis=0, keepdims=True) for h in hrange]
        pick = [blk_f == first[h] for h in hrange]
        sel = [sel[h] | pick[h] for h in hrange]
        g = [jnp.where(pick[h], neg_inf, g[h]) for h in hrange]

    qt = [(q[h] * (HEAD_DIM ** -0.5 * LOG2_E)).T.astype(BF16) for h in hrange]
    vt = [v_ref[:, _head_cols(h)].astype(F32).T.astype(BF16) for h in hrange]
    for h in hrange:
        qt_ref[0, h] = qt[h]
        bias_ref[0, h] = jnp.where(sel[h], 0.0, MASK_NEG).astype(BF16)
        k16_ref[0, h] = k[h].astype(BF16)
        vt_ref[0, h, 0] = vt[h]
        kmean_ref[h, pl.ds(i, 1), :] = jnp.mean(k[h], axis=0, keepdims=True)


def _moba_prep(proj, cos_full, sin_signed, batch, seq):
    nb = seq // MOBA_BLOCK
    nbp = -(-nb // 16) * 16
    blk = lambda col: pl.BlockSpec((MOBA_BLOCK, D_MODEL), lambda b, i, col=col: (b * nb + i, col))
    tab = pl.BlockSpec((MOBA_BLOCK, HEAD_DIM), lambda b, i: (i, 0))
    return pl.pallas_call(
        _moba_prep_kernel,
        grid=(batch, nb),
        in_specs=[blk(COL_MQ), blk(COL_MK), blk(COL_MV), tab, tab],
        out_specs=[pl.BlockSpec((1, HEADS, HEAD_DIM, MOBA_BLOCK), lambda b, i: (b, 0, 0, i)),
                   pl.BlockSpec((1, HEADS, nbp, MOBA_BLOCK), lambda b, i: (b, 0, 0, i)),
                   pl.BlockSpec((1, HEADS, MOBA_BLOCK, HEAD_DIM), lambda b, i: (b, 0, i, 0)),
                   pl.BlockSpec((1, HEADS, 1, HEAD_DIM, MOBA_BLOCK),
                                lambda b, i: (b, 0, i // MOBA_GROUP, 0, i % MOBA_GROUP))],
        out_shape=[jax.ShapeDtypeStruct((batch, HEADS, HEAD_DIM, seq), BF16),
                   jax.ShapeDtypeStruct((batch, HEADS, nbp, seq), BF16),
                   jax.ShapeDtypeStruct((batch, HEADS, seq, HEAD_DIM), BF16),
                   jax.ShapeDtypeStruct((batch, HEADS, nb // MOBA_GROUP, HEAD_DIM,
                                         MOBA_GROUP * MOBA_BLOCK), BF16)],
        scratch_shapes=[pltpu.VMEM((HEADS, nbp, HEAD_DIM), F32)],
        compiler_params=_cparams("parallel", "arbitrary"),
        name="moba_prep",
    )(proj, proj, proj, cos_full, sin_signed)


def _moba_attn_kernel(q_ref, bias_ref, k_ref, oh_ref, vt_ref, o_ref, sa_ref, sb_ref):
    tq_idx = pl.program_id(2)
    TK = MOBA_GROUP * MOBA_BLOCK
    TQ = MOBA_QTILE_KEYS * TK
    nbp = bias_ref.shape[2]
    qt = jnp.concatenate([q_ref[0, 0], bias_ref[0, 0], jnp.zeros((HEAD_DIM - nbp, TQ), BF16)], axis=0)

    def k_aug(j):
        rows = pl.ds(pl.multiple_of(j * TK, TK), TK)
        return jnp.concatenate([k_ref[0, 0, rows, :], oh_ref[rows, :]], axis=1)

    def scores(j):
        return _dot(k_aug(j), qt)

    def attend(s, j, m, l, acc):
        m_new = jnp.maximum(m, jnp.max(s, axis=0, keepdims=True))
        alpha = jnp.exp2(m - m_new)
        p = jnp.exp2(s - m_new)
        l = alpha * l + jnp.sum(p, axis=0, keepdims=True)
        acc = alpha * acc + _dot(vt_ref[0, 0, j], p.astype(BF16))
        return m_new, l, acc

    def step(g, cur_ref, nxt_ref, carry):
        nxt_ref[...] = scores(g + 1)
        return attend(cur_ref[...], g, *carry)

    assert MOBA_QTILE_KEYS == 2
    sa_ref[...] = scores(0)
    carry = (jnp.full((1, TQ), MASK_NEG, F32), jnp.zeros((1, TQ), F32), jnp.zeros((HEAD_DIM, TQ), F32))

    def two_steps(h, carry):
        carry = step(2 * h, sa_ref, sb_ref, carry)
        return step(2 * h + 1, sb_ref, sa_ref, carry)

    carry = lax.fori_loop(0, tq_idx, two_steps, carry)

    own0, own1 = 2 * tq_idx, 2 * tq_idx + 1
    half = slice(TK, TQ)
    tri = (lax.broadcasted_iota(jnp.int32, (TK, TK), 0) <= lax.broadcasted_iota(jnp.int32, (TK, TK), 1))
    sb_ref[:, half] = _dot(k_aug(own1), qt[:, half])
    s0 = sa_ref[...]
    s0 = jnp.concatenate([jnp.where(tri, s0[:, :TK], MASK_NEG), s0[:, half]], axis=1)
    m, l, acc = attend(s0, own0, *carry)
    m2, l2, acc2 = attend(jnp.where(tri, sb_ref[:, half], MASK_NEG), own1, m[:, half], l[:, half], acc[:, half])
    l = jnp.concatenate([l[:, :TK], l2], axis=1)
    acc = jnp.concatenate([acc[:, :TK], acc2], axis=1)
    o_ref[...] = (acc / l).T.astype(BF16)


def _moba_attn(q_t, bias_t, k16, block_onehot, v_t, batch, seq):
    tk = MOBA_GROUP * MOBA_BLOCK
    tq = MOBA_QTILE_KEYS * tk
    nbp = bias_t.shape[2]
    return pl.pallas_call(
        _moba_attn_kernel,
        grid=(batch, HEADS, seq // tq),
        in_specs=[
            pl.BlockSpec((1, 1, HEAD_DIM, tq), lambda b, h, i: (b, h, 0, i)),
            pl.BlockSpec((1, 1, nbp, tq), lambda b, h, i: (b, h, 0, i)),
            pl.BlockSpec((1, 1, seq, HEAD_DIM), lambda b, h, i: (b, h, 0, 0)),
            pl.BlockSpec((seq, HEAD_DIM), lambda b, h, i: (0, 0)),
            pl.BlockSpec((1, 1, seq // tk, HEAD_DIM, tk), lambda b, h, i: (b, h, 0, 0, 0)),
        ],
        out_specs=pl.BlockSpec((tq, HEAD_DIM), lambda b, h, i: (b * (seq // tq) + i, h)),
        out_shape=jax.ShapeDtypeStruct((batch * seq, D_MODEL), BF16),
        scratch_shapes=[pltpu.VMEM((tk, tq), F32), pltpu.VMEM((tk, tq), F32)],
        compiler_params=_cparams("parallel", "parallel", "arbitrary"),
        name="moba_attn",
    )(q_t, bias_t, k16, block_onehot, v_t)


def _merge_kernel(ax_ref, ac_ref, ab_ref, hx_ref, hc_ref, cw_ref, ydn_ref, ymb_ref,
                  ga_ref, gd_ref, gm_ref, x_ref, wa_ref, wd_ref, wm_ref, wo_ref, o_ref, *, seq):
    tm = ax_ref.shape[0]
    f32 = lambda ref: ref[...].astype(F32)
    first = (pl.program_id(0) * tm) % seq == 0
    halo = jnp.where(first, 0.0, (f32(hx_ref) * f32(hc_ref))[HALO_ROWS - SUBLANES:])
    p = f32(ax_ref) * f32(ac_ref)
    y_a = f32(ab_ref) * _causal_conv(p, halo, cw_ref[...], CONV_A_K)
    merged = (jax.nn.sigmoid(f32(ga_ref)) * _dot(y_a.astype(BF16), wa_ref[...])
              + jax.nn.sigmoid(f32(gd_ref)) * _dot(ydn_ref[...], wd_ref[...])
              + jax.nn.sigmoid(f32(gm_ref)) * _dot(ymb_ref[...], wm_ref[...]))
    o_ref[...] = x_ref[...] + _dot(merged.astype(BF16), wo_ref[...])


def _merge(proj_a, proj_b, conv_w, y_dn, y_mb, x, wa, wd, wm, wo, layer, seq, tm):
    t = x.shape[0]
    blk = lambda col: pl.BlockSpec((tm, D_MODEL), lambda i, col=col: (i, col))
    halo = lambda col: pl.BlockSpec(
        (HALO_ROWS, D_MODEL), lambda i, col=col: (jnp.maximum(i * (tm // HALO_ROWS) - 1, 0), col))
    row = pl.BlockSpec((tm, D_MODEL), lambda i: (i, 0))
    wspec = pl.BlockSpec((None, D_MODEL, D_MODEL), lambda i: (layer, 0, 0))
    return pl.pallas_call(
        functools.partial(_merge_kernel, seq=seq),
        grid=(t // tm,),
        in_specs=[blk(COL_AX), blk(COL_AC), blk(COL_AB), halo(COL_AX), halo(COL_AC),
                  pl.BlockSpec((CONV_A_K, D_MODEL), lambda i: (0, 0)),
                  row, row, blk(COL_GATE), blk(COL_GATE + 1), blk(COL_GATE + 2), row,
                  wspec, wspec, wspec, wspec],
        out_specs=row,
        out_shape=jax.ShapeDtypeStruct((t, D_MODEL), F32),
        compiler_params=_cparams("parallel"),
        name="merge",
    )(proj_a, proj_a, proj_a, proj_a, proj_a, conv_w, y_dn, y_mb, proj_b, proj_b, proj_b, x,
      wa, wd, wm, wo)


def _rms_scale(x, gain):
    return x * lax.rsqrt(jnp.mean(x * x, axis=-1, keepdims=True) + NORM_EPS) * gain


def _ffn_kernel(x_ref, g_ref, wg_ref, wu_ref, wd_ref, fg_ref, o_ref, h_ref, acc_ref, *, final_norm):
    j = pl.program_id(1)

    @pl.when(j == 0)
    def _():
        x = x_ref[...]
        h_ref[...] = _rms_scale(x, g_ref[...]).astype(BF16)
        acc_ref[...] = x

    h = h_ref[...]
    act = _silu(_dot(h, wg_ref[...])) * _dot(h, wu_ref[...])
    acc_ref[...] += _dot(act.astype(BF16), wd_ref[...])

    @pl.when(j == pl.num_programs(1) - 1)
    def _():
        y = acc_ref[...]
        o_ref[...] = _rms_scale(y, fg_ref[...]) if final_norm else y


def _ffn(x, gain, w_gate_up, w_down, layer, final_gain, final_norm, tm, th):
    t, d = x.shape
    hid = w_down.shape[1]
    nh = hid // th
    return pl.pallas_call(
        functools.partial(_ffn_kernel, final_norm=final_norm),
        grid=(t // tm, nh),
        in_specs=[
            pl.BlockSpec((tm, d), lambda i, j: (i, 0)),
            pl.BlockSpec((1, d), lambda i, j: (0, 0)),
            pl.BlockSpec((None, d, th), lambda i, j: (layer, 0, j)),
            pl.BlockSpec((None, d, th), lambda i, j: (layer, 0, j + nh)),
            pl.BlockSpec((None, th, d), lambda i, j: (layer, j, 0)),
            pl.BlockSpec((1, d), lambda i, j: (0, 0)),
        ],
        out_specs=pl.BlockSpec((tm, d), lambda i, j: (i, 0)),
        out_shape=jax.ShapeDtypeStruct((t, d), F32),
        scratch_shapes=[pltpu.VMEM((tm, d), BF16), pltpu.VMEM((tm, d), F32)],
        compiler_params=_cparams("parallel", "arbitrary"),
        name="ffn",
    )(x, gain, w_gate_up, w_gate_up, w_down, final_gain)


def _largest_tile(n, cap):
    t = cap
    while n % t:
        t //= 2
    return t


def _split_w_in_kernel(w_ref, tail_ref, oa_ref, ob_ref, os_ref, *, na):
    j = pl.program_id(1)
    shift = 2 * HEADS

    @pl.when(j == 0)
    def _():
        lane = lax.broadcasted_iota(jnp.int32, tail_ref.shape, 1)
        os_ref[...] = jnp.where(lane < shift, tail_ref[...], 0.0).astype(BF16)

    @pl.when(j < na)
    def _():
        oa_ref[...] = w_ref[...].astype(BF16)

    @pl.when(j >= na)
    def _():
        w = jnp.concatenate([w_ref[:, shift:], tail_ref[:, :shift]], axis=1)
        ob_ref[...] = w.astype(BF16)


def _split_w_in(w, tn):
    depth, d, n = w.shape
    na, nb = N_PROJ_A // tn, N_PROJ_B // tn
    per_tn = tn // N_SMALL
    w = w.astype(BF16).reshape(depth * d, n)
    return pl.pallas_call(
        functools.partial(_split_w_in_kernel, na=na),
        grid=(depth, na + nb),
        in_specs=[
            pl.BlockSpec((d, tn), lambda l, j: (l, j)),
            pl.BlockSpec((d, N_SMALL), lambda l, j: (l, (jnp.maximum(j, na - 1) + 1) * per_tn)),
        ],
        out_specs=[pl.BlockSpec((None, d, tn), lambda l, j: (l, 0, jnp.minimum(j, na - 1))),
                   pl.BlockSpec((None, d, tn), lambda l, j: (l, 0, jnp.maximum(j - na, 0))),
                   pl.BlockSpec((None, d, N_SMALL), lambda l, j: (l, 0, 0))],
        out_shape=[jax.ShapeDtypeStruct((depth, d, N_PROJ_A), BF16),
                   jax.ShapeDtypeStruct((depth, d, N_PROJ_B), BF16),
                   jax.ShapeDtypeStruct((depth, d, N_SMALL), BF16)],
        compiler_params=_cparams("parallel", "arbitrary"),
        name="split_w_in",
    )(w, w)


def _lane_row(vals, offset):
    row = jnp.zeros((1, LANES), F32)
    return row.at[0, offset:offset + vals.shape[0]].set(vals.astype(F32))


def kernel(x, attn_norm, w_in, conv_a_w, dn_conv_w, dn_a_log, dn_dt_bias, dn_norm, w_br_a, w_br_dn,
           w_br_moba, w_out, ffn_norm, w_gate_up, w_down, final_norm):
    batch, seq, d = x.shape
    depth = attn_norm.shape[0]
    assert d == D_MODEL and seq // MOBA_BLOCK <= LANES
    assert seq % (MOBA_QTILE_KEYS * MOBA_GROUP * MOBA_BLOCK) == 0
    t = batch * seq
    tm_huge = _largest_tile(t, 2048)
    assert seq % (2 * DN_TILE) == 0 and DN_TILE % DN_PAIR == 0 and (seq // DN_PAIR) % SUBLANES == 0

    inv = 1.0 / (ROPE_THETA ** (jnp.arange(0, HEAD_DIM, 2, dtype=F32) / HEAD_DIM))
    ang = jnp.arange(seq, dtype=F32)[:, None] * inv[None, :]
    cos_full = jnp.concatenate([jnp.cos(ang), jnp.cos(ang)], axis=-1)
    sin_signed = jnp.concatenate([-jnp.sin(ang), jnp.sin(ang)], axis=-1)
    block_onehot = (jnp.arange(seq)[:, None] // MOBA_BLOCK == jnp.arange(HEAD_DIM)[None, :]).astype(BF16)

    w_gate_up16, w_down16 = w_gate_up.astype(BF16), w_down.astype(BF16)
    w_br16 = [w.astype(BF16) for w in (w_br_a, w_br_dn, w_br_moba, w_out)]
    w_in16 = _split_w_in(w_in, 1024)

    xf = x.reshape(t, d)
    for l in range(depth):
        proj_a, proj_b, gates = _in_proj(xf, attn_norm[l][None, :], *w_in16, _lane_row(dn_a_log[l], HEADS),
                                         _lane_row(dn_dt_bias[l], HEADS), l, tm_huge, 1024)
        g_row = (gates[:, HEADS:2 * HEADS].reshape(batch, seq, HEADS).transpose(0, 2, 1)
                 .reshape(batch, HEADS, seq // DN_PAIR, DN_PAIR))
        y_dn = _deltanet(proj_a, gates, g_row, dn_conv_w[l], dn_norm[l][None, :], batch, seq)

        q_t, bias_t, k16, v_t = _moba_prep(proj_b, cos_full, sin_signed, batch, seq)
        y_mb = _moba_attn(q_t, bias_t, k16, block_onehot, v_t, batch, seq)

        x1 = _merge(proj_a, proj_b, conv_a_w[l], y_dn, y_mb, xf, *w_br16, l, seq, _largest_tile(t, 512))
        xf = _ffn(x1, ffn_norm[l][None, :], w_gate_up16, w_down16, l, final_norm[None, :],
                  l == depth - 1, tm_huge, 256)

    return xf.reshape(batch, seq, d)
```

```python
import functools

import jax
import jax.numpy as jnp
from jax import lax
from jax.experimental import pallas as pl
from jax.experimental.pallas import tpu as pltpu

F32 = jnp.float32
BF16 = jnp.bfloat16

D_MODEL = 1024
HEADS = 8
HEAD_DIM = 128
NORM_EPS = 1e-6
CONV_A_K = 3
DN_CONV_K = 4
DN_CHUNK = 64
DN_PAIR = 2 * DN_CHUNK
DN_TILE = 256
DN_CONV_GROUP = 128
MOBA_BLOCK = 256
MOBA_TOPK = 3
MOBA_GROUP = 2
MOBA_QTILE_KEYS = 2
LOG2_E = 1.4426950408889634
ROPE_THETA = 10000.0
FFN_HIDDEN = 2816

COL_AX, COL_AC, COL_AB = 0, 1, 2
COL_DQ, COL_DK, COL_DV, COL_DZ = 3, 4, 5, 6
N_PROJ_A = 7 * D_MODEL
COL_MQ, COL_MK, COL_MV = 0, 1, 2
COL_GATE = 3
N_PROJ_B = 6 * D_MODEL
N_SMALL = 128

MASK_NEG = -1e30
VMEM_LIMIT = 56 * 1024 * 1024
SUBLANES = 8
LANES = 128
HALO_ROWS = 16


def _cparams(*sem):
    return pltpu.CompilerParams(dimension_semantics=sem, vmem_limit_bytes=VMEM_LIMIT)


def _silu(x):
    return x * jax.nn.sigmoid(x)


def _dot(a, b):
    return jnp.dot(a, b, preferred_element_type=F32)


def _dot_nt(a, b):
    return lax.dot_general(a, b, (((1,), (1,)), ((), ())), preferred_element_type=F32)


def _dot_tn(a, b):
    return lax.dot_general(a, b, (((0,), (0,)), ((), ())), preferred_element_type=F32)


def _dn_gates(raw, alog_row, dt_row):
    tm = raw.shape[0]
    xa = raw + dt_row
    softplus = jnp.maximum(xa, 0.0) + jnp.log1p(jnp.exp(-jnp.abs(xa)))
    g = -jnp.exp(alog_row) * softplus
    row = lax.broadcasted_iota(jnp.int32, (tm, LANES), 0) % DN_CHUNK
    shift = 1
    while shift < DN_CHUNK:
        g = g + jnp.where(row >= shift, pltpu.roll(g, shift, axis=0), 0.0)
        shift *= 2
    lane = lax.broadcasted_iota(jnp.int32, (tm, LANES), 1)
    return jnp.where(lane < HEADS, jax.nn.sigmoid(raw), g)


def _in_proj_kernel(x_ref, g_ref, wa_ref, wb_ref, ws_ref, alog_ref, dt_ref, oa_ref, ob_ref, os_ref, h_ref,
                    *, na):
    j = pl.program_id(1)

    @pl.when(j == 0)
    def _():
        x = x_ref[...]
        ms = jnp.mean(x * x, axis=-1, keepdims=True)
        h_ref[...] = (x * lax.rsqrt(ms + NORM_EPS) * g_ref[...]).astype(BF16)
        os_ref[...] = _dn_gates(_dot(h_ref[...], ws_ref[...]), alog_ref[...], dt_ref[...])

    @pl.when(j < na)
    def _():
        oa_ref[...] = _dot(h_ref[...], wa_ref[...]).astype(BF16)

    @pl.when(j >= na)
    def _():
        ob_ref[...] = _dot(h_ref[...], wb_ref[...]).astype(BF16)


def _in_proj(x, gain, wa, wb, ws, alog_row, dt_row, layer, tm, tn):
    t, d = x.shape
    assert tm % DN_CHUNK == 0
    lane_row = pl.BlockSpec((1, LANES), lambda i, j: (0, 0))
    na, nb = wa.shape[2] // tn, wb.shape[2] // tn
    a_idx = lambda j: jnp.minimum(j, na - 1)
    b_idx = lambda j: jnp.maximum(j - na, 0)
    return pl.pallas_call(
        functools.partial(_in_proj_kernel, na=na),
        grid=(t // tm, na + nb),
        in_specs=[
            pl.BlockSpec((tm, d), lambda i, j: (i, 0)),
            pl.BlockSpec((1, d), lambda i, j: (0, 0)),
            pl.BlockSpec((None, d, tn), lambda i, j: (layer, 0, a_idx(j))),
            pl.BlockSpec((None, d, tn), lambda i, j: (layer, 0, b_idx(j))),
            pl.BlockSpec((None, d, N_SMALL), lambda i, j: (layer, 0, 0)),
            lane_row, lane_row,
        ],
        out_specs=[pl.BlockSpec((tm, tn), lambda i, j: (i, a_idx(j))),
                   pl.BlockSpec((tm, tn), lambda i, j: (i, b_idx(j))),
                   pl.BlockSpec((tm, N_SMALL), lambda i, j: (i, 0))],
        out_shape=[jax.ShapeDtypeStruct((t, wa.shape[2]), BF16),
                   jax.ShapeDtypeStruct((t, wb.shape[2]), BF16),
                   jax.ShapeDtypeStruct((t, N_SMALL), F32)],
        scratch_shapes=[pltpu.VMEM((tm, d), BF16)],
        compiler_params=_cparams("parallel", "arbitrary"),
        name="in_proj",
    )(x, gain, wa, wb, ws, alog_row, dt_row)


def _causal_conv(x, carry, w, k):
    xe = jnp.concatenate([carry, x], axis=0)
    y = x * w[k - 1:k]
    for j in range(1, k):
        y = y + pltpu.roll(xe, j, axis=0)[SUBLANES:] * w[k - 1 - j:k - j]
    return y


def _head_cols(h):
    return slice(h * HEAD_DIM, (h + 1) * HEAD_DIM)


def _interleave(*streams):
    order = [((k + 0.5) / len(s), si, step) for si, s in enumerate(streams) for k, step in enumerate(s)]
    for _, _, step in sorted(order, key=lambda e: e[:2]):
        step()


def _deltanet_kernel(q_ref, k_ref, v_ref, z_ref, qx_ref, kx_ref, vx_ref, gates_ref, grow_ref, cw_ref,
                     nw_ref, o_ref, carry_ref, state_ref, qkv_ref,
                     u_ref, w_ref, qd_ref, kd_ref, qk_ref):
    T = DN_TILE
    C = DN_CHUNK
    P = DN_PAIR
    hrange = range(HEADS)
    step_idx = pl.program_id(1)

    G = DN_CONV_GROUP
    sr = lax.broadcasted_iota(jnp.int32, ((DN_CONV_K - 1) * G, 2 * G), 0)
    sc = lax.broadcasted_iota(jnp.int32, ((DN_CONV_K - 1) * G, 2 * G), 1)
    shift_all = (sc == G + sr % G - (sr // G + 1)).astype(BF16)

    def pre_stream(srcs, r0, buf):
        steps = []
        for idx, src in enumerate(srcs):
            for rg in range(T // G):
                for hp in range(HEADS // 2):
                    def piece(idx=idx, src=src, rg=rg, hp=hp):
                        cols = slice(hp * 2 * HEAD_DIM, (hp + 1) * 2 * HEAD_DIM)
                        cur = src[r0 + rg * G:r0 + (rg + 1) * G, cols]
                        prev = carry_ref[idx, :, cols] if rg == 0 else src[r0 + (rg - 1) * G:r0 + rg * G, cols]
                        sh = _dot(shift_all, jnp.concatenate([prev, cur], axis=0))
                        w = cw_ref[:, idx * D_MODEL + hp * 2 * HEAD_DIM:idx * D_MODEL + (hp + 1) * 2 * HEAD_DIM]
                        y = cur.astype(F32) * w[DN_CONV_K - 1:DN_CONV_K]
                        for j in range(1, DN_CONV_K):
                            y = y + sh[(j - 1) * G:j * G] * w[DN_CONV_K - 1 - j:DN_CONV_K - j]
                        y = _silu(y)
                        for e in range(2):
                            ye = y[:, e * HEAD_DIM:(e + 1) * HEAD_DIM]
                            if idx < 2:
                                scale = HEAD_DIM ** -0.5 if idx == 0 else 1.0
                                ye = ye * (lax.rsqrt(jnp.sum(ye * ye, axis=-1, keepdims=True) + NORM_EPS) * scale)
                            qkv_ref[buf, idx, rg * G:(rg + 1) * G, _head_cols(2 * hp + e)] = ye
                    steps.append(piece)

            def save(idx=idx, src=src):
                carry_ref[idx] = src[r0 + T - G:r0 + T, :]
            steps.append(save)
        return steps

    @pl.when(step_idx == 0)
    def _():
        carry_ref[...] = jnp.zeros_like(carry_ref)
        state_ref[...] = jnp.zeros_like(state_ref)
        for step in pre_stream((q_ref, k_ref, v_ref), 0, 0):
            step()

    ri = lax.broadcasted_iota(jnp.int32, (P, P), 0)
    ci = lax.broadcasted_iota(jnp.int32, (P, P), 1)
    same_chunk = (ri // C) == (ci // C)
    causal = same_chunk & (ci <= ri)
    strict = same_chunk & (ci < ri)
    eye = (ci == ri).astype(F32)
    level_masks = []
    bs = 1
    while bs < C:
        same = (ri // (2 * bs)) == (ci // (2 * bs))
        level_masks.append(same & ((ri % (2 * bs)) >= bs) & ((ci % (2 * bs)) < bs))
        bs *= 2
    level_masks16 = [jnp.where(m, 1.0, 0.0).astype(BF16) for m in level_masks[1:]]
    first_half = lax.broadcasted_iota(jnp.int32, (P, 1), 0) < C
    pair0 = step_idx * (2 * T // P)
    nw = nw_ref[...]

    def a_stream(buf, r0):
        probs = [(sl, h) for sl in range(T // P) for h in hrange]
        n = range(len(probs))
        rows = [slice(sl * P, (sl + 1) * P) for sl in range(T // P)]
        ld = lambda idx, i: qkv_ref[buf, idx, rows[probs[i][0]], _head_cols(probs[i][1])]
        st = {}

        def s_decay():
            gbs = [gates_ref[r0 + sl * P:r0 + (sl + 1) * P, :] for sl in range(T // P)]
            st["beta"] = [gbs[sl][:, h:h + 1] for sl, h in probs]
            st["gc"] = [gbs[sl][:, HEADS + h:HEADS + h + 1] for sl, h in probs]
            gr = [grow_ref[0, h, pl.ds(pair0 + r0 // P + sl, 1), :] for sl, h in probs]
            st["decay"] = [jnp.where(causal, jnp.exp(jnp.where(causal, st["gc"][i] - gr[i], 0.0)), 0.0)
                           for i in n]

        def s_a():
            k = [ld(1, i) for i in n]
            st["kb"] = [k[i] * st["beta"][i] for i in n]
            st["k16"] = [k[i].astype(BF16) for i in n]
            st["a"] = [jnp.where(strict, _dot_nt(st["kb"][i].astype(BF16), st["k16"][i]) * st["decay"][i], 0.0)
                       for i in n]
            st["t"] = [eye - jnp.where(level_masks[0], st["a"][i], 0.0) for i in n]
            st["a16"] = [st["a"][i].astype(BF16) for i in n]

        def s_tx(m16):
            st["t16"] = [st["t"][i].astype(BF16) for i in n]
            st["tx"] = [_dot(st["t16"][i], st["a16"][i] * m16) for i in n]

        def s_t():
            st["t"] = [st["t"][i] - _dot(st["tx"][i].astype(BF16), st["t16"][i]) for i in n]

        def s_uw():
            st["eg"] = [jnp.exp(st["gc"][i]) for i in n]
            rhs = [jnp.concatenate([ld(2, i) * st["beta"][i], st["kb"][i] * st["eg"][i]], axis=1).astype(BF16)
                   for i in n]
            st["uw"] = [_dot(st["t"][i].astype(BF16), rhs[i]) for i in n]

        def s_qk():
            st["qk"] = [(_dot_nt(ld(0, i).astype(BF16), st["k16"][i]) * st["decay"][i]).astype(BF16) for i in n]

        def s_store():
            for i, (sl, h) in enumerate(probs):
                r, gc = rows[sl], st["gc"][i]
                glast = jnp.where(first_half, gc[C - 1:C, :], gc[P - 1:P, :])
                u_ref[buf, r, _head_cols(h)] = st["uw"][i][:, :HEAD_DIM]
                w_ref[buf, r, _head_cols(h)] = st["uw"][i][:, HEAD_DIM:].astype(BF16)
                qd_ref[buf, r, _head_cols(h)] = (ld(0, i) * st["eg"][i]).astype(BF16)
                kd_ref[buf, r, _head_cols(h)] = (ld(1, i) * jnp.exp(glast - gc)).astype(BF16)
                qk_ref[buf, sl * P:sl * P + C, h * C:(h + 1) * C] = st["qk"][i][:C, :C]
                qk_ref[buf, sl * P + C:(sl + 1) * P, h * C:(h + 1) * C] = st["qk"][i][C:, C:]

        steps = [s_decay, s_a]
        for m16 in level_masks16:
            steps += [functools.partial(s_tx, m16), s_t]
        return steps + [s_uw, s_qk, s_store]

    def b_stream(buf, r0):
        st = {}
        steps = []
        for c in range(T // C):
            rc = slice(c * C, (c + 1) * C)
            rg = slice(r0 + c * C, r0 + (c + 1) * C)

            def s_ws(rc=rc):
                st["s"] = [state_ref[h] for h in hrange]
                lhs = [jnp.concatenate([w_ref[buf, rc, _head_cols(h)], qd_ref[buf, rc, _head_cols(h)]], axis=0)
                       for h in hrange]
                st["ws"] = [_dot(lhs[h], st["s"][h].astype(BF16)) for h in hrange]

            def s_state(rc=rc, rg=rg):
                gb = gates_ref[rg, :]
                vn16 = [(u_ref[buf, rc, _head_cols(h)] - st["ws"][h][:C]).astype(BF16) for h in hrange]
                st["o"] = [st["ws"][h][C:] + _dot(qk_ref[buf, rc, h * C:(h + 1) * C], vn16[h]) for h in hrange]
                for h in hrange:
                    cd = jnp.exp(gb[C - 1:C, HEADS + h:HEADS + h + 1])
                    state_ref[h] = st["s"][h] * cd + _dot_tn(kd_ref[buf, rc, _head_cols(h)], vn16[h])

            def s_out(rg=rg):
                for h in hrange:
                    o = st["o"][h]
                    on = o * lax.rsqrt(jnp.mean(o * o, axis=-1, keepdims=True) + NORM_EPS) * nw
                    o_ref[rg, _head_cols(h)] = (on * _silu(z_ref[rg, _head_cols(h)].astype(F32))).astype(BF16)

            steps += [s_ws, s_state, s_out]
        return steps

    _interleave(a_stream(0, 0), pre_stream((q_ref, k_ref, v_ref), T, 1))
    _interleave(a_stream(1, T), b_stream(0, 0))
    _interleave(b_stream(1, T), pre_stream((qx_ref, kx_ref, vx_ref), 0, 0))


def _deltanet(proj, gates, g_row, conv_w, norm_w, batch, seq):
    t = proj.shape[0]
    T = DN_TILE
    L = 2 * T
    nl = seq // L
    blk = lambda col: pl.BlockSpec((L, D_MODEL), lambda b, s, col=col: (b * nl + s, col))
    nxt = lambda col: pl.BlockSpec(
        (T, D_MODEL), lambda b, s, col=col: (2 * b * nl + jnp.minimum(2 * s + 2, 2 * nl - 1), col))
    return pl.pallas_call(
        _deltanet_kernel,
        grid=(batch, nl),
        in_specs=[
            blk(COL_DQ), blk(COL_DK), blk(COL_DV), blk(COL_DZ),
            nxt(COL_DQ), nxt(COL_DK), nxt(COL_DV),
            pl.BlockSpec((L, LANES), lambda b, s: (b * nl + s, 0)),
            pl.BlockSpec((1, HEADS, seq // DN_PAIR, DN_PAIR), lambda b, s: (b, 0, 0, 0)),
            pl.BlockSpec((DN_CONV_K, 3 * D_MODEL), lambda b, s: (0, 0)),
            pl.BlockSpec((1, HEAD_DIM), lambda b, s: (0, 0)),
        ],
        out_specs=pl.BlockSpec((L, D_MODEL), lambda b, s: (b * nl + s, 0)),
        out_shape=jax.ShapeDtypeStruct((t, D_MODEL), BF16),
        scratch_shapes=[
            pltpu.VMEM((3, DN_CONV_GROUP, D_MODEL), BF16),
            pltpu.VMEM((HEADS, HEAD_DIM, HEAD_DIM), F32),
            pltpu.VMEM((2, 3, T, D_MODEL), F32),
            pltpu.VMEM((2, T, D_MODEL), F32),
            pltpu.VMEM((2, T, D_MODEL), BF16),
            pltpu.VMEM((2, T, D_MODEL), BF16),
            pltpu.VMEM((2, T, D_MODEL), BF16),
            pltpu.VMEM((2, T, HEADS * DN_CHUNK), BF16),
        ],
        compiler_params=_cparams("arbitrary", "arbitrary"),
        name="deltanet",
    )(proj, proj, proj, proj, proj, proj, proj, gates, g_row, conv_w, norm_w)


def _moba_prep_kernel(q_ref, k_ref, v_ref, cos_ref, sin_ref, qt_ref, bias_ref, k16_ref, vt_ref, kmean_ref):
    i = pl.program_id(1)
    BS = MOBA_BLOCK
    nbp = kmean_ref.shape[1]
    hrange = range(HEADS)

    @pl.when(i == 0)
    def _():
        kmean_ref[...] = jnp.zeros_like(kmean_ref)

    cos = cos_ref[...]
    sin = sin_ref[...]
    rope = lambda x: x * cos + pltpu.roll(x, HEAD_DIM // 2, axis=1) * sin
    q = [rope(q_ref[:, _head_cols(h)].astype(F32)) for h in hrange]
    k = [rope(k_ref[:, _head_cols(h)].astype(F32)) for h in hrange]

    def split(x):
        hi = x.astype(BF16)
        return hi, (x - hi.astype(F32)).astype(BF16)

    qs = [split(q[h]) for h in hrange]
    ms = [split(kmean_ref[h]) for h in hrange]
    gate = [_dot_nt(ms[h][0], qs[h][0]) + (_dot_nt(ms[h][0], qs[h][1]) + _dot_nt(ms[h][1], qs[h][0]))
            for h in hrange]
    blk = lax.broadcasted_iota(jnp.int32, (nbp, BS), 0)
    blk_f = blk.astype(F32)
    neg_inf = jnp.float32(-jnp.inf)
    g = [jnp.where(blk < i, gate[h], neg_inf) for h in hrange]
    sel = [blk == i for h in hrange]
    for _ in range(MOBA_TOPK):
        m = [jnp.max(g[h], axis=0, keepdims=True) for h in hrange]
        first = [jnp.min(jnp.where((g[h] == m[h]) & (g[h] > neg_inf), blk_f, float(nbp)),
                         axis=0, keepdims=True) for h in hrange]
        pick = [blk_f == first[h] for h in hrange]
        sel = [sel[h] | pick[h] for h in hrange]
        g = [jnp.where(pick[h], neg_inf, g[h]) for h in hrange]

    qt = [(q[h] * (HEAD_DIM ** -0.5 * LOG2_E)).T.astype(BF16) for h in hrange]
    vt = [v_ref[:, _head_cols(h)].astype(F32).T.astype(BF16) for h in hrange]
    for h in hrange:
        qt_ref[0, h] = qt[h]
        bias_ref[0, h] = jnp.where(sel[h], 0.0, MASK_NEG).astype(BF16)
        k16_ref[0, h] = k[h].astype(BF16)
        vt_ref[0, h, 0] = vt[h]
        kmean_ref[h, pl.ds(i, 1), :] = jnp.mean(k[h], axis=0, keepdims=True)


def _moba_prep(proj, cos_full, sin_signed, batch, seq):
    nb = seq // MOBA_BLOCK
    nbp = -(-nb // 16) * 16
    blk = lambda col: pl.BlockSpec((MOBA_BLOCK, D_MODEL), lambda b, i, col=col: (b * nb + i, col))
    tab = pl.BlockSpec((MOBA_BLOCK, HEAD_DIM), lambda b, i: (i, 0))
    return pl.pallas_call(
        _moba_prep_kernel,
        grid=(batch, nb),
        in_specs=[blk(COL_MQ), blk(COL_MK), blk(COL_MV), tab, tab],
        out_specs=[pl.BlockSpec((1, HEADS, HEAD_DIM, MOBA_BLOCK), lambda b, i: (b, 0, 0, i)),
                   pl.BlockSpec((1, HEADS, nbp, MOBA_BLOCK), lambda b, i: (b, 0, 0, i)),
                   pl.BlockSpec((1, HEADS, MOBA_BLOCK, HEAD_DIM), lambda b, i: (b, 0, i, 0)),
                   pl.BlockSpec((1, HEADS, 1, HEAD_DIM, MOBA_BLOCK),
                                lambda b, i: (b, 0, i // MOBA_GROUP, 0, i % MOBA_GROUP))],
        out_shape=[jax.ShapeDtypeStruct((batch, HEADS, HEAD_DIM, seq), BF16),
                   jax.ShapeDtypeStruct((batch, HEADS, nbp, seq), BF16),
                   jax.ShapeDtypeStruct((batch, HEADS, seq, HEAD_DIM), BF16),
                   jax.ShapeDtypeStruct((batch, HEADS, nb // MOBA_GROUP, HEAD_DIM,
                                         MOBA_GROUP * MOBA_BLOCK), BF16)],
        scratch_shapes=[pltpu.VMEM((HEADS, nbp, HEAD_DIM), F32)],
        compiler_params=_cparams("parallel", "arbitrary"),
        name="moba_prep",
    )(proj, proj, proj, cos_full, sin_signed)


def _moba_attn_kernel(q_ref, bias_ref, k_ref, oh_ref, vt_ref, o_ref, sa_ref, sb_ref):
    tq_idx = pl.program_id(2)
    TK = MOBA_GROUP * MOBA_BLOCK
    TQ = MOBA_QTILE_KEYS * TK
    nbp = bias_ref.shape[2]
    qt = jnp.concatenate([q_ref[0, 0], bias_ref[0, 0], jnp.zeros((HEAD_DIM - nbp, TQ), BF16)], axis=0)

    def k_aug(j):
        rows = pl.ds(pl.multiple_of(j * TK, TK), TK)
        return jnp.concatenate([k_ref[0, 0, rows, :], oh_ref[rows, :]], axis=1)

    def scores(j):
        return _dot(k_aug(j), qt)

    def attend(s, j, m, l, acc):
        m_new = jnp.maximum(m, jnp.max(s, axis=0, keepdims=True))
        alpha = jnp.exp2(m - m_new)
        p = jnp.exp2(s - m_new)
        l = alpha * l + jnp.sum(p, axis=0, keepdims=True)
        acc = alpha * acc + _dot(vt_ref[0, 0, j], p.astype(BF16))
        return m_new, l, acc

    def step(g, cur_ref, nxt_ref, carry):
        nxt_ref[...] = scores(g + 1)
        return attend(cur_ref[...], g, *carry)

    assert MOBA_QTILE_KEYS == 2
    sa_ref[...] = scores(0)
    carry = (jnp.full((1, TQ), MASK_NEG, F32), jnp.zeros((1, TQ), F32), jnp.zeros((HEAD_DIM, TQ), F32))

    def two_steps(h, carry):
        carry = step(2 * h, sa_ref, sb_ref, carry)
        return step(2 * h + 1, sb_ref, sa_ref, carry)

    def run(n):
        def body(r, c):
            for d in range(n):
                c = two_steps(n * r + d, c)
            return c
        return body

    done = 0
    for n in (4, 2, 1):
        trips = (tq_idx - done) // n
        carry = lax.fori_loop(done // n, done // n + trips, run(n), carry)
        done = done + trips * n

    own0, own1 = 2 * tq_idx, 2 * tq_idx + 1
    half = slice(TK, TQ)
    tri = (lax.broadcasted_iota(jnp.int32, (TK, TK), 0) <= lax.broadcasted_iota(jnp.int32, (TK, TK), 1))
    sb_ref[:, half] = _dot(k_aug(own1), qt[:, half])
    s0 = sa_ref[...]
    s0 = jnp.concatenate([jnp.where(tri, s0[:, :TK], MASK_NEG), s0[:, half]], axis=1)
    m, l, acc = attend(s0, own0, *carry)
    m2, l2, acc2 = attend(jnp.where(tri, sb_ref[:, half], MASK_NEG), own1, m[:, half], l[:, half], acc[:, half])
    l = jnp.concatenate([l[:, :TK], l2], axis=1)
    acc = jnp.concatenate([acc[:, :TK], acc2], axis=1)
    o_ref[...] = (acc / l).T.astype(BF16)


def _moba_attn(q_t, bias_t, k16, block_onehot, v_t, batch, seq):
    tk = MOBA_GROUP * MOBA_BLOCK
    tq = MOBA_QTILE_KEYS * tk
    nbp = bias_t.shape[2]
    return pl.pallas_call(
        _moba_attn_kernel,
        grid=(batch, HEADS, seq // tq),
        in_specs=[
            pl.BlockSpec((1, 1, HEAD_DIM, tq), lambda b, h, i: (b, h, 0, i)),
            pl.BlockSpec((1, 1, nbp, tq), lambda b, h, i: (b, h, 0, i)),
            pl.BlockSpec((1, 1, seq, HEAD_DIM), lambda b, h, i: (b, h, 0, 0)),
            pl.BlockSpec((seq, HEAD_DIM), lambda b, h, i: (0, 0)),
            pl.BlockSpec((1, 1, seq // tk, HEAD_DIM, tk), lambda b, h, i: (b, h, 0, 0, 0)),
        ],
        out_specs=pl.BlockSpec((tq, HEAD_DIM), lambda b, h, i: (b * (seq // tq) + i, h)),
        out_shape=jax.ShapeDtypeStruct((batch * seq, D_MODEL), BF16),
        scratch_shapes=[pltpu.VMEM((tk, tq), F32), pltpu.VMEM((tk, tq), F32)],
        compiler_params=_cparams("parallel", "parallel", "arbitrary"),
        name="moba_attn",
    )(q_t, bias_t, k16, block_onehot, v_t)


def _merge_kernel(ax_ref, ac_ref, ab_ref, hx_ref, hc_ref, cw_ref, ydn_ref, ymb_ref,
                  ga_ref, gd_ref, gm_ref, x_ref, wa_ref, wd_ref, wm_ref, wo_ref, o_ref, *, seq):
    tm = ax_ref.shape[0]
    f32 = lambda ref: ref[...].astype(F32)
    first = (pl.program_id(0) * tm) % seq == 0
    halo = jnp.where(first, 0.0, (f32(hx_ref) * f32(hc_ref))[HALO_ROWS - SUBLANES:])
    p = f32(ax_ref) * f32(ac_ref)
    y_a = f32(ab_ref) * _causal_conv(p, halo, cw_ref[...], CONV_A_K)
    merged = (jax.nn.sigmoid(f32(ga_ref)) * _dot(y_a.astype(BF16), wa_ref[...])
              + jax.nn.sigmoid(f32(gd_ref)) * _dot(ydn_ref[...], wd_ref[...])
              + jax.nn.sigmoid(f32(gm_ref)) * _dot(ymb_ref[...], wm_ref[...]))
    o_ref[...] = x_ref[...] + _dot(merged.astype(BF16), wo_ref[...])


def _merge(proj_a, proj_b, conv_w, y_dn, y_mb, x, wa, wd, wm, wo, layer, seq, tm):
    t = x.shape[0]
    blk = lambda col: pl.BlockSpec((tm, D_MODEL), lambda i, col=col: (i, col))
    halo = lambda col: pl.BlockSpec(
        (HALO_ROWS, D_MODEL), lambda i, col=col: (jnp.maximum(i * (tm // HALO_ROWS) - 1, 0), col))
    row = pl.BlockSpec((tm, D_MODEL), lambda i: (i, 0))
    wspec = pl.BlockSpec((None, D_MODEL, D_MODEL), lambda i: (layer, 0, 0))
    return pl.pallas_call(
        functools.partial(_merge_kernel, seq=seq),
        grid=(t // tm,),
        in_specs=[blk(COL_AX), blk(COL_AC), blk(COL_AB), halo(COL_AX), halo(COL_AC),
                  pl.BlockSpec((CONV_A_K, D_MODEL), lambda i: (0, 0)),
                  row, row, blk(COL_GATE), blk(COL_GATE + 1), blk(COL_GATE + 2), row,
                  wspec, wspec, wspec, wspec],
        out_specs=row,
        out_shape=jax.ShapeDtypeStruct((t, D_MODEL), F32),
        compiler_params=_cparams("parallel"),
        name="merge",
    )(proj_a, proj_a, proj_a, proj_a, proj_a, conv_w, y_dn, y_mb, proj_b, proj_b, proj_b, x,
      wa, wd, wm, wo)


def _rms_scale(x, gain):
    return x * lax.rsqrt(jnp.mean(x * x, axis=-1, keepdims=True) + NORM_EPS) * gain


def _ffn_kernel(x_ref, g_ref, wg_ref, wu_ref, wd_ref, fg_ref, o_ref, h_ref, acc_ref, *, final_norm):
    j = pl.program_id(1)

    @pl.when(j == 0)
    def _():
        x = x_ref[...]
        h_ref[...] = _rms_scale(x, g_ref[...]).astype(BF16)
        acc_ref[...] = x

    h = h_ref[...]
    act = _silu(_dot(h, wg_ref[...])) * _dot(h, wu_ref[...])
    acc_ref[...] += _dot(act.astype(BF16), wd_ref[...])

    @pl.when(j == pl.num_programs(1) - 1)
    def _():
        y = acc_ref[...]
        o_ref[...] = _rms_scale(y, fg_ref[...]) if final_norm else y


def _ffn(x, gain, w_gate_up, w_down, layer, final_gain, final_norm, tm, th):
    t, d = x.shape
    hid = w_down.shape[1]
    nh = hid // th
    return pl.pallas_call(
        functools.partial(_ffn_kernel, final_norm=final_norm),
        grid=(t // tm, nh),
        in_specs=[
            pl.BlockSpec((tm, d), lambda i, j: (i, 0)),
            pl.BlockSpec((1, d), lambda i, j: (0, 0)),
            pl.BlockSpec((None, d, th), lambda i, j: (layer, 0, j)),
            pl.BlockSpec((None, d, th), lambda i, j: (layer, 0, j + nh)),
            pl.BlockSpec((None, th, d), lambda i, j: (layer, j, 0)),
            pl.BlockSpec((1, d), lambda i, j: (0, 0)),
        ],
        out_specs=pl.BlockSpec((tm, d), lambda i, j: (i, 0)),
        out_shape=jax.ShapeDtypeStruct((t, d), F32),
        scratch_shapes=[pltpu.VMEM((tm, d), BF16), pltpu.VMEM((tm, d), F32)],
        compiler_params=_cparams("parallel", "arbitrary"),
        name="ffn",
    )(x, gain, w_gate_up, w_gate_up, w_down, final_gain)


def _largest_tile(n, cap):
    t = cap
    while n % t:
        t //= 2
    return t


def _split_w_in_kernel(w_ref, tail_ref, oa_ref, ob_ref, os_ref, *, na):
    j = pl.program_id(1)
    shift = 2 * HEADS

    @pl.when(j == 0)
    def _():
        lane = lax.broadcasted_iota(jnp.int32, tail_ref.shape, 1)
        os_ref[...] = jnp.where(lane < shift, tail_ref[...], 0.0).astype(BF16)

    @pl.when(j < na)
    def _():
        oa_ref[...] = w_ref[...].astype(BF16)

    @pl.when(j >= na)
    def _():
        w = jnp.concatenate([w_ref[:, shift:], tail_ref[:, :shift]], axis=1)
        ob_ref[...] = w.astype(BF16)


def _split_w_in(w, tn):
    depth, d, n = w.shape
    na, nb = N_PROJ_A // tn, N_PROJ_B // tn
    per_tn = tn // N_SMALL
    w = w.astype(BF16).reshape(depth * d, n)
    return pl.pallas_call(
        functools.partial(_split_w_in_kernel, na=na),
        grid=(depth, na + nb),
        in_specs=[
            pl.BlockSpec((d, tn), lambda l, j: (l, j)),
            pl.BlockSpec((d, N_SMALL), lambda l, j: (l, (jnp.maximum(j, na - 1) + 1) * per_tn)),
        ],
        out_specs=[pl.BlockSpec((None, d, tn), lambda l, j: (l, 0, jnp.minimum(j, na - 1))),
                   pl.BlockSpec((None, d, tn), lambda l, j: (l, 0, jnp.maximum(j - na, 0))),
                   pl.BlockSpec((None, d, N_SMALL), lambda l, j: (l, 0, 0))],
        out_shape=[jax.ShapeDtypeStruct((depth, d, N_PROJ_A), BF16),
                   jax.ShapeDtypeStruct((depth, d, N_PROJ_B), BF16),
                   jax.ShapeDtypeStruct((depth, d, N_SMALL), BF16)],
        compiler_params=_cparams("parallel", "arbitrary"),
        name="split_w_in",
    )(w, w)


def _lane_row(vals, offset):
    row = jnp.zeros((1, LANES), F32)
    return row.at[0, offset:offset + vals.shape[0]].set(vals.astype(F32))


def kernel(x, attn_norm, w_in, conv_a_w, dn_conv_w, dn_a_log, dn_dt_bias, dn_norm, w_br_a, w_br_dn,
           w_br_moba, w_out, ffn_norm, w_gate_up, w_down, final_norm):
    batch, seq, d = x.shape
    depth = attn_norm.shape[0]
    assert d == D_MODEL and seq // MOBA_BLOCK <= LANES
    assert seq % (MOBA_QTILE_KEYS * MOBA_GROUP * MOBA_BLOCK) == 0
    t = batch * seq
    tm_huge = _largest_tile(t, 2048)
    assert seq % (2 * DN_TILE) == 0 and DN_TILE % DN_PAIR == 0 and (seq // DN_PAIR) % SUBLANES == 0

    inv = 1.0 / (ROPE_THETA ** (jnp.arange(0, HEAD_DIM, 2, dtype=F32) / HEAD_DIM))
    ang = jnp.arange(seq, dtype=F32)[:, None] * inv[None, :]
    cos_full = jnp.concatenate([jnp.cos(ang), jnp.cos(ang)], axis=-1)
    sin_signed = jnp.concatenate([-jnp.sin(ang), jnp.sin(ang)], axis=-1)
    block_onehot = (jnp.arange(seq)[:, None] // MOBA_BLOCK == jnp.arange(HEAD_DIM)[None, :]).astype(BF16)

    w_gate_up16, w_down16 = w_gate_up.astype(BF16), w_down.astype(BF16)
    w_br16 = [w.astype(BF16) for w in (w_br_a, w_br_dn, w_br_moba, w_out)]
    w_in16 = _split_w_in(w_in, 1024)

    xf = x.reshape(t, d)
    for l in range(depth):
        proj_a, proj_b, gates = _in_proj(xf, attn_norm[l][None, :], *w_in16, _lane_row(dn_a_log[l], HEADS),
                                         _lane_row(dn_dt_bias[l], HEADS), l, tm_huge, 1024)
        g_row = (gates[:, HEADS:2 * HEADS].reshape(batch, seq, HEADS).transpose(0, 2, 1)
                 .reshape(batch, HEADS, seq // DN_PAIR, DN_PAIR))
        y_dn = _deltanet(proj_a, gates, g_row, dn_conv_w[l], dn_norm[l][None, :], batch, seq)

        q_t, bias_t, k16, v_t = _moba_prep(proj_b, cos_full, sin_signed, batch, seq)
        y_mb = _moba_attn(q_t, bias_t, k16, block_onehot, v_t, batch, seq)

        x1 = _merge(proj_a, proj_b, conv_a_w[l], y_dn, y_mb, xf, *w_br16, l, seq, _largest_tile(t, 512))
        xf = _ffn(x1, ffn_norm[l][None, :], w_gate_up16, w_down16, l, final_norm[None, :],
                  l == depth - 1, tm_huge, 256)

    return xf.reshape(batch, seq, d)
```

```python
import functools

import jax
import jax.numpy as jnp
from jax import lax
from jax.experimental import pallas as pl
from jax.experimental.pallas import tpu as pltpu

F32 = jnp.float32
BF16 = jnp.bfloat16

D_MODEL = 1024
HEADS = 8
HEAD_DIM = 128
NORM_EPS = 1e-6
CONV_A_K = 3
DN_CONV_K = 4
DN_CHUNK = 64
DN_PAIR = 2 * DN_CHUNK
DN_TILE = 256
DN_CONV_GROUP = 128
MOBA_BLOCK = 256
MOBA_TOPK = 3
MOBA_GROUP = 2
MOBA_QTILE_KEYS = 2
LOG2_E = 1.4426950408889634
ROPE_THETA = 10000.0
FFN_HIDDEN = 2816

COL_AX, COL_AC, COL_AB = 0, 1, 2
COL_DQ, COL_DK, COL_DV, COL_DZ = 3, 4, 5, 6
N_PROJ_A = 7 * D_MODEL
COL_MQ, COL_MK, COL_MV = 0, 1, 2
COL_GATE = 3
N_PROJ_B = 6 * D_MODEL
N_SMALL = 128

MASK_NEG = -1e30
VMEM_LIMIT = 56 * 1024 * 1024
SUBLANES = 8
LANES = 128
HALO_ROWS = 16


def _cparams(*sem):
    return pltpu.CompilerParams(dimension_semantics=sem, vmem_limit_bytes=VMEM_LIMIT)


def _silu(x):
    return x * jax.nn.sigmoid(x)


def _dot(a, b):
    return jnp.dot(a, b, preferred_element_type=F32)


def _dot_nt(a, b):
    return lax.dot_general(a, b, (((1,), (1,)), ((), ())), preferred_element_type=F32)


def _dot_tn(a, b):
    return lax.dot_general(a, b, (((0,), (0,)), ((), ())), preferred_element_type=F32)


def _dn_gates(raw, alog_row, dt_row):
    tm = raw.shape[0]
    xa = raw + dt_row
    softplus = jnp.maximum(xa, 0.0) + jnp.log1p(jnp.exp(-jnp.abs(xa)))
    g = -jnp.exp(alog_row) * softplus
    row = lax.broadcasted_iota(jnp.int32, (tm, LANES), 0) % DN_CHUNK
    shift = 1
    while shift < DN_CHUNK:
        g = g + jnp.where(row >= shift, pltpu.roll(g, shift, axis=0), 0.0)
        shift *= 2
    lane = lax.broadcasted_iota(jnp.int32, (tm, LANES), 1)
    return jnp.where(lane < HEADS, jax.nn.sigmoid(raw), g)


def _in_proj_kernel(x_ref, g_ref, wa_ref, wb_ref, ws_ref, alog_ref, dt_ref, oa_ref, ob_ref, os_ref, h_ref,
                    *, na):
    j = pl.program_id(1)

    @pl.when(j == 0)
    def _():
        x = x_ref[...]
        ms = jnp.mean(x * x, axis=-1, keepdims=True)
        h_ref[...] = (x * lax.rsqrt(ms + NORM_EPS) * g_ref[...]).astype(BF16)
        os_ref[...] = _dn_gates(_dot(h_ref[...], ws_ref[...]), alog_ref[...], dt_ref[...])

    @pl.when(j < na)
    def _():
        oa_ref[...] = _dot(h_ref[...], wa_ref[...]).astype(BF16)

    @pl.when(j >= na)
    def _():
        ob_ref[...] = _dot(h_ref[...], wb_ref[...]).astype(BF16)


def _in_proj(x, gain, wa, wb, ws, alog_row, dt_row, layer, tm, tn):
    t, d = x.shape
    assert tm % DN_CHUNK == 0
    lane_row = pl.BlockSpec((1, LANES), lambda i, j: (0, 0))
    na, nb = wa.shape[2] // tn, wb.shape[2] // tn
    a_idx = lambda j: jnp.minimum(j, na - 1)
    b_idx = lambda j: jnp.maximum(j - na, 0)
    return pl.pallas_call(
        functools.partial(_in_proj_kernel, na=na),
        grid=(t // tm, na + nb),
        in_specs=[
            pl.BlockSpec((tm, d), lambda i, j: (i, 0)),
            pl.BlockSpec((1, d), lambda i, j: (0, 0)),
            pl.BlockSpec((None, d, tn), lambda i, j: (layer, 0, a_idx(j))),
            pl.BlockSpec((None, d, tn), lambda i, j: (layer, 0, b_idx(j))),
            pl.BlockSpec((None, d, N_SMALL), lambda i, j: (layer, 0, 0)),
            lane_row, lane_row,
        ],
        out_specs=[pl.BlockSpec((tm, tn), lambda i, j: (i, a_idx(j))),
                   pl.BlockSpec((tm, tn), lambda i, j: (i, b_idx(j))),
                   pl.BlockSpec((tm, N_SMALL), lambda i, j: (i, 0))],
        out_shape=[jax.ShapeDtypeStruct((t, wa.shape[2]), BF16),
                   jax.ShapeDtypeStruct((t, wb.shape[2]), BF16),
                   jax.ShapeDtypeStruct((t, N_SMALL), F32)],
        scratch_shapes=[pltpu.VMEM((tm, d), BF16)],
        compiler_params=_cparams("parallel", "arbitrary"),
        name="in_proj",
    )(x, gain, wa, wb, ws, alog_row, dt_row)


def _causal_conv(x, carry, w, k):
    xe = jnp.concatenate([carry, x], axis=0)
    y = x * w[k - 1:k]
    for j in range(1, k):
        y = y + pltpu.roll(xe, j, axis=0)[SUBLANES:] * w[k - 1 - j:k - j]
    return y


def _head_cols(h):
    return slice(h * HEAD_DIM, (h + 1) * HEAD_DIM)


def _interleave(*streams):
    order = [((k + 0.5) / len(s), si, step) for si, s in enumerate(streams) for k, step in enumerate(s)]
    for _, _, step in sorted(order, key=lambda e: e[:2]):
        step()


def _deltanet_kernel(q_ref, k_ref, v_ref, z_ref, qx_ref, kx_ref, vx_ref, gates_ref, grow_ref, cw_ref,
                     nw_ref, o_ref, carry_ref, state_ref, qkv_ref,
                     u_ref, w_ref, qd_ref, kd_ref, qk_ref):
    T = DN_TILE
    C = DN_CHUNK
    P = DN_PAIR
    hrange = range(HEADS)
    step_idx = pl.program_id(1)

    G = DN_CONV_GROUP
    sr = lax.broadcasted_iota(jnp.int32, ((DN_CONV_K - 1) * G, 2 * G), 0)
    sc = lax.broadcasted_iota(jnp.int32, ((DN_CONV_K - 1) * G, 2 * G), 1)
    shift_all = (sc == G + sr % G - (sr // G + 1)).astype(BF16)

    def pre_stream(srcs, r0, buf):
        steps = []
        for idx, src in enumerate(srcs):
            for rg in range(T // G):
                for hp in range(HEADS // 2):
                    def piece(idx=idx, src=src, rg=rg, hp=hp):
                        cols = slice(hp * 2 * HEAD_DIM, (hp + 1) * 2 * HEAD_DIM)
                        cur = src[r0 + rg * G:r0 + (rg + 1) * G, cols]
                        prev = carry_ref[idx, :, cols] if rg == 0 else src[r0 + (rg - 1) * G:r0 + rg * G, cols]
                        sh = _dot(shift_all, jnp.concatenate([prev, cur], axis=0))
                        w = cw_ref[:, idx * D_MODEL + hp * 2 * HEAD_DIM:idx * D_MODEL + (hp + 1) * 2 * HEAD_DIM]
                        y = cur.astype(F32) * w[DN_CONV_K - 1:DN_CONV_K]
                        for j in range(1, DN_CONV_K):
                            y = y + sh[(j - 1) * G:j * G] * w[DN_CONV_K - 1 - j:DN_CONV_K - j]
                        y = _silu(y)
                        for e in range(2):
                            ye = y[:, e * HEAD_DIM:(e + 1) * HEAD_DIM]
                            if idx < 2:
                                scale = HEAD_DIM ** -0.5 if idx == 0 else 1.0
                                ye = ye * (lax.rsqrt(jnp.sum(ye * ye, axis=-1, keepdims=True) + NORM_EPS) * scale)
                            qkv_ref[buf, idx, rg * G:(rg + 1) * G, _head_cols(2 * hp + e)] = ye
                    steps.append(piece)

            def save(idx=idx, src=src):
                carry_ref[idx] = src[r0 + T - G:r0 + T, :]
            steps.append(save)
        return steps

    @pl.when(step_idx == 0)
    def _():
        carry_ref[...] = jnp.zeros_like(carry_ref)
        state_ref[...] = jnp.zeros_like(state_ref)
        for step in pre_stream((q_ref, k_ref, v_ref), 0, 0):
            step()

    ri = lax.broadcasted_iota(jnp.int32, (P, P), 0)
    ci = lax.broadcasted_iota(jnp.int32, (P, P), 1)
    same_chunk = (ri // C) == (ci // C)
    causal = same_chunk & (ci <= ri)
    strict = same_chunk & (ci < ri)
    eye = (ci == ri).astype(F32)
    level_masks = []
    bs = 1
    while bs < C:
        same = (ri // (2 * bs)) == (ci // (2 * bs))
        level_masks.append(same & ((ri % (2 * bs)) >= bs) & ((ci % (2 * bs)) < bs))
        bs *= 2
    level_masks16 = [jnp.where(m, 1.0, 0.0).astype(BF16) for m in level_masks[1:]]
    first_half = lax.broadcasted_iota(jnp.int32, (P, 1), 0) < C
    pair0 = step_idx * (2 * T // P)
    nw = nw_ref[...]

    def a_stream(buf, r0):
        probs = [(sl, h) for sl in range(T // P) for h in hrange]
        n = range(len(probs))
        rows = [slice(sl * P, (sl + 1) * P) for sl in range(T // P)]
        ld = lambda idx, i: qkv_ref[buf, idx, rows[probs[i][0]], _head_cols(probs[i][1])]
        st = {}

        def s_decay():
            gbs = [gates_ref[r0 + sl * P:r0 + (sl + 1) * P, :] for sl in range(T // P)]
            st["beta"] = [gbs[sl][:, h:h + 1] for sl, h in probs]
            st["gc"] = [gbs[sl][:, HEADS + h:HEADS + h + 1] for sl, h in probs]
            gr = [grow_ref[0, h, pl.ds(pair0 + r0 // P + sl, 1), :] for sl, h in probs]
            st["decay"] = [jnp.where(causal, jnp.exp(jnp.where(causal, st["gc"][i] - gr[i], 0.0)), 0.0)
                           for i in n]

        def s_a():
            k = [ld(1, i) for i in n]
            st["kb"] = [k[i] * st["beta"][i] for i in n]
            st["k16"] = [k[i].astype(BF16) for i in n]
            st["a"] = [jnp.where(strict, _dot_nt(st["kb"][i].astype(BF16), st["k16"][i]) * st["decay"][i], 0.0)
                       for i in n]
            st["t"] = [eye - jnp.where(level_masks[0], st["a"][i], 0.0) for i in n]
            st["a16"] = [st["a"][i].astype(BF16) for i in n]

        def s_tx(m16):
            st["t16"] = [st["t"][i].astype(BF16) for i in n]
            st["tx"] = [_dot(st["t16"][i], st["a16"][i] * m16) for i in n]

        def s_t():
            st["t"] = [st["t"][i] - _dot(st["tx"][i].astype(BF16), st["t16"][i]) for i in n]

        def s_uw():
            st["eg"] = [jnp.exp(st["gc"][i]) for i in n]
            rhs = [jnp.concatenate([ld(2, i) * st["beta"][i], st["kb"][i] * st["eg"][i]], axis=1).astype(BF16)
                   for i in n]
            st["uw"] = [_dot(st["t"][i].astype(BF16), rhs[i]) for i in n]

        def s_qk():
            st["qk"] = [(_dot_nt(ld(0, i).astype(BF16), st["k16"][i]) * st["decay"][i]).astype(BF16) for i in n]

        def s_store():
            for i, (sl, h) in enumerate(probs):
                r, gc = rows[sl], st["gc"][i]
                glast = jnp.where(first_half, gc[C - 1:C, :], gc[P - 1:P, :])
                u_ref[buf, r, _head_cols(h)] = st["uw"][i][:, :HEAD_DIM]
                w_ref[buf, r, _head_cols(h)] = st["uw"][i][:, HEAD_DIM:].astype(BF16)
                qd_ref[buf, r, _head_cols(h)] = (ld(0, i) * st["eg"][i]).astype(BF16)
                kd_ref[buf, r, _head_cols(h)] = (ld(1, i) * jnp.exp(glast - gc)).astype(BF16)
                qk_ref[buf, sl * P:sl * P + C, h * C:(h + 1) * C] = st["qk"][i][:C, :C]
                qk_ref[buf, sl * P + C:(sl + 1) * P, h * C:(h + 1) * C] = st["qk"][i][C:, C:]

        steps = [s_decay, s_a]
        for m16 in level_masks16:
            steps += [functools.partial(s_tx, m16), s_t]
        return steps + [s_uw, s_qk, s_store]

    def b_stream(buf, r0):
        st = {}
        steps = []
        for c in range(T // C):
            rc = slice(c * C, (c + 1) * C)
            rg = slice(r0 + c * C, r0 + (c + 1) * C)

            def s_ws(rc=rc):
                st["s"] = [state_ref[h] for h in hrange]
                lhs = [jnp.concatenate([w_ref[buf, rc, _head_cols(h)], qd_ref[buf, rc, _head_cols(h)]], axis=0)
                       for h in hrange]
                st["ws"] = [_dot(lhs[h], st["s"][h].astype(BF16)) for h in hrange]

            def s_state(rc=rc, rg=rg):
                gb = gates_ref[rg, :]
                vn16 = [(u_ref[buf, rc, _head_cols(h)] - st["ws"][h][:C]).astype(BF16) for h in hrange]
                st["o"] = [st["ws"][h][C:] + _dot(qk_ref[buf, rc, h * C:(h + 1) * C], vn16[h]) for h in hrange]
                for h in hrange:
                    cd = jnp.exp(gb[C - 1:C, HEADS + h:HEADS + h + 1])
                    state_ref[h] = st["s"][h] * cd + _dot_tn(kd_ref[buf, rc, _head_cols(h)], vn16[h])

            def s_out(rg=rg):
                for h in hrange:
                    o = st["o"][h]
                    on = o * lax.rsqrt(jnp.mean(o * o, axis=-1, keepdims=True) + NORM_EPS) * nw
                    o_ref[rg, _head_cols(h)] = (on * _silu(z_ref[rg, _head_cols(h)].astype(F32))).astype(BF16)

            steps += [s_ws, s_state, s_out]
        return steps

    _interleave(a_stream(0, 0), pre_stream((q_ref, k_ref, v_ref), T, 1))
    _interleave(a_stream(1, T), b_stream(0, 0))
    _interleave(b_stream(1, T), pre_stream((qx_ref, kx_ref, vx_ref), 0, 0))


def _deltanet(proj, gates, g_row, conv_w, norm_w, batch, seq):
    t = proj.shape[0]
    T = DN_TILE
    L = 2 * T
    nl = seq // L
    blk = lambda col: pl.BlockSpec((L, D_MODEL), lambda b, s, col=col: (b * nl + s, col))
    nxt = lambda col: pl.BlockSpec(
        (T, D_MODEL), lambda b, s, col=col: (2 * b * nl + jnp.minimum(2 * s + 2, 2 * nl - 1), col))
    return pl.pallas_call(
        _deltanet_kernel,
        grid=(batch, nl),
        in_specs=[
            blk(COL_DQ), blk(COL_DK), blk(COL_DV), blk(COL_DZ),
            nxt(COL_DQ), nxt(COL_DK), nxt(COL_DV),
            pl.BlockSpec((L, LANES), lambda b, s: (b * nl + s, 0)),
            pl.BlockSpec((1, HEADS, seq // DN_PAIR, DN_PAIR), lambda b, s: (b, 0, 0, 0)),
            pl.BlockSpec((DN_CONV_K, 3 * D_MODEL), lambda b, s: (0, 0)),
            pl.BlockSpec((1, HEAD_DIM), lambda b, s: (0, 0)),
        ],
        out_specs=pl.BlockSpec((L, D_MODEL), lambda b, s: (b * nl + s, 0)),
        out_shape=jax.ShapeDtypeStruct((t, D_MODEL), BF16),
        scratch_shapes=[
            pltpu.VMEM((3, DN_CONV_GROUP, D_MODEL), BF16),
            pltpu.VMEM((HEADS, HEAD_DIM, HEAD_DIM), F32),
            pltpu.VMEM((2, 3, T, D_MODEL), F32),
            pltpu.VMEM((2, T, D_MODEL), F32),
            pltpu.VMEM((2, T, D_MODEL), BF16),
            pltpu.VMEM((2, T, D_MODEL), BF16),
            pltpu.VMEM((2, T, D_MODEL), BF16),
            pltpu.VMEM((2, T, HEADS * DN_CHUNK), BF16),
        ],
        compiler_params=_cparams("arbitrary", "arbitrary"),
        name="deltanet",
    )(proj, proj, proj, proj, proj, proj, proj, gates, g_row, conv_w, norm_w)


def _moba_prep_kernel(q_ref, k_ref, v_ref, cos_ref, sin_ref, qt_ref, bias_ref, k16_ref, vt_ref, kmean_ref):
    i = pl.program_id(1)
    BS = MOBA_BLOCK
    nbp = kmean_ref.shape[1]
    hrange = range(HEADS)

    @pl.when(i == 0)
    def _():
        kmean_ref[...] = jnp.zeros_like(kmean_ref)

    cos = cos_ref[...]
    sin = sin_ref[...]
    rope = lambda x: x * cos + pltpu.roll(x, HEAD_DIM // 2, axis=1) * sin
    q = [rope(q_ref[:, _head_cols(h)].astype(F32)) for h in hrange]
    k = [rope(k_ref[:, _head_cols(h)].astype(F32)) for h in hrange]

    def split(x):
        hi = x.astype(BF16)
        return hi, (x - hi.astype(F32)).astype(BF16)

    qs = [split(q[h]) for h in hrange]
    ms = [split(kmean_ref[h]) for h in hrange]
    gate = [_dot_nt(ms[h][0], qs[h][0]) + (_dot_nt(ms[h][0], qs[h][1]) + _dot_nt(ms[h][1], qs[h][0]))
            for h in hrange]
    blk = lax.broadcasted_iota(jnp.int32, (nbp, BS), 0)
    blk_f = blk.astype(F32)
    neg_inf = jnp.float32(-jnp.inf)
    g = [jnp.where(blk < i, gate[h], neg_inf) for h in hrange]
    sel = [blk == i for h in hrange]
    for _ in range(MOBA_TOPK):
        m = [jnp.max(g[h], axis=0, keepdims=True) for h in hrange]
        first = [jnp.min(jnp.where((g[h] == m[h]) & (g[h] > neg_inf), blk_f, float(nbp)),
                         axis=0, keepdims=True) for h in hrange]
        pick = [blk_f == first[h] for h in hrange]
        sel = [sel[h] | pick[h] for h in hrange]
        g = [jnp.where(pick[h], neg_inf, g[h]) for h in hrange]

    qt = [(q[h] * (HEAD_DIM ** -0.5 * LOG2_E)).T.astype(BF16) for h in hrange]
    vt = [v_ref[:, _head_cols(h)].astype(F32).T.astype(BF16) for h in hrange]
    for h in hrange:
        qt_ref[0, h] = qt[h]
        bias_ref[0, h] = jnp.where(sel[h], 0.0, MASK_NEG).astype(BF16)
        k16_ref[0, h] = k[h].astype(BF16)
        vt_ref[0, h, 0] = vt[h]
        kmean_ref[h, pl.ds(i, 1), :] = jnp.mean(k[h], axis=0, keepdims=True)


def _moba_prep(proj, cos_full, sin_signed, batch, seq):
    nb = seq // MOBA_BLOCK
    nbp = -(-nb // 16) * 16
    blk = lambda col: pl.BlockSpec((MOBA_BLOCK, D_MODEL), lambda b, i, col=col: (b * nb + i, col))
    tab = pl.BlockSpec((MOBA_BLOCK, HEAD_DIM), lambda b, i: (i, 0))
    return pl.pallas_call(
        _moba_prep_kernel,
        grid=(batch, nb),
        in_specs=[blk(COL_MQ), blk(COL_MK), blk(COL_MV), tab, tab],
        out_specs=[pl.BlockSpec((1, HEADS, HEAD_DIM, MOBA_BLOCK), lambda b, i: (b, 0, 0, i)),
                   pl.BlockSpec((1, HEADS, nbp, MOBA_BLOCK), lambda b, i: (b, 0, 0, i)),
                   pl.BlockSpec((1, HEADS, MOBA_BLOCK, HEAD_DIM), lambda b, i: (b, 0, i, 0)),
                   pl.BlockSpec((1, HEADS, 1, HEAD_DIM, MOBA_BLOCK),
                                lambda b, i: (b, 0, i // MOBA_GROUP, 0, i % MOBA_GROUP))],
        out_shape=[jax.ShapeDtypeStruct((batch, HEADS, HEAD_DIM, seq), BF16),
                   jax.ShapeDtypeStruct((batch, HEADS, nbp, seq), BF16),
                   jax.ShapeDtypeStruct((batch, HEADS, seq, HEAD_DIM), BF16),
                   jax.ShapeDtypeStruct((batch, HEADS, nb // MOBA_GROUP, HEAD_DIM,
                                         MOBA_GROUP * MOBA_BLOCK), BF16)],
        scratch_shapes=[pltpu.VMEM((HEADS, nbp, HEAD_DIM), F32)],
        compiler_params=_cparams("parallel", "arbitrary"),
        name="moba_prep",
    )(proj, proj, proj, cos_full, sin_signed)


def _moba_attn_kernel(q_ref, bias_ref, k_ref, oh_ref, vt_ref, o_ref, sa_ref, sb_ref, m_ref, l_ref, acc_ref):
    tq_idx = pl.program_id(2)
    TK = MOBA_GROUP * MOBA_BLOCK
    TQ = MOBA_QTILE_KEYS * TK
    nbp = bias_ref.shape[2]
    qt = jnp.concatenate([q_ref[0, 0], bias_ref[0, 0], jnp.zeros((HEAD_DIM - nbp, TQ), BF16)], axis=0)

    def k_aug(j):
        rows = pl.ds(pl.multiple_of(j * TK, TK), TK)
        return jnp.concatenate([k_ref[0, 0, rows, :], oh_ref[rows, :]], axis=1)

    def scores(j):
        return _dot(k_aug(j), qt)

    def attend(s, j, lanes=slice(None)):
        m = m_ref[:, lanes]
        m_new = jnp.maximum(m, jnp.max(s, axis=0, keepdims=True))
        alpha = jnp.exp2(m - m_new)
        p = jnp.exp2(s - m_new)
        m_ref[:, lanes] = m_new
        l_ref[:, lanes] = alpha * l_ref[:, lanes] + jnp.sum(p, axis=0, keepdims=True)
        acc_ref[:, lanes] = alpha * acc_ref[:, lanes] + _dot(vt_ref[0, 0, j], p.astype(BF16))

    def step(g, cur_ref, nxt_ref):
        nxt_ref[...] = scores(g + 1)
        attend(cur_ref[...], g)

    assert MOBA_QTILE_KEYS == 2
    sa_ref[...] = scores(0)
    m_ref[...] = jnp.full((1, TQ), MASK_NEG, F32)
    l_ref[...] = jnp.zeros((1, TQ), F32)
    acc_ref[...] = jnp.zeros((HEAD_DIM, TQ), F32)

    def two_steps(h, c=0):
        step(2 * h, sa_ref, sb_ref)
        step(2 * h + 1, sb_ref, sa_ref)
        return c

    lax.fori_loop(0, tq_idx // 2, lambda h, c: two_steps(2 * h + 1, two_steps(2 * h, c)), 0)
    lax.fori_loop(tq_idx // 2 * 2, tq_idx, two_steps, 0)

    own0, own1 = 2 * tq_idx, 2 * tq_idx + 1
    half = slice(TK, TQ)
    tri = (lax.broadcasted_iota(jnp.int32, (TK, TK), 0) <= lax.broadcasted_iota(jnp.int32, (TK, TK), 1))
    sb_ref[:, half] = _dot(k_aug(own1), qt[:, half])
    s0 = sa_ref[...]
    attend(jnp.concatenate([jnp.where(tri, s0[:, :TK], MASK_NEG), s0[:, half]], axis=1), own0)
    attend(jnp.where(tri, sb_ref[:, half], MASK_NEG), own1, half)
    o_ref[...] = (acc_ref[...] / l_ref[...]).T.astype(BF16)


def _moba_attn(q_t, bias_t, k16, block_onehot, v_t, batch, seq):
    tk = MOBA_GROUP * MOBA_BLOCK
    tq = MOBA_QTILE_KEYS * tk
    nbp = bias_t.shape[2]
    return pl.pallas_call(
        _moba_attn_kernel,
        grid=(batch, HEADS, seq // tq),
        in_specs=[
            pl.BlockSpec((1, 1, HEAD_DIM, tq), lambda b, h, i: (b, h, 0, i)),
            pl.BlockSpec((1, 1, nbp, tq), lambda b, h, i: (b, h, 0, i)),
            pl.BlockSpec((1, 1, seq, HEAD_DIM), lambda b, h, i: (b, h, 0, 0)),
            pl.BlockSpec((seq, HEAD_DIM), lambda b, h, i: (0, 0)),
            pl.BlockSpec((1, 1, seq // tk, HEAD_DIM, tk), lambda b, h, i: (b, h, 0, 0, 0)),
        ],
        out_specs=pl.BlockSpec((tq, HEAD_DIM), lambda b, h, i: (b * (seq // tq) + i, h)),
        out_shape=jax.ShapeDtypeStruct((batch * seq, D_MODEL), BF16),
        scratch_shapes=[pltpu.VMEM((tk, tq), F32), pltpu.VMEM((tk, tq), F32),
                        pltpu.VMEM((1, tq), F32), pltpu.VMEM((1, tq), F32), pltpu.VMEM((HEAD_DIM, tq), F32)],
        compiler_params=_cparams("parallel", "parallel", "arbitrary"),
        name="moba_attn",
    )(q_t, bias_t, k16, block_onehot, v_t)


def _merge_kernel(ax_ref, ac_ref, ab_ref, hx_ref, hc_ref, cw_ref, ydn_ref, ymb_ref,
                  ga_ref, gd_ref, gm_ref, x_ref, wa_ref, wd_ref, wm_ref, wo_ref, o_ref, *, seq):
    tm = ax_ref.shape[0]
    f32 = lambda ref: ref[...].astype(F32)
    first = (pl.program_id(0) * tm) % seq == 0
    halo = jnp.where(first, 0.0, (f32(hx_ref) * f32(hc_ref))[HALO_ROWS - SUBLANES:])
    p = f32(ax_ref) * f32(ac_ref)
    y_a = f32(ab_ref) * _causal_conv(p, halo, cw_ref[...], CONV_A_K)
    merged = (jax.nn.sigmoid(f32(ga_ref)) * _dot(y_a.astype(BF16), wa_ref[...])
              + jax.nn.sigmoid(f32(gd_ref)) * _dot(ydn_ref[...], wd_ref[...])
              + jax.nn.sigmoid(f32(gm_ref)) * _dot(ymb_ref[...], wm_ref[...]))
    o_ref[...] = x_ref[...] + _dot(merged.astype(BF16), wo_ref[...])


def _merge(proj_a, proj_b, conv_w, y_dn, y_mb, x, wa, wd, wm, wo, layer, seq, tm):
    t = x.shape[0]
    blk = lambda col: pl.BlockSpec((tm, D_MODEL), lambda i, col=col: (i, col))
    halo = lambda col: pl.BlockSpec(
        (HALO_ROWS, D_MODEL), lambda i, col=col: (jnp.maximum(i * (tm // HALO_ROWS) - 1, 0), col))
    row = pl.BlockSpec((tm, D_MODEL), lambda i: (i, 0))
    wspec = pl.BlockSpec((None, D_MODEL, D_MODEL), lambda i: (layer, 0, 0))
    return pl.pallas_call(
        functools.partial(_merge_kernel, seq=seq),
        grid=(t // tm,),
        in_specs=[blk(COL_AX), blk(COL_AC), blk(COL_AB), halo(COL_AX), halo(COL_AC),
                  pl.BlockSpec((CONV_A_K, D_MODEL), lambda i: (0, 0)),
                  row, row, blk(COL_GATE), blk(COL_GATE + 1), blk(COL_GATE + 2), row,
                  wspec, wspec, wspec, wspec],
        out_specs=row,
        out_shape=jax.ShapeDtypeStruct((t, D_MODEL), F32),
        compiler_params=_cparams("parallel"),
        name="merge",
    )(proj_a, proj_a, proj_a, proj_a, proj_a, conv_w, y_dn, y_mb, proj_b, proj_b, proj_b, x,
      wa, wd, wm, wo)


def _rms_scale(x, gain):
    return x * lax.rsqrt(jnp.mean(x * x, axis=-1, keepdims=True) + NORM_EPS) * gain


def _ffn_kernel(x_ref, g_ref, wg_ref, wu_ref, wd_ref, fg_ref, o_ref, h_ref, acc_ref, *, final_norm):
    j = pl.program_id(1)

    @pl.when(j == 0)
    def _():
        x = x_ref[...]
        h_ref[...] = _rms_scale(x, g_ref[...]).astype(BF16)
        acc_ref[...] = x

    h = h_ref[...]
    act = _silu(_dot(h, wg_ref[...])) * _dot(h, wu_ref[...])
    acc_ref[...] += _dot(act.astype(BF16), wd_ref[...])

    @pl.when(j == pl.num_programs(1) - 1)
    def _():
        y = acc_ref[...]
        o_ref[...] = _rms_scale(y, fg_ref[...]) if final_norm else y


def _ffn(x, gain, w_gate_up, w_down, layer, final_gain, final_norm, tm, th):
    t, d = x.shape
    hid = w_down.shape[1]
    nh = hid // th
    return pl.pallas_call(
        functools.partial(_ffn_kernel, final_norm=final_norm),
        grid=(t // tm, nh),
        in_specs=[
            pl.BlockSpec((tm, d), lambda i, j: (i, 0)),
            pl.BlockSpec((1, d), lambda i, j: (0, 0)),
            pl.BlockSpec((None, d, th), lambda i, j: (layer, 0, j)),
            pl.BlockSpec((None, d, th), lambda i, j: (layer, 0, j + nh)),
            pl.BlockSpec((None, th, d), lambda i, j: (layer, j, 0)),
            pl.BlockSpec((1, d), lambda i, j: (0, 0)),
        ],
        out_specs=pl.BlockSpec((tm, d), lambda i, j: (i, 0)),
        out_shape=jax.ShapeDtypeStruct((t, d), F32),
        scratch_shapes=[pltpu.VMEM((tm, d), BF16), pltpu.VMEM((tm, d), F32)],
        compiler_params=_cparams("parallel", "arbitrary"),
        name="ffn",
    )(x, gain, w_gate_up, w_gate_up, w_down, final_gain)


def _largest_tile(n, cap):
    t = cap
    while n % t:
        t //= 2
    return t


def _split_w_in_kernel(w_ref, tail_ref, oa_ref, ob_ref, os_ref, *, na):
    j = pl.program_id(1)
    shift = 2 * HEADS

    @pl.when(j == 0)
    def _():
        lane = lax.broadcasted_iota(jnp.int32, tail_ref.shape, 1)
        os_ref[...] = jnp.where(lane < shift, tail_ref[...], 0.0).astype(BF16)

    @pl.when(j < na)
    def _():
        oa_ref[...] = w_ref[...].astype(BF16)

    @pl.when(j >= na)
    def _():
        w = jnp.concatenate([w_ref[:, shift:], tail_ref[:, :shift]], axis=1)
        ob_ref[...] = w.astype(BF16)


def _split_w_in(w, tn):
    depth, d, n = w.shape
    na, nb = N_PROJ_A // tn, N_PROJ_B // tn
    per_tn = tn // N_SMALL
    w = w.astype(BF16).reshape(depth * d, n)
    return pl.pallas_call(
        functools.partial(_split_w_in_kernel, na=na),
        grid=(depth, na + nb),
        in_specs=[
            pl.BlockSpec((d, tn), lambda l, j: (l, j)),
            pl.BlockSpec((d, N_SMALL), lambda l, j: (l, (jnp.maximum(j, na - 1) + 1) * per_tn)),
        ],
        out_specs=[pl.BlockSpec((None, d, tn), lambda l, j: (l, 0, jnp.minimum(j, na - 1))),
                   pl.BlockSpec((None, d, tn), lambda l, j: (l, 0, jnp.maximum(j - na, 0))),
                   pl.BlockSpec((None, d, N_SMALL), lambda l, j: (l, 0, 0))],
        out_shape=[jax.ShapeDtypeStruct((depth, d, N_PROJ_A), BF16),
                   jax.ShapeDtypeStruct((depth, d, N_PROJ_B), BF16),
                   jax.ShapeDtypeStruct((depth, d, N_SMALL), BF16)],
        compiler_params=_cparams("parallel", "arbitrary"),
        name="split_w_in",
    )(w, w)


def _lane_row(vals, offset):
    row = jnp.zeros((1, LANES), F32)
    return row.at[0, offset:offset + vals.shape[0]].set(vals.astype(F32))


def kernel(x, attn_norm, w_in, conv_a_w, dn_conv_w, dn_a_log, dn_dt_bias, dn_norm, w_br_a, w_br_dn,
           w_br_moba, w_out, ffn_norm, w_gate_up, w_down, final_norm):
    batch, seq, d = x.shape
    depth = attn_norm.shape[0]
    assert d == D_MODEL and seq // MOBA_BLOCK <= LANES
    assert seq % (MOBA_QTILE_KEYS * MOBA_GROUP * MOBA_BLOCK) == 0
    t = batch * seq
    tm_huge = _largest_tile(t, 2048)
    assert seq % (2 * DN_TILE) == 0 and DN_TILE % DN_PAIR == 0 and (seq // DN_PAIR) % SUBLANES == 0

    inv = 1.0 / (ROPE_THETA ** (jnp.arange(0, HEAD_DIM, 2, dtype=F32) / HEAD_DIM))
    ang = jnp.arange(seq, dtype=F32)[:, None] * inv[None, :]
    cos_full = jnp.concatenate([jnp.cos(ang), jnp.cos(ang)], axis=-1)
    sin_signed = jnp.concatenate([-jnp.sin(ang), jnp.sin(ang)], axis=-1)
    block_onehot = (jnp.arange(seq)[:, None] // MOBA_BLOCK == jnp.arange(HEAD_DIM)[None, :]).astype(BF16)

    w_gate_up16, w_down16 = w_gate_up.astype(BF16), w_down.astype(BF16)
    w_br16 = [w.astype(BF16) for w in (w_br_a, w_br_dn, w_br_moba, w_out)]
    w_in16 = _split_w_in(w_in, 1024)

    xf = x.reshape(t, d)
    for l in range(depth):
        proj_a, proj_b, gates = _in_proj(xf, attn_norm[l][None, :], *w_in16, _lane_row(dn_a_log[l], HEADS),
                                         _lane_row(dn_dt_bias[l], HEADS), l, tm_huge, 1024)
        g_row = (gates[:, HEADS:2 * HEADS].reshape(batch, seq, HEADS).transpose(0, 2, 1)
                 .reshape(batch, HEADS, seq // DN_PAIR, DN_PAIR))
        y_dn = _deltanet(proj_a, gates, g_row, dn_conv_w[l], dn_norm[l][None, :], batch, seq)

        q_t, bias_t, k16, v_t = _moba_prep(proj_b, cos_full, sin_signed, batch, seq)
        y_mb = _moba_attn(q_t, bias_t, k16, block_onehot, v_t, batch, seq)

        x1 = _merge(proj_a, proj_b, conv_a_w[l], y_dn, y_mb, xf, *w_br16, l, seq, _largest_tile(t, 512))
        xf = _ffn(x1, ffn_norm[l][None, :], w_gate_up16, w_down16, l, final_norm[None, :],
                  l == depth - 1, tm_huge, 256)

    return xf.reshape(batch, seq, d)
```

```python
import functools

import jax
import jax.numpy as jnp
from jax import lax
from jax.experimental import pallas as pl
from jax.experimental.pallas import tpu as pltpu

F32 = jnp.float32
BF16 = jnp.bfloat16

D_MODEL = 1024
HEADS = 8
HEAD_DIM = 128
NORM_EPS = 1e-6
CONV_A_K = 3
DN_CONV_K = 4
DN_CHUNK = 64
DN_PAIR = 2 * DN_CHUNK
DN_TILE = 256
DN_CONV_GROUP = 128
MOBA_BLOCK = 256
MOBA_TOPK = 3
MOBA_GROUP = 2
MOBA_QTILE_KEYS = 2
LOG2_E = 1.4426950408889634
ROPE_THETA = 10000.0
FFN_HIDDEN = 2816

COL_AX, COL_AC, COL_AB = 0, 1, 2
COL_DQ, COL_DK, COL_DV, COL_DZ = 3, 4, 5, 6
N_PROJ_A = 7 * D_MODEL
COL_MQ, COL_MK, COL_MV = 0, 1, 2
COL_GATE = 3
N_PROJ_B = 6 * D_MODEL
N_SMALL = 128

MASK_NEG = -1e30
VMEM_LIMIT = 56 * 1024 * 1024
SUBLANES = 8
LANES = 128
HALO_ROWS = 16


def _cparams(*sem):
    return pltpu.CompilerParams(dimension_semantics=sem, vmem_limit_bytes=VMEM_LIMIT)


def _silu(x):
    return x * jax.nn.sigmoid(x)


def _dot(a, b):
    return jnp.dot(a, b, preferred_element_type=F32)


def _dot_nt(a, b):
    return lax.dot_general(a, b, (((1,), (1,)), ((), ())), preferred_element_type=F32)


def _dot_tn(a, b):
    return lax.dot_general(a, b, (((0,), (0,)), ((), ())), preferred_element_type=F32)


def _dn_gates(raw, alog_row, dt_row):
    tm = raw.shape[0]
    xa = raw + dt_row
    softplus = jnp.maximum(xa, 0.0) + jnp.log1p(jnp.exp(-jnp.abs(xa)))
    g = -jnp.exp(alog_row) * softplus
    row = lax.broadcasted_iota(jnp.int32, (tm, LANES), 0) % DN_CHUNK
    shift = 1
    while shift < DN_CHUNK:
        g = g + jnp.where(row >= shift, pltpu.roll(g, shift, axis=0), 0.0)
        shift *= 2
    lane = lax.broadcasted_iota(jnp.int32, (tm, LANES), 1)
    return jnp.where(lane < HEADS, jax.nn.sigmoid(raw), g)


def _in_proj_kernel(x_ref, g_ref, wa_ref, wb_ref, ws_ref, alog_ref, dt_ref, oa_ref, ob_ref, os_ref, h_ref,
                    *, na):
    j = pl.program_id(1)

    @pl.when(j == 0)
    def _():
        x = x_ref[...]
        ms = jnp.mean(x * x, axis=-1, keepdims=True)
        h_ref[...] = (x * lax.rsqrt(ms + NORM_EPS) * g_ref[...]).astype(BF16)
        os_ref[...] = _dn_gates(_dot(h_ref[...], ws_ref[...]), alog_ref[...], dt_ref[...])

    @pl.when(j < na)
    def _():
        oa_ref[...] = _dot(h_ref[...], wa_ref[...]).astype(BF16)

    @pl.when(j >= na)
    def _():
        ob_ref[...] = _dot(h_ref[...], wb_ref[...]).astype(BF16)


def _in_proj(x, gain, wa, wb, ws, alog_row, dt_row, layer, tm, tn):
    t, d = x.shape
    assert tm % DN_CHUNK == 0
    lane_row = pl.BlockSpec((1, LANES), lambda i, j: (0, 0))
    na, nb = wa.shape[2] // tn, wb.shape[2] // tn
    a_idx = lambda j: jnp.minimum(j, na - 1)
    b_idx = lambda j: jnp.maximum(j - na, 0)
    return pl.pallas_call(
        functools.partial(_in_proj_kernel, na=na),
        grid=(t // tm, na + nb),
        in_specs=[
            pl.BlockSpec((tm, d), lambda i, j: (i, 0)),
            pl.BlockSpec((1, d), lambda i, j: (0, 0)),
            pl.BlockSpec((None, d, tn), lambda i, j: (layer, 0, a_idx(j))),
            pl.BlockSpec((None, d, tn), lambda i, j: (layer, 0, b_idx(j))),
            pl.BlockSpec((None, d, N_SMALL), lambda i, j: (layer, 0, 0)),
            lane_row, lane_row,
        ],
        out_specs=[pl.BlockSpec((tm, tn), lambda i, j: (i, a_idx(j))),
                   pl.BlockSpec((tm, tn), lambda i, j: (i, b_idx(j))),
                   pl.BlockSpec((tm, N_SMALL), lambda i, j: (i, 0))],
        out_shape=[jax.ShapeDtypeStruct((t, wa.shape[2]), BF16),
                   jax.ShapeDtypeStruct((t, wb.shape[2]), BF16),
                   jax.ShapeDtypeStruct((t, N_SMALL), F32)],
        scratch_shapes=[pltpu.VMEM((tm, d), BF16)],
        compiler_params=_cparams("parallel", "arbitrary"),
        name="in_proj",
    )(x, gain, wa, wb, ws, alog_row, dt_row)


def _causal_conv(x, carry, w, k):
    xe = jnp.concatenate([carry, x], axis=0)
    y = x * w[k - 1:k]
    for j in range(1, k):
        y = y + pltpu.roll(xe, j, axis=0)[SUBLANES:] * w[k - 1 - j:k - j]
    return y


def _head_cols(h):
    return slice(h * HEAD_DIM, (h + 1) * HEAD_DIM)


def _interleave(*streams):
    order = [((k + 0.5) / len(s), si, step) for si, s in enumerate(streams) for k, step in enumerate(s)]
    for _, _, step in sorted(order, key=lambda e: e[:2]):
        step()


def _deltanet_kernel(q_ref, k_ref, v_ref, z_ref, qx_ref, kx_ref, vx_ref, gates_ref, grow_ref, cw_ref,
                     nw_ref, o_ref, carry_ref, state_ref, qkv_ref,
                     u_ref, w_ref, qd_ref, kd_ref, qk_ref):
    T = DN_TILE
    C = DN_CHUNK
    P = DN_PAIR
    hrange = range(HEADS)
    step_idx = pl.program_id(1)

    G = DN_CONV_GROUP
    sr = lax.broadcasted_iota(jnp.int32, ((DN_CONV_K - 1) * G, 2 * G), 0)
    sc = lax.broadcasted_iota(jnp.int32, ((DN_CONV_K - 1) * G, 2 * G), 1)
    shift_all = (sc == G + sr % G - (sr // G + 1)).astype(BF16)

    def pre_stream(srcs, r0, buf):
        steps = []
        for idx, src in enumerate(srcs):
            for rg in range(T // G):
                for hp in range(HEADS // 2):
                    def piece(idx=idx, src=src, rg=rg, hp=hp):
                        cols = slice(hp * 2 * HEAD_DIM, (hp + 1) * 2 * HEAD_DIM)
                        cur = src[r0 + rg * G:r0 + (rg + 1) * G, cols]
                        prev = carry_ref[idx, :, cols] if rg == 0 else src[r0 + (rg - 1) * G:r0 + rg * G, cols]
                        sh = _dot(shift_all, jnp.concatenate([prev, cur], axis=0))
                        w = cw_ref[:, idx * D_MODEL + hp * 2 * HEAD_DIM:idx * D_MODEL + (hp + 1) * 2 * HEAD_DIM]
                        y = cur.astype(F32) * w[DN_CONV_K - 1:DN_CONV_K]
                        for j in range(1, DN_CONV_K):
                            y = y + sh[(j - 1) * G:j * G] * w[DN_CONV_K - 1 - j:DN_CONV_K - j]
                        y = _silu(y)
                        for e in range(2):
                            ye = y[:, e * HEAD_DIM:(e + 1) * HEAD_DIM]
                            if idx < 2:
                                scale = HEAD_DIM ** -0.5 if idx == 0 else 1.0
                                ye = ye * (lax.rsqrt(jnp.sum(ye * ye, axis=-1, keepdims=True) + NORM_EPS) * scale)
                            qkv_ref[buf, idx, rg * G:(rg + 1) * G, _head_cols(2 * hp + e)] = ye
                    steps.append(piece)

            def save(idx=idx, src=src):
                carry_ref[idx] = src[r0 + T - G:r0 + T, :]
            steps.append(save)
        return steps

    @pl.when(step_idx == 0)
    def _():
        carry_ref[...] = jnp.zeros_like(carry_ref)
        state_ref[...] = jnp.zeros_like(state_ref)
        for step in pre_stream((q_ref, k_ref, v_ref), 0, 0):
            step()

    ri = lax.broadcasted_iota(jnp.int32, (P, P), 0)
    ci = lax.broadcasted_iota(jnp.int32, (P, P), 1)
    same_chunk = (ri // C) == (ci // C)
    causal = same_chunk & (ci <= ri)
    strict = same_chunk & (ci < ri)
    eye = (ci == ri).astype(F32)
    level_masks = []
    bs = 1
    while bs < C:
        same = (ri // (2 * bs)) == (ci // (2 * bs))
        level_masks.append(same & ((ri % (2 * bs)) >= bs) & ((ci % (2 * bs)) < bs))
        bs *= 2
    level_masks16 = [jnp.where(m, 1.0, 0.0).astype(BF16) for m in level_masks[1:]]
    first_half = lax.broadcasted_iota(jnp.int32, (P, 1), 0) < C
    pair0 = step_idx * (2 * T // P)
    nw = nw_ref[...]

    def a_stream(buf, r0):
        probs = [(sl, h) for sl in range(T // P) for h in hrange]
        n = range(len(probs))
        rows = [slice(sl * P, (sl + 1) * P) for sl in range(T // P)]
        ld = lambda idx, i: qkv_ref[buf, idx, rows[probs[i][0]], _head_cols(probs[i][1])]
        st = {}

        def s_decay():
            gbs = [gates_ref[r0 + sl * P:r0 + (sl + 1) * P, :] for sl in range(T // P)]
            st["beta"] = [gbs[sl][:, h:h + 1] for sl, h in probs]
            st["gc"] = [gbs[sl][:, HEADS + h:HEADS + h + 1] for sl, h in probs]
            gr = [grow_ref[0, h, pl.ds(pair0 + r0 // P + sl, 1), :] for sl, h in probs]
            st["decay"] = [jnp.where(causal, jnp.exp(jnp.where(causal, st["gc"][i] - gr[i], 0.0)), 0.0)
                           for i in n]

        def s_a():
            k = [ld(1, i) for i in n]
            st["kb"] = [k[i] * st["beta"][i] for i in n]
            st["k16"] = [k[i].astype(BF16) for i in n]
            st["a"] = [jnp.where(strict, _dot_nt(st["kb"][i].astype(BF16), st["k16"][i]) * st["decay"][i], 0.0)
                       for i in n]
            st["t"] = [eye - jnp.where(level_masks[0], st["a"][i], 0.0) for i in n]
            st["a16"] = [st["a"][i].astype(BF16) for i in n]

        def s_tx(m16):
            st["t16"] = [st["t"][i].astype(BF16) for i in n]
            st["tx"] = [_dot(st["t16"][i], st["a16"][i] * m16) for i in n]

        def s_t():
            st["t"] = [st["t"][i] - _dot(st["tx"][i].astype(BF16), st["t16"][i]) for i in n]

        def s_uw():
            st["eg"] = [jnp.exp(st["gc"][i]) for i in n]
            rhs = [jnp.concatenate([ld(2, i) * st["beta"][i], st["kb"][i] * st["eg"][i]], axis=1).astype(BF16)
                   for i in n]
            st["uw"] = [_dot(st["t"][i].astype(BF16), rhs[i]) for i in n]

        def s_qk():
            st["qk"] = [(_dot_nt(ld(0, i).astype(BF16), st["k16"][i]) * st["decay"][i]).astype(BF16) for i in n]

        def s_store():
            for i, (sl, h) in enumerate(probs):
                r, gc = rows[sl], st["gc"][i]
                glast = jnp.where(first_half, gc[C - 1:C, :], gc[P - 1:P, :])
                u_ref[buf, r, _head_cols(h)] = st["uw"][i][:, :HEAD_DIM]
                w_ref[buf, r, _head_cols(h)] = st["uw"][i][:, HEAD_DIM:].astype(BF16)
                qd_ref[buf, r, _head_cols(h)] = (ld(0, i) * st["eg"][i]).astype(BF16)
                kd_ref[buf, r, _head_cols(h)] = (ld(1, i) * jnp.exp(glast - gc)).astype(BF16)
                qk_ref[buf, sl * P:sl * P + C, h * C:(h + 1) * C] = st["qk"][i][:C, :C]
                qk_ref[buf, sl * P + C:(sl + 1) * P, h * C:(h + 1) * C] = st["qk"][i][C:, C:]

        steps = [s_decay, s_a]
        for m16 in level_masks16:
            steps += [functools.partial(s_tx, m16), s_t]
        return steps + [s_uw, s_qk, s_store]

    def b_stream(buf, r0):
        st = {}
        steps = []
        for c in range(T // C):
            rc = slice(c * C, (c + 1) * C)
            rg = slice(r0 + c * C, r0 + (c + 1) * C)

            def s_ws(rc=rc):
                st["s"] = [state_ref[h] for h in hrange]
                lhs = [jnp.concatenate([w_ref[buf, rc, _head_cols(h)], qd_ref[buf, rc, _head_cols(h)]], axis=0)
                       for h in hrange]
                st["ws"] = [_dot(lhs[h], st["s"][h].astype(BF16)) for h in hrange]

            def s_state(rc=rc, rg=rg):
                gb = gates_ref[rg, :]
                vn16 = [(u_ref[buf, rc, _head_cols(h)] - st["ws"][h][:C]).astype(BF16) for h in hrange]
                st["o"] = [st["ws"][h][C:] + _dot(qk_ref[buf, rc, h * C:(h + 1) * C], vn16[h]) for h in hrange]
                for h in hrange:
                    cd = jnp.exp(gb[C - 1:C, HEADS + h:HEADS + h + 1])
                    state_ref[h] = st["s"][h] * cd + _dot_tn(kd_ref[buf, rc, _head_cols(h)], vn16[h])

            def s_out(rg=rg):
                for h in hrange:
                    o = st["o"][h]
                    on = o * lax.rsqrt(jnp.mean(o * o, axis=-1, keepdims=True) + NORM_EPS) * nw
                    o_ref[rg, _head_cols(h)] = (on * _silu(z_ref[rg, _head_cols(h)].astype(F32))).astype(BF16)

            steps += [s_ws, s_state, s_out]
        return steps

    _interleave(a_stream(0, 0), pre_stream((q_ref, k_ref, v_ref), T, 1))
    _interleave(a_stream(1, T), b_stream(0, 0))
    _interleave(b_stream(1, T), pre_stream((qx_ref, kx_ref, vx_ref), 0, 0))


def _deltanet(proj, gates, g_row, conv_w, norm_w, batch, seq):
    t = proj.shape[0]
    T = DN_TILE
    L = 2 * T
    nl = seq // L
    blk = lambda col: pl.BlockSpec((L, D_MODEL), lambda b, s, col=col: (b * nl + s, col))
    nxt = lambda col: pl.BlockSpec(
        (T, D_MODEL), lambda b, s, col=col: (2 * b * nl + jnp.minimum(2 * s + 2, 2 * nl - 1), col))
    return pl.pallas_call(
        _deltanet_kernel,
        grid=(batch, nl),
        in_specs=[
            blk(COL_DQ), blk(COL_DK), blk(COL_DV), blk(COL_DZ),
            nxt(COL_DQ), nxt(COL_DK), nxt(COL_DV),
            pl.BlockSpec((L, LANES), lambda b, s: (b * nl + s, 0)),
            pl.BlockSpec((1, HEADS, seq // DN_PAIR, DN_PAIR), lambda b, s: (b, 0, 0, 0)),
            pl.BlockSpec((DN_CONV_K, 3 * D_MODEL), lambda b, s: (0, 0)),
            pl.BlockSpec((1, HEAD_DIM), lambda b, s: (0, 0)),
        ],
        out_specs=pl.BlockSpec((L, D_MODEL), lambda b, s: (b * nl + s, 0)),
        out_shape=jax.ShapeDtypeStruct((t, D_MODEL), BF16),
        scratch_shapes=[
            pltpu.VMEM((3, DN_CONV_GROUP, D_MODEL), BF16),
            pltpu.VMEM((HEADS, HEAD_DIM, HEAD_DIM), F32),
            pltpu.VMEM((2, 3, T, D_MODEL), F32),
            pltpu.VMEM((2, T, D_MODEL), F32),
            pltpu.VMEM((2, T, D_MODEL), BF16),
            pltpu.VMEM((2, T, D_MODEL), BF16),
            pltpu.VMEM((2, T, D_MODEL), BF16),
            pltpu.VMEM((2, T, HEADS * DN_CHUNK), BF16),
        ],
        compiler_params=_cparams("arbitrary", "arbitrary"),
        name="deltanet",
    )(proj, proj, proj, proj, proj, proj, proj, gates, g_row, conv_w, norm_w)


def _moba_prep_kernel(q_ref, k_ref, v_ref, cos_ref, sin_ref, qt_ref, bias_ref, k16_ref, vt_ref, kmean_ref):
    step = pl.program_id(1)
    BS = MOBA_BLOCK
    nbp = kmean_ref.shape[1]
    hrange = range(HEADS)

    @pl.when(step == 0)
    def _():
        kmean_ref[...] = jnp.zeros_like(kmean_ref)

    def split(x):
        hi = x.astype(BF16)
        return hi, (x - hi.astype(F32)).astype(BF16)

    def one_block(sub):
        i = step * MOBA_GROUP + sub
        rows = slice(sub * BS, (sub + 1) * BS)
        cos = cos_ref[rows, :]
        sin = sin_ref[rows, :]
        rope = lambda x: x * cos + pltpu.roll(x, HEAD_DIM // 2, axis=1) * sin
        q = [rope(q_ref[rows, _head_cols(h)].astype(F32)) for h in hrange]
        k = [rope(k_ref[rows, _head_cols(h)].astype(F32)) for h in hrange]

        qs = [split(q[h]) for h in hrange]
        ms = [split(kmean_ref[h]) for h in hrange]
        gate = [_dot_nt(ms[h][0], qs[h][0]) + (_dot_nt(ms[h][0], qs[h][1]) + _dot_nt(ms[h][1], qs[h][0]))
                for h in hrange]
        blk = lax.broadcasted_iota(jnp.int32, (nbp, BS), 0)
        blk_f = blk.astype(F32)
        neg_inf = jnp.float32(-jnp.inf)
        g = [jnp.where(blk < i, gate[h], neg_inf) for h in hrange]
        sel = [blk == i for h in hrange]
        for _ in range(MOBA_TOPK):
            m = [jnp.max(g[h], axis=0, keepdims=True) for h in hrange]
            first = [jnp.min(jnp.where((g[h] == m[h]) & (g[h] > neg_inf), blk_f, float(nbp)),
                             axis=0, keepdims=True) for h in hrange]
            pick = [blk_f == first[h] for h in hrange]
            sel = [sel[h] | pick[h] for h in hrange]
            g = [jnp.where(pick[h], neg_inf, g[h]) for h in hrange]

        qt = [(q[h] * (HEAD_DIM ** -0.5 * LOG2_E)).T.astype(BF16) for h in hrange]
        vt = [v_ref[rows, _head_cols(h)].astype(F32).T.astype(BF16) for h in hrange]
        for h in hrange:
            qt_ref[0, h, :, rows] = qt[h]
            bias_ref[0, h, :, rows] = jnp.where(sel[h], 0.0, MASK_NEG).astype(BF16)
            k16_ref[0, h, rows, :] = k[h].astype(BF16)
            vt_ref[0, h, 0, :, rows] = vt[h]
            kmean_ref[h, pl.ds(i, 1), :] = jnp.mean(k[h], axis=0, keepdims=True)

    for sub in range(MOBA_GROUP):
        one_block(sub)


def _moba_prep(proj, cos_full, sin_signed, batch, seq):
    nb = seq // MOBA_BLOCK
    nbp = -(-nb // 16) * 16
    rows = MOBA_GROUP * MOBA_BLOCK
    ns = nb // MOBA_GROUP
    blk = lambda col: pl.BlockSpec((rows, D_MODEL), lambda b, i, col=col: (b * ns + i, col))
    tab = pl.BlockSpec((rows, HEAD_DIM), lambda b, i: (i, 0))
    return pl.pallas_call(
        _moba_prep_kernel,
        grid=(batch, ns),
        in_specs=[blk(COL_MQ), blk(COL_MK), blk(COL_MV), tab, tab],
        out_specs=[pl.BlockSpec((1, HEADS, HEAD_DIM, rows), lambda b, i: (b, 0, 0, i)),
                   pl.BlockSpec((1, HEADS, nbp, rows), lambda b, i: (b, 0, 0, i)),
                   pl.BlockSpec((1, HEADS, rows, HEAD_DIM), lambda b, i: (b, 0, i, 0)),
                   pl.BlockSpec((1, HEADS, 1, HEAD_DIM, rows), lambda b, i: (b, 0, i, 0, 0))],
        out_shape=[jax.ShapeDtypeStruct((batch, HEADS, HEAD_DIM, seq), BF16),
                   jax.ShapeDtypeStruct((batch, HEADS, nbp, seq), BF16),
                   jax.ShapeDtypeStruct((batch, HEADS, seq, HEAD_DIM), BF16),
                   jax.ShapeDtypeStruct((batch, HEADS, nb // MOBA_GROUP, HEAD_DIM,
                                         MOBA_GROUP * MOBA_BLOCK), BF16)],
        scratch_shapes=[pltpu.VMEM((HEADS, nbp, HEAD_DIM), F32)],
        compiler_params=_cparams("parallel", "arbitrary"),
        name="moba_prep",
    )(proj, proj, proj, cos_full, sin_signed)


def _moba_attn_kernel(q_ref, bias_ref, k_ref, oh_ref, vt_ref, o_ref, sa_ref, sb_ref):
    tq_idx = pl.program_id(2)
    TK = MOBA_GROUP * MOBA_BLOCK
    TQ = MOBA_QTILE_KEYS * TK
    nbp = bias_ref.shape[2]
    qt = jnp.concatenate([q_ref[0, 0], bias_ref[0, 0], jnp.zeros((HEAD_DIM - nbp, TQ), BF16)], axis=0)

    def k_aug(j):
        rows = pl.ds(pl.multiple_of(j * TK, TK), TK)
        return jnp.concatenate([k_ref[0, 0, rows, :], oh_ref[rows, :]], axis=1)

    def scores(j):
        return _dot(k_aug(j), qt)

    def attend(s, j, m, l, acc):
        m_new = jnp.maximum(m, jnp.max(s, axis=0, keepdims=True))
        alpha = jnp.exp2(m - m_new)
        p = jnp.exp2(s - m_new)
        l = alpha * l + jnp.sum(p, axis=0, keepdims=True)
        acc = alpha * acc + _dot(vt_ref[0, 0, j], p.astype(BF16))
        return m_new, l, acc

    def step(g, cur_ref, nxt_ref, carry):
        nxt_ref[...] = scores(g + 1)
        return attend(cur_ref[...], g, *carry)

    assert MOBA_QTILE_KEYS == 2
    sa_ref[...] = scores(0)
    carry = (jnp.full((1, TQ), MASK_NEG, F32), jnp.zeros((1, TQ), F32), jnp.zeros((HEAD_DIM, TQ), F32))

    def two_steps(h, carry):
        carry = step(2 * h, sa_ref, sb_ref, carry)
        return step(2 * h + 1, sb_ref, sa_ref, carry)

    carry = lax.fori_loop(0, tq_idx // 2, lambda h, c: two_steps(2 * h + 1, two_steps(2 * h, c)), carry)
    carry = lax.fori_loop(tq_idx // 2 * 2, tq_idx, two_steps, carry)

    own0, own1 = 2 * tq_idx, 2 * tq_idx + 1
    half = slice(TK, TQ)
    tri = (lax.broadcasted_iota(jnp.int32, (TK, TK), 0) <= lax.broadcasted_iota(jnp.int32, (TK, TK), 1))
    sb_ref[:, half] = _dot(k_aug(own1), qt[:, half])
    s0 = sa_ref[...]
    s0 = jnp.concatenate([jnp.where(tri, s0[:, :TK], MASK_NEG), s0[:, half]], axis=1)
    m, l, acc = attend(s0, own0, *carry)
    m2, l2, acc2 = attend(jnp.where(tri, sb_ref[:, half], MASK_NEG), own1, m[:, half], l[:, half], acc[:, half])
    l = jnp.concatenate([l[:, :TK], l2], axis=1)
    acc = jnp.concatenate([acc[:, :TK], acc2], axis=1)
    o_ref[...] = (acc / l).T.astype(BF16)


def _moba_attn(q_t, bias_t, k16, block_onehot, v_t, batch, seq):
    tk = MOBA_GROUP * MOBA_BLOCK
    tq = MOBA_QTILE_KEYS * tk
    nbp = bias_t.shape[2]
    return pl.pallas_call(
        _moba_attn_kernel,
        grid=(batch, HEADS, seq // tq),
        in_specs=[
            pl.BlockSpec((1, 1, HEAD_DIM, tq), lambda b, h, i: (b, h, 0, i)),
            pl.BlockSpec((1, 1, nbp, tq), lambda b, h, i: (b, h, 0, i)),
            pl.BlockSpec((1, 1, seq, HEAD_DIM), lambda b, h, i: (b, h, 0, 0)),
            pl.BlockSpec((seq, HEAD_DIM), lambda b, h, i: (0, 0)),
            pl.BlockSpec((1, 1, seq // tk, HEAD_DIM, tk), lambda b, h, i: (b, h, 0, 0, 0)),
        ],
        out_specs=pl.BlockSpec((tq, HEAD_DIM), lambda b, h, i: (b * (seq // tq) + i, h)),
        out_shape=jax.ShapeDtypeStruct((batch * seq, D_MODEL), BF16),
        scratch_shapes=[pltpu.VMEM((tk, tq), F32), pltpu.VMEM((tk, tq), F32)],
        compiler_params=_cparams("parallel", "parallel", "arbitrary"),
        name="moba_attn",
    )(q_t, bias_t, k16, block_onehot, v_t)


def _merge_kernel(ax_ref, ac_ref, ab_ref, hx_ref, hc_ref, cw_ref, ydn_ref, ymb_ref,
                  ga_ref, gd_ref, gm_ref, x_ref, wa_ref, wd_ref, wm_ref, wo_ref, o_ref, *, seq):
    tm = ax_ref.shape[0]
    f32 = lambda ref: ref[...].astype(F32)
    first = (pl.program_id(0) * tm) % seq == 0
    halo = jnp.where(first, 0.0, (f32(hx_ref) * f32(hc_ref))[HALO_ROWS - SUBLANES:])
    p = f32(ax_ref) * f32(ac_ref)
    y_a = f32(ab_ref) * _causal_conv(p, halo, cw_ref[...], CONV_A_K)
    merged = (jax.nn.sigmoid(f32(ga_ref)) * _dot(y_a.astype(BF16), wa_ref[...])
              + jax.nn.sigmoid(f32(gd_ref)) * _dot(ydn_ref[...], wd_ref[...])
              + jax.nn.sigmoid(f32(gm_ref)) * _dot(ymb_ref[...], wm_ref[...]))
    o_ref[...] = x_ref[...] + _dot(merged.astype(BF16), wo_ref[...])


def _merge(proj_a, proj_b, conv_w, y_dn, y_mb, x, wa, wd, wm, wo, layer, seq, tm):
    t = x.shape[0]
    blk = lambda col: pl.BlockSpec((tm, D_MODEL), lambda i, col=col: (i, col))
    halo = lambda col: pl.BlockSpec(
        (HALO_ROWS, D_MODEL), lambda i, col=col: (jnp.maximum(i * (tm // HALO_ROWS) - 1, 0), col))
    row = pl.BlockSpec((tm, D_MODEL), lambda i: (i, 0))
    wspec = pl.BlockSpec((None, D_MODEL, D_MODEL), lambda i: (layer, 0, 0))
    return pl.pallas_call(
        functools.partial(_merge_kernel, seq=seq),
        grid=(t // tm,),
        in_specs=[blk(COL_AX), blk(COL_AC), blk(COL_AB), halo(COL_AX), halo(COL_AC),
                  pl.BlockSpec((CONV_A_K, D_MODEL), lambda i: (0, 0)),
                  row, row, blk(COL_GATE), blk(COL_GATE + 1), blk(COL_GATE + 2), row,
                  wspec, wspec, wspec, wspec],
        out_specs=row,
        out_shape=jax.ShapeDtypeStruct((t, D_MODEL), F32),
        compiler_params=_cparams("parallel"),
        name="merge",
    )(proj_a, proj_a, proj_a, proj_a, proj_a, conv_w, y_dn, y_mb, proj_b, proj_b, proj_b, x,
      wa, wd, wm, wo)


def _rms_scale(x, gain):
    return x * lax.rsqrt(jnp.mean(x * x, axis=-1, keepdims=True) + NORM_EPS) * gain


def _ffn_kernel(x_ref, g_ref, wg_ref, wu_ref, wd_ref, fg_ref, o_ref, h_ref, acc_ref, *, final_norm):
    j = pl.program_id(1)

    @pl.when(j == 0)
    def _():
        x = x_ref[...]
        h_ref[...] = _rms_scale(x, g_ref[...]).astype(BF16)
        acc_ref[...] = x

    h = h_ref[...]
    act = _silu(_dot(h, wg_ref[...])) * _dot(h, wu_ref[...])
    acc_ref[...] += _dot(act.astype(BF16), wd_ref[...])

    @pl.when(j == pl.num_programs(1) - 1)
    def _():
        y = acc_ref[...]
        o_ref[...] = _rms_scale(y, fg_ref[...]) if final_norm else y


def _ffn(x, gain, w_gate_up, w_down, layer, final_gain, final_norm, tm, th):
    t, d = x.shape
    hid = w_down.shape[1]
    nh = hid // th
    return pl.pallas_call(
        functools.partial(_ffn_kernel, final_norm=final_norm),
        grid=(t // tm, nh),
        in_specs=[
            pl.BlockSpec((tm, d), lambda i, j: (i, 0)),
            pl.BlockSpec((1, d), lambda i, j: (0, 0)),
            pl.BlockSpec((None, d, th), lambda i, j: (layer, 0, j)),
            pl.BlockSpec((None, d, th), lambda i, j: (layer, 0, j + nh)),
            pl.BlockSpec((None, th, d), lambda i, j: (layer, j, 0)),
            pl.BlockSpec((1, d), lambda i, j: (0, 0)),
        ],
        out_specs=pl.BlockSpec((tm, d), lambda i, j: (i, 0)),
        out_shape=jax.ShapeDtypeStruct((t, d), F32),
        scratch_shapes=[pltpu.VMEM((tm, d), BF16), pltpu.VMEM((tm, d), F32)],
        compiler_params=_cparams("parallel", "arbitrary"),
        name="ffn",
    )(x, gain, w_gate_up, w_gate_up, w_down, final_gain)


def _largest_tile(n, cap):
    t = cap
    while n % t:
        t //= 2
    return t


def _split_w_in_kernel(w_ref, tail_ref, oa_ref, ob_ref, os_ref, *, na):
    j = pl.program_id(1)
    shift = 2 * HEADS

    @pl.when(j == 0)
    def _():
        lane = lax.broadcasted_iota(jnp.int32, tail_ref.shape, 1)
        os_ref[...] = jnp.where(lane < shift, tail_ref[...], 0.0).astype(BF16)

    @pl.when(j < na)
    def _():
        oa_ref[...] = w_ref[...].astype(BF16)

    @pl.when(j >= na)
    def _():
        w = jnp.concatenate([w_ref[:, shift:], tail_ref[:, :shift]], axis=1)
        ob_ref[...] = w.astype(BF16)


def _split_w_in(w, tn):
    depth, d, n = w.shape
    na, nb = N_PROJ_A // tn, N_PROJ_B // tn
    per_tn = tn // N_SMALL
    w = w.astype(BF16).reshape(depth * d, n)
    return pl.pallas_call(
        functools.partial(_split_w_in_kernel, na=na),
        grid=(depth, na + nb),
        in_specs=[
            pl.BlockSpec((d, tn), lambda l, j: (l, j)),
            pl.BlockSpec((d, N_SMALL), lambda l, j: (l, (jnp.maximum(j, na - 1) + 1) * per_tn)),
        ],
        out_specs=[pl.BlockSpec((None, d, tn), lambda l, j: (l, 0, jnp.minimum(j, na - 1))),
                   pl.BlockSpec((None, d, tn), lambda l, j: (l, 0, jnp.maximum(j - na, 0))),
                   pl.BlockSpec((None, d, N_SMALL), lambda l, j: (l, 0, 0))],
        out_shape=[jax.ShapeDtypeStruct((depth, d, N_PROJ_A), BF16),
                   jax.ShapeDtypeStruct((depth, d, N_PROJ_B), BF16),
                   jax.ShapeDtypeStruct((depth, d, N_SMALL), BF16)],
        compiler_params=_cparams("parallel", "arbitrary"),
        name="split_w_in",
    )(w, w)


def _lane_row(vals, offset):
    row = jnp.zeros((1, LANES), F32)
    return row.at[0, offset:offset + vals.shape[0]].set(vals.astype(F32))


def kernel(x, attn_norm, w_in, conv_a_w, dn_conv_w, dn_a_log, dn_dt_bias, dn_norm, w_br_a, w_br_dn,
           w_br_moba, w_out, ffn_norm, w_gate_up, w_down, final_norm):
    batch, seq, d = x.shape
    depth = attn_norm.shape[0]
    assert d == D_MODEL and seq // MOBA_BLOCK <= LANES
    assert seq % (MOBA_QTILE_KEYS * MOBA_GROUP * MOBA_BLOCK) == 0
    t = batch * seq
    tm_huge = _largest_tile(t, 2048)
    assert seq % (2 * DN_TILE) == 0 and DN_TILE % DN_PAIR == 0 and (seq // DN_PAIR) % SUBLANES == 0

    inv = 1.0 / (ROPE_THETA ** (jnp.arange(0, HEAD_DIM, 2, dtype=F32) / HEAD_DIM))
    ang = jnp.arange(seq, dtype=F32)[:, None] * inv[None, :]
    cos_full = jnp.concatenate([jnp.cos(ang), jnp.cos(ang)], axis=-1)
    sin_signed = jnp.concatenate([-jnp.sin(ang), jnp.sin(ang)], axis=-1)
    block_onehot = (jnp.arange(seq)[:, None] // MOBA_BLOCK == jnp.arange(HEAD_DIM)[None, :]).astype(BF16)

    w_gate_up16, w_down16 = w_gate_up.astype(BF16), w_down.astype(BF16)
    w_br16 = [w.astype(BF16) for w in (w_br_a, w_br_dn, w_br_moba, w_out)]
    w_in16 = _split_w_in(w_in, 1024)

    xf = x.reshape(t, d)
    for l in range(depth):
        proj_a, proj_b, gates = _in_proj(xf, attn_norm[l][None, :], *w_in16, _lane_row(dn_a_log[l], HEADS),
                                         _lane_row(dn_dt_bias[l], HEADS), l, tm_huge, 1024)
        g_row = (gates[:, HEADS:2 * HEADS].reshape(batch, seq, HEADS).transpose(0, 2, 1)
                 .reshape(batch, HEADS, seq // DN_PAIR, DN_PAIR))
        y_dn = _deltanet(proj_a, gates, g_row, dn_conv_w[l], dn_norm[l][None, :], batch, seq)

        q_t, bias_t, k16, v_t = _moba_prep(proj_b, cos_full, sin_signed, batch, seq)
        y_mb = _moba_attn(q_t, bias_t, k16, block_onehot, v_t, batch, seq)

        x1 = _merge(proj_a, proj_b, conv_a_w[l], y_dn, y_mb, xf, *w_br16, l, seq, _largest_tile(t, 512))
        xf = _ffn(x1, ffn_norm[l][None, :], w_gate_up16, w_down16, l, final_norm[None, :],
                  l == depth - 1, tm_huge, 256)

    return xf.reshape(batch, seq, d)
```

```python
import functools

import jax
import jax.numpy as jnp
from jax import lax
from jax.experimental import pallas as pl
from jax.experimental.pallas import tpu as pltpu

F32 = jnp.float32
BF16 = jnp.bfloat16

D_MODEL = 1024
HEADS = 8
HEAD_DIM = 128
NORM_EPS = 1e-6
CONV_A_K = 3
DN_CONV_K = 4
DN_CHUNK = 64
DN_PAIR = 2 * DN_CHUNK
DN_TILE = 256
DN_CONV_GROUP = 128
MOBA_BLOCK = 256
MOBA_TOPK = 3
MOBA_GROUP = 2
MOBA_PREP_BLOCKS = 4
MOBA_QTILE_KEYS = 2
LOG2_E = 1.4426950408889634
ROPE_THETA = 10000.0
FFN_HIDDEN = 2816

COL_AX, COL_AC, COL_AB = 0, 1, 2
COL_DQ, COL_DK, COL_DV, COL_DZ = 3, 4, 5, 6
N_PROJ_A = 7 * D_MODEL
COL_MQ, COL_MK, COL_MV = 0, 1, 2
COL_GATE = 3
N_PROJ_B = 6 * D_MODEL
N_SMALL = 128

MASK_NEG = -1e30
VMEM_LIMIT = 56 * 1024 * 1024
SUBLANES = 8
LANES = 128
HALO_ROWS = 16


def _cparams(*sem):
    return pltpu.CompilerParams(dimension_semantics=sem, vmem_limit_bytes=VMEM_LIMIT)


def _silu(x):
    return x * jax.nn.sigmoid(x)


def _dot(a, b):
    return jnp.dot(a, b, preferred_element_type=F32)


def _dot_nt(a, b):
    return lax.dot_general(a, b, (((1,), (1,)), ((), ())), preferred_element_type=F32)


def _dot_tn(a, b):
    return lax.dot_general(a, b, (((0,), (0,)), ((), ())), preferred_element_type=F32)


def _dn_gates(raw, alog_row, dt_row):
    tm = raw.shape[0]
    xa = raw + dt_row
    softplus = jnp.maximum(xa, 0.0) + jnp.log1p(jnp.exp(-jnp.abs(xa)))
    g = -jnp.exp(alog_row) * softplus
    row = lax.broadcasted_iota(jnp.int32, (tm, LANES), 0) % DN_CHUNK
    shift = 1
    while shift < DN_CHUNK:
        g = g + jnp.where(row >= shift, pltpu.roll(g, shift, axis=0), 0.0)
        shift *= 2
    lane = lax.broadcasted_iota(jnp.int32, (tm, LANES), 1)
    return jnp.where(lane < HEADS, jax.nn.sigmoid(raw), g)


def _in_proj_kernel(x_ref, g_ref, wa_ref, wb_ref, ws_ref, alog_ref, dt_ref, oa_ref, ob_ref, os_ref, h_ref,
                    *, na):
    j = pl.program_id(1)

    @pl.when(j == 0)
    def _():
        x = x_ref[...]
        ms = jnp.mean(x * x, axis=-1, keepdims=True)
        h_ref[...] = (x * lax.rsqrt(ms + NORM_EPS) * g_ref[...]).astype(BF16)
        os_ref[...] = _dn_gates(_dot(h_ref[...], ws_ref[...]), alog_ref[...], dt_ref[...])

    @pl.when(j < na)
    def _():
        oa_ref[...] = _dot(h_ref[...], wa_ref[...]).astype(BF16)

    @pl.when(j >= na)
    def _():
        ob_ref[...] = _dot(h_ref[...], wb_ref[...]).astype(BF16)


def _in_proj(x, gain, wa, wb, ws, alog_row, dt_row, layer, tm, tn):
    t, d = x.shape
    assert tm % DN_CHUNK == 0
    lane_row = pl.BlockSpec((1, LANES), lambda i, j: (0, 0))
    na, nb = wa.shape[2] // tn, wb.shape[2] // tn
    a_idx = lambda j: jnp.minimum(j, na - 1)
    b_idx = lambda j: jnp.maximum(j - na, 0)
    return pl.pallas_call(
        functools.partial(_in_proj_kernel, na=na),
        grid=(t // tm, na + nb),
        in_specs=[
            pl.BlockSpec((tm, d), lambda i, j: (i, 0)),
            pl.BlockSpec((1, d), lambda i, j: (0, 0)),
            pl.BlockSpec((None, d, tn), lambda i, j: (layer, 0, a_idx(j))),
            pl.BlockSpec((None, d, tn), lambda i, j: (layer, 0, b_idx(j))),
            pl.BlockSpec((None, d, N_SMALL), lambda i, j: (layer, 0, 0)),
            lane_row, lane_row,
        ],
        out_specs=[pl.BlockSpec((tm, tn), lambda i, j: (i, a_idx(j))),
                   pl.BlockSpec((tm, tn), lambda i, j: (i, b_idx(j))),
                   pl.BlockSpec((tm, N_SMALL), lambda i, j: (i, 0))],
        out_shape=[jax.ShapeDtypeStruct((t, wa.shape[2]), BF16),
                   jax.ShapeDtypeStruct((t, wb.shape[2]), BF16),
                   jax.ShapeDtypeStruct((t, N_SMALL), F32)],
        scratch_shapes=[pltpu.VMEM((tm, d), BF16)],
        compiler_params=_cparams("parallel", "arbitrary"),
        name="in_proj",
    )(x, gain, wa, wb, ws, alog_row, dt_row)


def _causal_conv(x, carry, w, k):
    xe = jnp.concatenate([carry, x], axis=0)
    y = x * w[k - 1:k]
    for j in range(1, k):
        y = y + pltpu.roll(xe, j, axis=0)[SUBLANES:] * w[k - 1 - j:k - j]
    return y


def _head_cols(h):
    return slice(h * HEAD_DIM, (h + 1) * HEAD_DIM)


def _interleave(*streams):
    order = [((k + 0.5) / len(s), si, step) for si, s in enumerate(streams) for k, step in enumerate(s)]
    for _, _, step in sorted(order, key=lambda e: e[:2]):
        step()


def _deltanet_kernel(q_ref, k_ref, v_ref, z_ref, qx_ref, kx_ref, vx_ref, gates_ref, grow_ref, cw_ref,
                     nw_ref, o_ref, carry_ref, state_ref, qkv_ref,
                     u_ref, w_ref, qd_ref, kd_ref, qk_ref):
    T = DN_TILE
    C = DN_CHUNK
    P = DN_PAIR
    hrange = range(HEADS)
    step_idx = pl.program_id(1)

    G = DN_CONV_GROUP
    sr = lax.broadcasted_iota(jnp.int32, ((DN_CONV_K - 1) * G, 2 * G), 0)
    sc = lax.broadcasted_iota(jnp.int32, ((DN_CONV_K - 1) * G, 2 * G), 1)
    shift_all = (sc == G + sr % G - (sr // G + 1)).astype(BF16)

    def pre_stream(srcs, r0, buf):
        steps = []
        for idx, src in enumerate(srcs):
            for rg in range(T // G):
                for hp in range(HEADS // 2):
                    def piece(idx=idx, src=src, rg=rg, hp=hp):
                        cols = slice(hp * 2 * HEAD_DIM, (hp + 1) * 2 * HEAD_DIM)
                        cur = src[r0 + rg * G:r0 + (rg + 1) * G, cols]
                        prev = carry_ref[idx, :, cols] if rg == 0 else src[r0 + (rg - 1) * G:r0 + rg * G, cols]
                        sh = _dot(shift_all, jnp.concatenate([prev, cur], axis=0))
                        w = cw_ref[:, idx * D_MODEL + hp * 2 * HEAD_DIM:idx * D_MODEL + (hp + 1) * 2 * HEAD_DIM]
                        y = cur.astype(F32) * w[DN_CONV_K - 1:DN_CONV_K]
                        for j in range(1, DN_CONV_K):
                            y = y + sh[(j - 1) * G:j * G] * w[DN_CONV_K - 1 - j:DN_CONV_K - j]
                        y = _silu(y)
                        for e in range(2):
                            ye = y[:, e * HEAD_DIM:(e + 1) * HEAD_DIM]
                            if idx < 2:
                                scale = HEAD_DIM ** -0.5 if idx == 0 else 1.0
                                ye = ye * (lax.rsqrt(jnp.sum(ye * ye, axis=-1, keepdims=True) + NORM_EPS) * scale)
                            qkv_ref[buf, idx, rg * G:(rg + 1) * G, _head_cols(2 * hp + e)] = ye
                    steps.append(piece)

            def save(idx=idx, src=src):
                carry_ref[idx] = src[r0 + T - G:r0 + T, :]
            steps.append(save)
        return steps

    @pl.when(step_idx == 0)
    def _():
        carry_ref[...] = jnp.zeros_like(carry_ref)
        state_ref[...] = jnp.zeros_like(state_ref)
        for step in pre_stream((q_ref, k_ref, v_ref), 0, 0):
            step()

    ri = lax.broadcasted_iota(jnp.int32, (P, P), 0)
    ci = lax.broadcasted_iota(jnp.int32, (P, P), 1)
    same_chunk = (ri // C) == (ci // C)
    causal = same_chunk & (ci <= ri)
    strict = same_chunk & (ci < ri)
    eye = (ci == ri).astype(F32)
    level_masks = []
    bs = 1
    while bs < C:
        same = (ri // (2 * bs)) == (ci // (2 * bs))
        level_masks.append(same & ((ri % (2 * bs)) >= bs) & ((ci % (2 * bs)) < bs))
        bs *= 2
    level_masks16 = [jnp.where(m, 1.0, 0.0).astype(BF16) for m in level_masks[1:]]
    first_half = lax.broadcasted_iota(jnp.int32, (P, 1), 0) < C
    pair0 = step_idx * (2 * T // P)
    nw = nw_ref[...]

    def a_stream(buf, r0):
        probs = [(sl, h) for sl in range(T // P) for h in hrange]
        n = range(len(probs))
        rows = [slice(sl * P, (sl + 1) * P) for sl in range(T // P)]
        ld = lambda idx, i: qkv_ref[buf, idx, rows[probs[i][0]], _head_cols(probs[i][1])]
        st = {}

        def s_decay():
            gbs = [gates_ref[r0 + sl * P:r0 + (sl + 1) * P, :] for sl in range(T // P)]
            st["beta"] = [gbs[sl][:, h:h + 1] for sl, h in probs]
            st["gc"] = [gbs[sl][:, HEADS + h:HEADS + h + 1] for sl, h in probs]
            gr = [grow_ref[0, h, pl.ds(pair0 + r0 // P + sl, 1), :] for sl, h in probs]
            st["decay"] = [jnp.where(causal, jnp.exp(jnp.where(causal, st["gc"][i] - gr[i], 0.0)), 0.0)
                           for i in n]

        def s_a():
            k = [ld(1, i) for i in n]
            st["kb"] = [k[i] * st["beta"][i] for i in n]
            st["k16"] = [k[i].astype(BF16) for i in n]
            st["a"] = [jnp.where(strict, _dot_nt(st["kb"][i].astype(BF16), st["k16"][i]) * st["decay"][i], 0.0)
                       for i in n]
            st["t"] = [eye - jnp.where(level_masks[0], st["a"][i], 0.0) for i in n]
            st["a16"] = [st["a"][i].astype(BF16) for i in n]

        def s_tx(m16):
            st["t16"] = [st["t"][i].astype(BF16) for i in n]
            st["tx"] = [_dot(st["t16"][i], st["a16"][i] * m16) for i in n]

        def s_t():
            st["t"] = [st["t"][i] - _dot(st["tx"][i].astype(BF16), st["t16"][i]) for i in n]

        def s_uw():
            st["eg"] = [jnp.exp(st["gc"][i]) for i in n]
            rhs = [jnp.concatenate([ld(2, i) * st["beta"][i], st["kb"][i] * st["eg"][i]], axis=1).astype(BF16)
                   for i in n]
            st["uw"] = [_dot(st["t"][i].astype(BF16), rhs[i]) for i in n]

        def s_qk():
            st["qk"] = [(_dot_nt(ld(0, i).astype(BF16), st["k16"][i]) * st["decay"][i]).astype(BF16) for i in n]

        def s_store():
            for i, (sl, h) in enumerate(probs):
                r, gc = rows[sl], st["gc"][i]
                glast = jnp.where(first_half, gc[C - 1:C, :], gc[P - 1:P, :])
                u_ref[buf, r, _head_cols(h)] = st["uw"][i][:, :HEAD_DIM]
                w_ref[buf, r, _head_cols(h)] = st["uw"][i][:, HEAD_DIM:].astype(BF16)
                qd_ref[buf, r, _head_cols(h)] = (ld(0, i) * st["eg"][i]).astype(BF16)
                kd_ref[buf, r, _head_cols(h)] = (ld(1, i) * jnp.exp(glast - gc)).astype(BF16)
                qk_ref[buf, sl * P:sl * P + C, h * C:(h + 1) * C] = st["qk"][i][:C, :C]
                qk_ref[buf, sl * P + C:(sl + 1) * P, h * C:(h + 1) * C] = st["qk"][i][C:, C:]

        steps = [s_decay, s_a]
        for m16 in level_masks16:
            steps += [functools.partial(s_tx, m16), s_t]
        return steps + [s_uw, s_qk, s_store]

    def b_stream(buf, r0):
        st = {}
        steps = []
        for c in range(T // C):
            rc = slice(c * C, (c + 1) * C)
            rg = slice(r0 + c * C, r0 + (c + 1) * C)

            def s_ws(rc=rc):
                st["s"] = [state_ref[h] for h in hrange]
                lhs = [jnp.concatenate([w_ref[buf, rc, _head_cols(h)], qd_ref[buf, rc, _head_cols(h)]], axis=0)
                       for h in hrange]
                st["ws"] = [_dot(lhs[h], st["s"][h].astype(BF16)) for h in hrange]

            def s_state(rc=rc, rg=rg):
                gb = gates_ref[rg, :]
                vn16 = [(u_ref[buf, rc, _head_cols(h)] - st["ws"][h][:C]).astype(BF16) for h in hrange]
                st["o"] = [st["ws"][h][C:] + _dot(qk_ref[buf, rc, h * C:(h + 1) * C], vn16[h]) for h in hrange]
                for h in hrange:
                    cd = jnp.exp(gb[C - 1:C, HEADS + h:HEADS + h + 1])
                    state_ref[h] = st["s"][h] * cd + _dot_tn(kd_ref[buf, rc, _head_cols(h)], vn16[h])

            def s_out(rg=rg):
                for h in hrange:
                    o = st["o"][h]
                    on = o * lax.rsqrt(jnp.mean(o * o, axis=-1, keepdims=True) + NORM_EPS) * nw
                    o_ref[rg, _head_cols(h)] = (on * _silu(z_ref[rg, _head_cols(h)].astype(F32))).astype(BF16)

            steps += [s_ws, s_state, s_out]
        return steps

    _interleave(a_stream(0, 0), pre_stream((q_ref, k_ref, v_ref), T, 1))
    _interleave(a_stream(1, T), b_stream(0, 0))
    _interleave(b_stream(1, T), pre_stream((qx_ref, kx_ref, vx_ref), 0, 0))


def _deltanet(proj, gates, g_row, conv_w, norm_w, batch, seq):
    t = proj.shape[0]
    T = DN_TILE
    L = 2 * T
    nl = seq // L
    blk = lambda col: pl.BlockSpec((L, D_MODEL), lambda b, s, col=col: (b * nl + s, col))
    nxt = lambda col: pl.BlockSpec(
        (T, D_MODEL), lambda b, s, col=col: (2 * b * nl + jnp.minimum(2 * s + 2, 2 * nl - 1), col))
    return pl.pallas_call(
        _deltanet_kernel,
        grid=(batch, nl),
        in_specs=[
            blk(COL_DQ), blk(COL_DK), blk(COL_DV), blk(COL_DZ),
            nxt(COL_DQ), nxt(COL_DK), nxt(COL_DV),
            pl.BlockSpec((L, LANES), lambda b, s: (b * nl + s, 0)),
            pl.BlockSpec((1, HEADS, seq // DN_PAIR, DN_PAIR), lambda b, s: (b, 0, 0, 0)),
            pl.BlockSpec((DN_CONV_K, 3 * D_MODEL), lambda b, s: (0, 0)),
            pl.BlockSpec((1, HEAD_DIM), lambda b, s: (0, 0)),
        ],
        out_specs=pl.BlockSpec((L, D_MODEL), lambda b, s: (b * nl + s, 0)),
        out_shape=jax.ShapeDtypeStruct((t, D_MODEL), BF16),
        scratch_shapes=[
            pltpu.VMEM((3, DN_CONV_GROUP, D_MODEL), BF16),
            pltpu.VMEM((HEADS, HEAD_DIM, HEAD_DIM), F32),
            pltpu.VMEM((2, 3, T, D_MODEL), F32),
            pltpu.VMEM((2, T, D_MODEL), F32),
            pltpu.VMEM((2, T, D_MODEL), BF16),
            pltpu.VMEM((2, T, D_MODEL), BF16),
            pltpu.VMEM((2, T, D_MODEL), BF16),
            pltpu.VMEM((2, T, HEADS * DN_CHUNK), BF16),
        ],
        compiler_params=_cparams("arbitrary", "arbitrary"),
        name="deltanet",
    )(proj, proj, proj, proj, proj, proj, proj, gates, g_row, conv_w, norm_w)


def _moba_prep_kernel(q_ref, k_ref, v_ref, cos_ref, sin_ref, qt_ref, bias_ref, k16_ref, vt_ref, kmean_ref):
    step = pl.program_id(1)
    BS = MOBA_BLOCK
    nbp = kmean_ref.shape[1]
    hrange = range(HEADS)

    @pl.when(step == 0)
    def _():
        kmean_ref[...] = jnp.zeros_like(kmean_ref)

    def split(x):
        hi = x.astype(BF16)
        return hi, (x - hi.astype(F32)).astype(BF16)

    def one_block(sub):
        i = step * MOBA_PREP_BLOCKS + sub
        rows = slice(sub * BS, (sub + 1) * BS)
        cos = cos_ref[rows, :]
        sin = sin_ref[rows, :]
        rope = lambda x: x * cos + pltpu.roll(x, HEAD_DIM // 2, axis=1) * sin
        q = [rope(q_ref[rows, _head_cols(h)].astype(F32)) for h in hrange]
        k = [rope(k_ref[rows, _head_cols(h)].astype(F32)) for h in hrange]

        qs = [split(q[h]) for h in hrange]
        ms = [split(kmean_ref[h]) for h in hrange]
        gate = [_dot_nt(ms[h][0], qs[h][0]) + (_dot_nt(ms[h][0], qs[h][1]) + _dot_nt(ms[h][1], qs[h][0]))
                for h in hrange]
        blk = lax.broadcasted_iota(jnp.int32, (nbp, BS), 0)
        blk_f = blk.astype(F32)
        neg_inf = jnp.float32(-jnp.inf)
        g = [jnp.where(blk < i, gate[h], neg_inf) for h in hrange]
        sel = [blk == i for h in hrange]
        for _ in range(MOBA_TOPK):
            m = [jnp.max(g[h], axis=0, keepdims=True) for h in hrange]
            first = [jnp.min(jnp.where((g[h] == m[h]) & (g[h] > neg_inf), blk_f, float(nbp)),
                             axis=0, keepdims=True) for h in hrange]
            pick = [blk_f == first[h] for h in hrange]
            sel = [sel[h] | pick[h] for h in hrange]
            g = [jnp.where(pick[h], neg_inf, g[h]) for h in hrange]

        qt = [(q[h] * (HEAD_DIM ** -0.5 * LOG2_E)).T.astype(BF16) for h in hrange]
        vt = [v_ref[rows, _head_cols(h)].astype(F32).T.astype(BF16) for h in hrange]
        for h in hrange:
            qt_ref[0, h, :, rows] = qt[h]
            bias_ref[0, h, :, rows] = jnp.where(sel[h], 0.0, MASK_NEG).astype(BF16)
            k16_ref[0, h, rows, :] = k[h].astype(BF16)
            vt_ref[0, h, sub // MOBA_GROUP, :, (sub % MOBA_GROUP) * BS:(sub % MOBA_GROUP + 1) * BS] = vt[h]
            kmean_ref[h, pl.ds(i, 1), :] = jnp.mean(k[h], axis=0, keepdims=True)

    for sub in range(MOBA_PREP_BLOCKS):
        one_block(sub)


def _moba_prep(proj, cos_full, sin_signed, batch, seq):
    nb = seq // MOBA_BLOCK
    nbp = -(-nb // 16) * 16
    rows = MOBA_PREP_BLOCKS * MOBA_BLOCK
    ns = nb // MOBA_PREP_BLOCKS
    tiles = MOBA_PREP_BLOCKS // MOBA_GROUP
    blk = lambda col: pl.BlockSpec((rows, D_MODEL), lambda b, i, col=col: (b * ns + i, col))
    tab = pl.BlockSpec((rows, HEAD_DIM), lambda b, i: (i, 0))
    return pl.pallas_call(
        _moba_prep_kernel,
        grid=(batch, ns),
        in_specs=[blk(COL_MQ), blk(COL_MK), blk(COL_MV), tab, tab],
        out_specs=[pl.BlockSpec((1, HEADS, HEAD_DIM, rows), lambda b, i: (b, 0, 0, i)),
                   pl.BlockSpec((1, HEADS, nbp, rows), lambda b, i: (b, 0, 0, i)),
                   pl.BlockSpec((1, HEADS, rows, HEAD_DIM), lambda b, i: (b, 0, i, 0)),
                   pl.BlockSpec((1, HEADS, tiles, HEAD_DIM, MOBA_GROUP * MOBA_BLOCK),
                                lambda b, i: (b, 0, i, 0, 0))],
        out_shape=[jax.ShapeDtypeStruct((batch, HEADS, HEAD_DIM, seq), BF16),
                   jax.ShapeDtypeStruct((batch, HEADS, nbp, seq), BF16),
                   jax.ShapeDtypeStruct((batch, HEADS, seq, HEAD_DIM), BF16),
                   jax.ShapeDtypeStruct((batch, HEADS, nb // MOBA_GROUP, HEAD_DIM,
                                         MOBA_GROUP * MOBA_BLOCK), BF16)],
        scratch_shapes=[pltpu.VMEM((HEADS, nbp, HEAD_DIM), F32)],
        compiler_params=_cparams("parallel", "arbitrary"),
        name="moba_prep",
    )(proj, proj, proj, cos_full, sin_signed)


def _moba_attn_kernel(q_ref, bias_ref, k_ref, oh_ref, vt_ref, o_ref, sa_ref, sb_ref):
    tq_idx = pl.program_id(2)
    TK = MOBA_GROUP * MOBA_BLOCK
    TQ = MOBA_QTILE_KEYS * TK
    nbp = bias_ref.shape[2]
    qt = jnp.concatenate([q_ref[0, 0], bias_ref[0, 0], jnp.zeros((HEAD_DIM - nbp, TQ), BF16)], axis=0)

    def k_aug(j):
        rows = pl.ds(pl.multiple_of(j * TK, TK), TK)
        return jnp.concatenate([k_ref[0, 0, rows, :], oh_ref[rows, :]], axis=1)

    def scores(j):
        return _dot(k_aug(j), qt)

    def attend(s, j, m, l, acc):
        m_new = jnp.maximum(m, jnp.max(s, axis=0, keepdims=True))
        alpha = jnp.exp2(m - m_new)
        p = jnp.exp2(s - m_new)
        l = alpha * l + jnp.sum(p, axis=0, keepdims=True)
        acc = alpha * acc + _dot(vt_ref[0, 0, j], p.astype(BF16))
        return m_new, l, acc

    def step(g, cur_ref, nxt_ref, carry):
        nxt_ref[...] = scores(g + 1)
        return attend(cur_ref[...], g, *carry)

    assert MOBA_QTILE_KEYS == 2
    sa_ref[...] = scores(0)
    carry = (jnp.full((1, TQ), MASK_NEG, F32), jnp.zeros((1, TQ), F32), jnp.zeros((HEAD_DIM, TQ), F32))

    def two_steps(h, carry):
        carry = step(2 * h, sa_ref, sb_ref, carry)
        return step(2 * h + 1, sb_ref, sa_ref, carry)

    carry = lax.fori_loop(0, tq_idx // 2, lambda h, c: two_steps(2 * h + 1, two_steps(2 * h, c)), carry)
    carry = lax.fori_loop(tq_idx // 2 * 2, tq_idx, two_steps, carry)

    own0, own1 = 2 * tq_idx, 2 * tq_idx + 1
    half = slice(TK, TQ)
    tri = (lax.broadcasted_iota(jnp.int32, (TK, TK), 0) <= lax.broadcasted_iota(jnp.int32, (TK, TK), 1))
    sb_ref[:, half] = _dot(k_aug(own1), qt[:, half])
    s0 = sa_ref[...]
    s0 = jnp.concatenate([jnp.where(tri, s0[:, :TK], MASK_NEG), s0[:, half]], axis=1)
    m, l, acc = attend(s0, own0, *carry)
    m2, l2, acc2 = attend(jnp.where(tri, sb_ref[:, half], MASK_NEG), own1, m[:, half], l[:, half], acc[:, half])
    l = jnp.concatenate([l[:, :TK], l2], axis=1)
    acc = jnp.concatenate([acc[:, :TK], acc2], axis=1)
    o_ref[...] = (acc / l).T.astype(BF16)


def _moba_attn(q_t, bias_t, k16, block_onehot, v_t, batch, seq):
    tk = MOBA_GROUP * MOBA_BLOCK
    tq = MOBA_QTILE_KEYS * tk
    nbp = bias_t.shape[2]
    return pl.pallas_call(
        _moba_attn_kernel,
        grid=(batch, HEADS, seq // tq),
        in_specs=[
            pl.BlockSpec((1, 1, HEAD_DIM, tq), lambda b, h, i: (b, h, 0, i)),
            pl.BlockSpec((1, 1, nbp, tq), lambda b, h, i: (b, h, 0, i)),
            pl.BlockSpec((1, 1, seq, HEAD_DIM), lambda b, h, i: (b, h, 0, 0)),
            pl.BlockSpec((seq, HEAD_DIM), lambda b, h, i: (0, 0)),
            pl.BlockSpec((1, 1, seq // tk, HEAD_DIM, tk), lambda b, h, i: (b, h, 0, 0, 0)),
        ],
        out_specs=pl.BlockSpec((tq, HEAD_DIM), lambda b, h, i: (b * (seq // tq) + i, h)),
        out_shape=jax.ShapeDtypeStruct((batch * seq, D_MODEL), BF16),
        scratch_shapes=[pltpu.VMEM((tk, tq), F32), pltpu.VMEM((tk, tq), F32)],
        compiler_params=_cparams("parallel", "parallel", "arbitrary"),
        name="moba_attn",
    )(q_t, bias_t, k16, block_onehot, v_t)


def _merge_kernel(ax_ref, ac_ref, ab_ref, hx_ref, hc_ref, cw_ref, ydn_ref, ymb_ref,
                  ga_ref, gd_ref, gm_ref, x_ref, wa_ref, wd_ref, wm_ref, wo_ref, o_ref, *, seq):
    tm = ax_ref.shape[0]
    f32 = lambda ref: ref[...].astype(F32)
    first = (pl.program_id(0) * tm) % seq == 0
    halo = jnp.where(first, 0.0, (f32(hx_ref) * f32(hc_ref))[HALO_ROWS - SUBLANES:])
    p = f32(ax_ref) * f32(ac_ref)
    y_a = f32(ab_ref) * _causal_conv(p, halo, cw_ref[...], CONV_A_K)
    merged = (jax.nn.sigmoid(f32(ga_ref)) * _dot(y_a.astype(BF16), wa_ref[...])
              + jax.nn.sigmoid(f32(gd_ref)) * _dot(ydn_ref[...], wd_ref[...])
              + jax.nn.sigmoid(f32(gm_ref)) * _dot(ymb_ref[...], wm_ref[...]))
    o_ref[...] = x_ref[...] + _dot(merged.astype(BF16), wo_ref[...])


def _merge(proj_a, proj_b, conv_w, y_dn, y_mb, x, wa, wd, wm, wo, layer, seq, tm):
    t = x.shape[0]
    blk = lambda col: pl.BlockSpec((tm, D_MODEL), lambda i, col=col: (i, col))
    halo = lambda col: pl.BlockSpec(
        (HALO_ROWS, D_MODEL), lambda i, col=col: (jnp.maximum(i * (tm // HALO_ROWS) - 1, 0), col))
    row = pl.BlockSpec((tm, D_MODEL), lambda i: (i, 0))
    wspec = pl.BlockSpec((None, D_MODEL, D_MODEL), lambda i: (layer, 0, 0))
    return pl.pallas_call(
        functools.partial(_merge_kernel, seq=seq),
        grid=(t // tm,),
        in_specs=[blk(COL_AX), blk(COL_AC), blk(COL_AB), halo(COL_AX), halo(COL_AC),
                  pl.BlockSpec((CONV_A_K, D_MODEL), lambda i: (0, 0)),
                  row, row, blk(COL_GATE), blk(COL_GATE + 1), blk(COL_GATE + 2), row,
                  wspec, wspec, wspec, wspec],
        out_specs=row,
        out_shape=jax.ShapeDtypeStruct((t, D_MODEL), F32),
        compiler_params=_cparams("parallel"),
        name="merge",
    )(proj_a, proj_a, proj_a, proj_a, proj_a, conv_w, y_dn, y_mb, proj_b, proj_b, proj_b, x,
      wa, wd, wm, wo)


def _rms_scale(x, gain):
    return x * lax.rsqrt(jnp.mean(x * x, axis=-1, keepdims=True) + NORM_EPS) * gain


def _ffn_kernel(x_ref, g_ref, wg_ref, wu_ref, wd_ref, fg_ref, o_ref, h_ref, acc_ref, *, final_norm):
    j = pl.program_id(1)

    @pl.when(j == 0)
    def _():
        x = x_ref[...]
        h_ref[...] = _rms_scale(x, g_ref[...]).astype(BF16)
        acc_ref[...] = x

    h = h_ref[...]
    act = _silu(_dot(h, wg_ref[...])) * _dot(h, wu_ref[...])
    acc_ref[...] += _dot(act.astype(BF16), wd_ref[...])

    @pl.when(j == pl.num_programs(1) - 1)
    def _():
        y = acc_ref[...]
        o_ref[...] = _rms_scale(y, fg_ref[...]) if final_norm else y


def _ffn(x, gain, w_gate_up, w_down, layer, final_gain, final_norm, tm, th):
    t, d = x.shape
    hid = w_down.shape[1]
    nh = hid // th
    return pl.pallas_call(
        functools.partial(_ffn_kernel, final_norm=final_norm),
        grid=(t // tm, nh),
        in_specs=[
            pl.BlockSpec((tm, d), lambda i, j: (i, 0)),
            pl.BlockSpec((1, d), lambda i, j: (0, 0)),
            pl.BlockSpec((None, d, th), lambda i, j: (layer, 0, j)),
            pl.BlockSpec((None, d, th), lambda i, j: (layer, 0, j + nh)),
            pl.BlockSpec((None, th, d), lambda i, j: (layer, j, 0)),
            pl.BlockSpec((1, d), lambda i, j: (0, 0)),
        ],
        out_specs=pl.BlockSpec((tm, d), lambda i, j: (i, 0)),
        out_shape=jax.ShapeDtypeStruct((t, d), F32),
        scratch_shapes=[pltpu.VMEM((tm, d), BF16), pltpu.VMEM((tm, d), F32)],
        compiler_params=_cparams("parallel", "arbitrary"),
        name="ffn",
    )(x, gain, w_gate_up, w_gate_up, w_down, final_gain)


def _largest_tile(n, cap):
    t = cap
    while n % t:
        t //= 2
    return t


def _split_w_in_kernel(w_ref, tail_ref, oa_ref, ob_ref, os_ref, *, na):
    j = pl.program_id(1)
    shift = 2 * HEADS

    @pl.when(j == 0)
    def _():
        lane = lax.broadcasted_iota(jnp.int32, tail_ref.shape, 1)
        os_ref[...] = jnp.where(lane < shift, tail_ref[...], 0.0).astype(BF16)

    @pl.when(j < na)
    def _():
        oa_ref[...] = w_ref[...].astype(BF16)

    @pl.when(j >= na)
    def _():
        w = jnp.concatenate([w_ref[:, shift:], tail_ref[:, :shift]], axis=1)
        ob_ref[...] = w.astype(BF16)


def _split_w_in(w, tn):
    depth, d, n = w.shape
    na, nb = N_PROJ_A // tn, N_PROJ_B // tn
    per_tn = tn // N_SMALL
    w = w.astype(BF16).reshape(depth * d, n)
    return pl.pallas_call(
        functools.partial(_split_w_in_kernel, na=na),
        grid=(depth, na + nb),
        in_specs=[
            pl.BlockSpec((d, tn), lambda l, j: (l, j)),
            pl.BlockSpec((d, N_SMALL), lambda l, j: (l, (jnp.maximum(j, na - 1) + 1) * per_tn)),
        ],
        out_specs=[pl.BlockSpec((None, d, tn), lambda l, j: (l, 0, jnp.minimum(j, na - 1))),
                   pl.BlockSpec((None, d, tn), lambda l, j: (l, 0, jnp.maximum(j - na, 0))),
                   pl.BlockSpec((None, d, N_SMALL), lambda l, j: (l, 0, 0))],
        out_shape=[jax.ShapeDtypeStruct((depth, d, N_PROJ_A), BF16),
                   jax.ShapeDtypeStruct((depth, d, N_PROJ_B), BF16),
                   jax.ShapeDtypeStruct((depth, d, N_SMALL), BF16)],
        compiler_params=_cparams("parallel", "arbitrary"),
        name="split_w_in",
    )(w, w)


def _lane_row(vals, offset):
    row = jnp.zeros((1, LANES), F32)
    return row.at[0, offset:offset + vals.shape[0]].set(vals.astype(F32))


def kernel(x, attn_norm, w_in, conv_a_w, dn_conv_w, dn_a_log, dn_dt_bias, dn_norm, w_br_a, w_br_dn,
           w_br_moba, w_out, ffn_norm, w_gate_up, w_down, final_norm):
    batch, seq, d = x.shape
    depth = attn_norm.shape[0]
    assert d == D_MODEL and seq // MOBA_BLOCK <= LANES
    assert seq % (MOBA_QTILE_KEYS * MOBA_GROUP * MOBA_BLOCK) == 0
    assert seq % (MOBA_PREP_BLOCKS * MOBA_BLOCK) == 0 and MOBA_PREP_BLOCKS % MOBA_GROUP == 0
    t = batch * seq
    tm_huge = _largest_tile(t, 2048)
    assert seq % (2 * DN_TILE) == 0 and DN_TILE % DN_PAIR == 0 and (seq // DN_PAIR) % SUBLANES == 0

    inv = 1.0 / (ROPE_THETA ** (jnp.arange(0, HEAD_DIM, 2, dtype=F32) / HEAD_DIM))
    ang = jnp.arange(seq, dtype=F32)[:, None] * inv[None, :]
    cos_full = jnp.concatenate([jnp.cos(ang), jnp.cos(ang)], axis=-1)
    sin_signed = jnp.concatenate([-jnp.sin(ang), jnp.sin(ang)], axis=-1)
    block_onehot = (jnp.arange(seq)[:, None] // MOBA_BLOCK == jnp.arange(HEAD_DIM)[None, :]).astype(BF16)

    w_gate_up16, w_down16 = w_gate_up.astype(BF16), w_down.astype(BF16)
    w_br16 = [w.astype(BF16) for w in (w_br_a, w_br_dn, w_br_moba, w_out)]
    w_in16 = _split_w_in(w_in, 1024)

    xf = x.reshape(t, d)
    for l in range(depth):
        proj_a, proj_b, gates = _in_proj(xf, attn_norm[l][None, :], *w_in16, _lane_row(dn_a_log[l], HEADS),
                                         _lane_row(dn_dt_bias[l], HEADS), l, tm_huge, 1024)
        g_row = (gates[:, HEADS:2 * HEADS].reshape(batch, seq, HEADS).transpose(0, 2, 1)
                 .reshape(batch, HEADS, seq // DN_PAIR, DN_PAIR))
        y_dn = _deltanet(proj_a, gates, g_row, dn_conv_w[l], dn_norm[l][None, :], batch, seq)

        q_t, bias_t, k16, v_t = _moba_prep(proj_b, cos_full, sin_signed, batch, seq)
        y_mb = _moba_attn(q_t, bias_t, k16, block_onehot, v_t, batch, seq)

        x1 = _merge(proj_a, proj_b, conv_a_w[l], y_dn, y_mb, xf, *w_br16, l, seq, _largest_tile(t, 512))
        xf = _ffn(x1, ffn_norm[l][None, :], w_gate_up16, w_down16, l, final_norm[None, :],
                  l == depth - 1, tm_huge, 256)

    return xf.reshape(batch, seq, d)
```

```python
import functools

import jax
import jax.numpy as jnp
from jax import lax
from jax.experimental import pallas as pl
from jax.experimental.pallas import tpu as pltpu

F32 = jnp.float32
BF16 = jnp.bfloat16

D_MODEL = 1024
HEADS = 8
HEAD_DIM = 128
NORM_EPS = 1e-6
CONV_A_K = 3
DN_CONV_K = 4
DN_CHUNK = 64
DN_PAIR = 2 * DN_CHUNK
DN_TILE = 256
DN_CONV_GROUP = 128
MOBA_BLOCK = 256
MOBA_TOPK = 3
MOBA_GROUP = 2
MOBA_PREP_BLOCKS = 4
MOBA_QTILE_KEYS = 2
LOG2_E = 1.4426950408889634
ROPE_THETA = 10000.0
FFN_HIDDEN = 2816

COL_AX, COL_AC, COL_AB = 0, 1, 2
COL_DQ, COL_DK, COL_DV, COL_DZ = 3, 4, 5, 6
N_PROJ_A = 7 * D_MODEL
COL_MQ, COL_MK, COL_MV = 0, 1, 2
COL_GATE = 3
N_PROJ_B = 6 * D_MODEL
N_SMALL = 128

MASK_NEG = -1e30
VMEM_LIMIT = 56 * 1024 * 1024
SUBLANES = 8
LANES = 128
HALO_ROWS = 16


def _cparams(*sem):
    return pltpu.CompilerParams(dimension_semantics=sem, vmem_limit_bytes=VMEM_LIMIT)


def _silu(x):
    return x * jax.nn.sigmoid(x)


def _dot(a, b):
    return jnp.dot(a, b, preferred_element_type=F32)


def _dot_nt(a, b):
    return lax.dot_general(a, b, (((1,), (1,)), ((), ())), preferred_element_type=F32)


def _dot_tn(a, b):
    return lax.dot_general(a, b, (((0,), (0,)), ((), ())), preferred_element_type=F32)


def _dn_gates(raw, alog_row, dt_row):
    tm = raw.shape[0]
    xa = raw + dt_row
    softplus = jnp.maximum(xa, 0.0) + jnp.log1p(jnp.exp(-jnp.abs(xa)))
    g = -jnp.exp(alog_row) * softplus
    row = lax.broadcasted_iota(jnp.int32, (tm, LANES), 0) % DN_CHUNK
    shift = 1
    while shift < DN_CHUNK:
        g = g + jnp.where(row >= shift, pltpu.roll(g, shift, axis=0), 0.0)
        shift *= 2
    lane = lax.broadcasted_iota(jnp.int32, (tm, LANES), 1)
    return jnp.where(lane < HEADS, jax.nn.sigmoid(raw), g)


def _in_proj_kernel(x_ref, g_ref, wa_ref, wb_ref, ws_ref, alog_ref, dt_ref, oa_ref, ob_ref, os_ref, h_ref,
                    *, na):
    j = pl.program_id(1)

    @pl.when(j == 0)
    def _():
        x = x_ref[...]
        ms = jnp.mean(x * x, axis=-1, keepdims=True)
        h_ref[...] = (x * lax.rsqrt(ms + NORM_EPS) * g_ref[...]).astype(BF16)
        os_ref[...] = _dn_gates(_dot(h_ref[...], ws_ref[...]), alog_ref[...], dt_ref[...])

    @pl.when(j < na)
    def _():
        oa_ref[...] = _dot(h_ref[...], wa_ref[...]).astype(BF16)

    @pl.when(j >= na)
    def _():
        ob_ref[...] = _dot(h_ref[...], wb_ref[...]).astype(BF16)


def _in_proj(x, gain, wa, wb, ws, alog_row, dt_row, layer, tm, tn):
    t, d = x.shape
    assert tm % DN_CHUNK == 0
    lane_row = pl.BlockSpec((1, LANES), lambda i, j: (0, 0))
    na, nb = wa.shape[2] // tn, wb.shape[2] // tn
    a_idx = lambda j: jnp.minimum(j, na - 1)
    b_idx = lambda j: jnp.maximum(j - na, 0)
    return pl.pallas_call(
        functools.partial(_in_proj_kernel, na=na),
        grid=(t // tm, na + nb),
        in_specs=[
            pl.BlockSpec((tm, d), lambda i, j: (i, 0)),
            pl.BlockSpec((1, d), lambda i, j: (0, 0)),
            pl.BlockSpec((None, d, tn), lambda i, j: (layer, 0, a_idx(j))),
            pl.BlockSpec((None, d, tn), lambda i, j: (layer, 0, b_idx(j))),
            pl.BlockSpec((None, d, N_SMALL), lambda i, j: (layer, 0, 0)),
            lane_row, lane_row,
        ],
        out_specs=[pl.BlockSpec((tm, tn), lambda i, j: (i, a_idx(j))),
                   pl.BlockSpec((tm, tn), lambda i, j: (i, b_idx(j))),
                   pl.BlockSpec((tm, N_SMALL), lambda i, j: (i, 0))],
        out_shape=[jax.ShapeDtypeStruct((t, wa.shape[2]), BF16),
                   jax.ShapeDtypeStruct((t, wb.shape[2]), BF16),
                   jax.ShapeDtypeStruct((t, N_SMALL), F32)],
        scratch_shapes=[pltpu.VMEM((tm, d), BF16)],
        compiler_params=_cparams("parallel", "arbitrary"),
        name="in_proj",
    )(x, gain, wa, wb, ws, alog_row, dt_row)


def _causal_conv(x, carry, w, k):
    xe = jnp.concatenate([carry, x], axis=0)
    y = x * w[k - 1:k]
    for j in range(1, k):
        y = y + pltpu.roll(xe, j, axis=0)[SUBLANES:] * w[k - 1 - j:k - j]
    return y


def _head_cols(h):
    return slice(h * HEAD_DIM, (h + 1) * HEAD_DIM)


def _interleave(*streams):
    order = [((k + 0.5) / len(s), si, step) for si, s in enumerate(streams) for k, step in enumerate(s)]
    for _, _, step in sorted(order, key=lambda e: e[:2]):
        step()


def _deltanet_kernel(q_ref, k_ref, v_ref, z_ref, qx_ref, kx_ref, vx_ref, gates_ref, grow_ref, cw_ref,
                     nw_ref, o_ref, carry_ref, state_ref, qkv_ref,
                     u_ref, w_ref, qd_ref, kd_ref, qk_ref):
    T = DN_TILE
    C = DN_CHUNK
    P = DN_PAIR
    hrange = range(HEADS)
    step_idx = pl.program_id(1)

    G = DN_CONV_GROUP
    sr = lax.broadcasted_iota(jnp.int32, ((DN_CONV_K - 1) * G, 2 * G), 0)
    sc = lax.broadcasted_iota(jnp.int32, ((DN_CONV_K - 1) * G, 2 * G), 1)
    shift_all = (sc == G + sr % G - (sr // G + 1)).astype(BF16)

    def pre_stream(srcs, r0, buf):
        steps = []
        for idx, src in enumerate(srcs):
            for rg in range(T // G):
                for hp in range(HEADS // 2):
                    def piece(idx=idx, src=src, rg=rg, hp=hp):
                        cols = slice(hp * 2 * HEAD_DIM, (hp + 1) * 2 * HEAD_DIM)
                        cur = src[r0 + rg * G:r0 + (rg + 1) * G, cols]
                        prev = carry_ref[idx, :, cols] if rg == 0 else src[r0 + (rg - 1) * G:r0 + rg * G, cols]
                        sh = _dot(shift_all, jnp.concatenate([prev, cur], axis=0))
                        w = cw_ref[:, idx * D_MODEL + hp * 2 * HEAD_DIM:idx * D_MODEL + (hp + 1) * 2 * HEAD_DIM]
                        y = cur.astype(F32) * w[DN_CONV_K - 1:DN_CONV_K]
                        for j in range(1, DN_CONV_K):
                            y = y + sh[(j - 1) * G:j * G] * w[DN_CONV_K - 1 - j:DN_CONV_K - j]
                        y = _silu(y)
                        for e in range(2):
                            ye = y[:, e * HEAD_DIM:(e + 1) * HEAD_DIM]
                            if idx < 2:
                                scale = HEAD_DIM ** -0.5 if idx == 0 else 1.0
                                ye = ye * (lax.rsqrt(jnp.sum(ye * ye, axis=-1, keepdims=True) + NORM_EPS) * scale)
                            qkv_ref[buf, idx, rg * G:(rg + 1) * G, _head_cols(2 * hp + e)] = ye
                    steps.append(piece)

            def save(idx=idx, src=src):
                carry_ref[idx] = src[r0 + T - G:r0 + T, :]
            steps.append(save)
        return steps

    @pl.when(step_idx == 0)
    def _():
        carry_ref[...] = jnp.zeros_like(carry_ref)
        state_ref[...] = jnp.zeros_like(state_ref)
        for step in pre_stream((q_ref, k_ref, v_ref), 0, 0):
            step()

    ri = lax.broadcasted_iota(jnp.int32, (P, P), 0)
    ci = lax.broadcasted_iota(jnp.int32, (P, P), 1)
    same_chunk = (ri // C) == (ci // C)
    causal = same_chunk & (ci <= ri)
    strict = same_chunk & (ci < ri)
    eye = (ci == ri).astype(F32)
    level_masks = []
    bs = 1
    while bs < C:
        same = (ri // (2 * bs)) == (ci // (2 * bs))
        level_masks.append(same & ((ri % (2 * bs)) >= bs) & ((ci % (2 * bs)) < bs))
        bs *= 2
    level_masks16 = [jnp.where(m, 1.0, 0.0).astype(BF16) for m in level_masks[1:]]
    first_half = lax.broadcasted_iota(jnp.int32, (P, 1), 0) < C
    pair0 = step_idx * (2 * T // P)
    nw = nw_ref[...]

    def a_stream(buf, r0):
        probs = [(sl, h) for sl in range(T // P) for h in hrange]
        n = range(len(probs))
        rows = [slice(sl * P, (sl + 1) * P) for sl in range(T // P)]
        ld = lambda idx, i: qkv_ref[buf, idx, rows[probs[i][0]], _head_cols(probs[i][1])]
        st = {}

        def s_decay():
            gbs = [gates_ref[r0 + sl * P:r0 + (sl + 1) * P, :] for sl in range(T // P)]
            st["beta"] = [gbs[sl][:, h:h + 1] for sl, h in probs]
            st["gc"] = [gbs[sl][:, HEADS + h:HEADS + h + 1] for sl, h in probs]
            gr = [grow_ref[0, h, pl.ds(pair0 + r0 // P + sl, 1), :] for sl, h in probs]
            st["decay"] = [jnp.where(causal, jnp.exp(jnp.where(causal, st["gc"][i] - gr[i], 0.0)), 0.0)
                           for i in n]

        def s_a():
            k = [ld(1, i) for i in n]
            st["kb"] = [k[i] * st["beta"][i] for i in n]
            st["k16"] = [k[i].astype(BF16) for i in n]
            st["a"] = [jnp.where(strict, _dot_nt(st["kb"][i].astype(BF16), st["k16"][i]) * st["decay"][i], 0.0)
                       for i in n]
            st["t"] = [eye - jnp.where(level_masks[0], st["a"][i], 0.0) for i in n]
            st["a16"] = [st["a"][i].astype(BF16) for i in n]

        def s_tx(m16):
            st["t16"] = [st["t"][i].astype(BF16) for i in n]
            st["tx"] = [_dot(st["t16"][i], st["a16"][i] * m16) for i in n]

        def s_t():
            st["t"] = [st["t"][i] - _dot(st["tx"][i].astype(BF16), st["t16"][i]) for i in n]

        def s_uw():
            st["eg"] = [jnp.exp(st["gc"][i]) for i in n]
            rhs = [jnp.concatenate([ld(2, i) * st["beta"][i], st["kb"][i] * st["eg"][i]], axis=1).astype(BF16)
                   for i in n]
            st["uw"] = [_dot(st["t"][i].astype(BF16), rhs[i]) for i in n]

        def s_qk():
            st["qk"] = [(_dot_nt(ld(0, i).astype(BF16), st["k16"][i]) * st["decay"][i]).astype(BF16) for i in n]

        def s_store():
            for i, (sl, h) in enumerate(probs):
                r, gc = rows[sl], st["gc"][i]
                glast = jnp.where(first_half, gc[C - 1:C, :], gc[P - 1:P, :])
                u_ref[buf, r, _head_cols(h)] = st["uw"][i][:, :HEAD_DIM]
                w_ref[buf, r, _head_cols(h)] = st["uw"][i][:, HEAD_DIM:].astype(BF16)
                qd_ref[buf, r, _head_cols(h)] = (ld(0, i) * st["eg"][i]).astype(BF16)
                kd_ref[buf, r, _head_cols(h)] = (ld(1, i) * jnp.exp(glast - gc)).astype(BF16)
                qk_ref[buf, sl * P:sl * P + C, h * C:(h + 1) * C] = st["qk"][i][:C, :C]
                qk_ref[buf, sl * P + C:(sl + 1) * P, h * C:(h + 1) * C] = st["qk"][i][C:, C:]

        steps = [s_decay, s_a]
        for m16 in level_masks16:
            steps += [functools.partial(s_tx, m16), s_t]
        return steps + [s_uw, s_qk, s_store]

    def b_stream(buf, r0):
        st = {}
        steps = []
        for c in range(T // C):
            rc = slice(c * C, (c + 1) * C)
            rg = slice(r0 + c * C, r0 + (c + 1) * C)

            def s_ws(rc=rc):
                st["s"] = [state_ref[h] for h in hrange]
                lhs = [jnp.concatenate([w_ref[buf, rc, _head_cols(h)], qd_ref[buf, rc, _head_cols(h)]], axis=0)
                       for h in hrange]
                st["ws"] = [_dot(lhs[h], st["s"][h].astype(BF16)) for h in hrange]

            def s_state(rc=rc, rg=rg):
                gb = gates_ref[rg, :]
                vn16 = [(u_ref[buf, rc, _head_cols(h)] - st["ws"][h][:C]).astype(BF16) for h in hrange]
                st["o"] = [st["ws"][h][C:] + _dot(qk_ref[buf, rc, h * C:(h + 1) * C], vn16[h]) for h in hrange]
                for h in hrange:
                    cd = jnp.exp(gb[C - 1:C, HEADS + h:HEADS + h + 1])
                    state_ref[h] = st["s"][h] * cd + _dot_tn(kd_ref[buf, rc, _head_cols(h)], vn16[h])

            def s_out(rg=rg):
                for h in hrange:
                    o = st["o"][h]
                    on = o * lax.rsqrt(jnp.mean(o * o, axis=-1, keepdims=True) + NORM_EPS) * nw
                    o_ref[rg, _head_cols(h)] = (on * _silu(z_ref[rg, _head_cols(h)].astype(F32))).astype(BF16)

            steps += [s_ws, s_state, s_out]
        return steps

    _interleave(a_stream(0, 0), pre_stream((q_ref, k_ref, v_ref), T, 1))
    _interleave(a_stream(1, T), b_stream(0, 0))
    _interleave(b_stream(1, T), pre_stream((qx_ref, kx_ref, vx_ref), 0, 0))


def _deltanet(proj, gates, g_row, conv_w, norm_w, batch, seq):
    t = proj.shape[0]
    T = DN_TILE
    L = 2 * T
    nl = seq // L
    blk = lambda col: pl.BlockSpec((L, D_MODEL), lambda b, s, col=col: (b * nl + s, col))
    nxt = lambda col: pl.BlockSpec(
        (T, D_MODEL), lambda b, s, col=col: (2 * b * nl + jnp.minimum(2 * s + 2, 2 * nl - 1), col))
    return pl.pallas_call(
        _deltanet_kernel,
        grid=(batch, nl),
        in_specs=[
            blk(COL_DQ), blk(COL_DK), blk(COL_DV), blk(COL_DZ),
            nxt(COL_DQ), nxt(COL_DK), nxt(COL_DV),
            pl.BlockSpec((L, LANES), lambda b, s: (b * nl + s, 0)),
            pl.BlockSpec((1, HEADS, seq // DN_PAIR, DN_PAIR), lambda b, s: (b, 0, 0, 0)),
            pl.BlockSpec((DN_CONV_K, 3 * D_MODEL), lambda b, s: (0, 0)),
            pl.BlockSpec((1, HEAD_DIM), lambda b, s: (0, 0)),
        ],
        out_specs=pl.BlockSpec((L, D_MODEL), lambda b, s: (b * nl + s, 0)),
        out_shape=jax.ShapeDtypeStruct((t, D_MODEL), BF16),
        scratch_shapes=[
            pltpu.VMEM((3, DN_CONV_GROUP, D_MODEL), BF16),
            pltpu.VMEM((HEADS, HEAD_DIM, HEAD_DIM), F32),
            pltpu.VMEM((2, 3, T, D_MODEL), F32),
            pltpu.VMEM((2, T, D_MODEL), F32),
            pltpu.VMEM((2, T, D_MODEL), BF16),
            pltpu.VMEM((2, T, D_MODEL), BF16),
            pltpu.VMEM((2, T, D_MODEL), BF16),
            pltpu.VMEM((2, T, HEADS * DN_CHUNK), BF16),
        ],
        compiler_params=_cparams("arbitrary", "arbitrary"),
        name="deltanet",
    )(proj, proj, proj, proj, proj, proj, proj, gates, g_row, conv_w, norm_w)


def _moba_prep_kernel(q_ref, k_ref, v_ref, cos_ref, sin_ref, qt_ref, bias_ref, k16_ref, vt_ref, kmean_ref):
    step = pl.program_id(1)
    BS = MOBA_BLOCK
    nbp = kmean_ref.shape[1]
    hrange = range(HEADS)

    @pl.when(step == 0)
    def _():
        kmean_ref[...] = jnp.zeros_like(kmean_ref)

    def split(x):
        hi = x.astype(BF16)
        return hi, (x - hi.astype(F32)).astype(BF16)

    def one_block(sub):
        i = step * MOBA_PREP_BLOCKS + sub
        rows = slice(sub * BS, (sub + 1) * BS)
        cos = cos_ref[rows, :]
        sin = sin_ref[rows, :]
        rope = lambda x: x * cos + pltpu.roll(x, HEAD_DIM // 2, axis=1) * sin
        q = [rope(q_ref[rows, _head_cols(h)].astype(F32)) for h in hrange]
        k = [rope(k_ref[rows, _head_cols(h)].astype(F32)) for h in hrange]

        qs = [split(q[h]) for h in hrange]
        ms = [split(kmean_ref[h]) for h in hrange]
        gate = [_dot_nt(ms[h][0], qs[h][0]) + (_dot_nt(ms[h][0], qs[h][1]) + _dot_nt(ms[h][1], qs[h][0]))
                for h in hrange]
        blk = lax.broadcasted_iota(jnp.int32, (nbp, BS), 0)
        blk_f = blk.astype(F32)
        neg_inf = jnp.float32(-jnp.inf)
        g = [jnp.where(blk < i, gate[h], neg_inf) for h in hrange]
        sel = [blk == i for h in hrange]
        for _ in range(MOBA_TOPK):
            m = [jnp.max(g[h], axis=0, keepdims=True) for h in hrange]
            first = [jnp.min(jnp.where((g[h] == m[h]) & (g[h] > neg_inf), blk_f, float(nbp)),
                             axis=0, keepdims=True) for h in hrange]
            pick = [blk_f == first[h] for h in hrange]
            sel = [sel[h] | pick[h] for h in hrange]
            g = [jnp.where(pick[h], neg_inf, g[h]) for h in hrange]

        qt = [(q[h] * (HEAD_DIM ** -0.5 * LOG2_E)).T.astype(BF16) for h in hrange]
        vt = [v_ref[rows, _head_cols(h)].astype(F32).T.astype(BF16) for h in hrange]
        for h in hrange:
            qt_ref[0, h, :, rows] = qt[h]
            bias_ref[0, h, :, rows] = jnp.where(sel[h], 0.0, MASK_NEG).astype(BF16)
            k16_ref[0, h, rows, :] = k[h].astype(BF16)
            vt_ref[0, h, sub // MOBA_GROUP, :, (sub % MOBA_GROUP) * BS:(sub % MOBA_GROUP + 1) * BS] = vt[h]
            kmean_ref[h, pl.ds(i, 1), :] = jnp.mean(k[h], axis=0, keepdims=True)

    for sub in range(MOBA_PREP_BLOCKS):
        one_block(sub)


def _moba_prep(proj, cos_full, sin_signed, batch, seq):
    nb = seq // MOBA_BLOCK
    nbp = -(-nb // 16) * 16
    rows = MOBA_PREP_BLOCKS * MOBA_BLOCK
    ns = nb // MOBA_PREP_BLOCKS
    tiles = MOBA_PREP_BLOCKS // MOBA_GROUP
    blk = lambda col: pl.BlockSpec((rows, D_MODEL), lambda b, i, col=col: (b * ns + i, col))
    tab = pl.BlockSpec((rows, HEAD_DIM), lambda b, i: (i, 0))
    return pl.pallas_call(
        _moba_prep_kernel,
        grid=(batch, ns),
        in_specs=[blk(COL_MQ), blk(COL_MK), blk(COL_MV), tab, tab],
        out_specs=[pl.BlockSpec((1, HEADS, HEAD_DIM, rows), lambda b, i: (b, 0, 0, i)),
                   pl.BlockSpec((1, HEADS, nbp, rows), lambda b, i: (b, 0, 0, i)),
                   pl.BlockSpec((1, HEADS, rows, HEAD_DIM), lambda b, i: (b, 0, i, 0)),
                   pl.BlockSpec((1, HEADS, tiles, HEAD_DIM, MOBA_GROUP * MOBA_BLOCK),
                                lambda b, i: (b, 0, i, 0, 0))],
        out_shape=[jax.ShapeDtypeStruct((batch, HEADS, HEAD_DIM, seq), BF16),
                   jax.ShapeDtypeStruct((batch, HEADS, nbp, seq), BF16),
                   jax.ShapeDtypeStruct((batch, HEADS, seq, HEAD_DIM), BF16),
                   jax.ShapeDtypeStruct((batch, HEADS, nb // MOBA_GROUP, HEAD_DIM,
                                         MOBA_GROUP * MOBA_BLOCK), BF16)],
        scratch_shapes=[pltpu.VMEM((HEADS, nbp, HEAD_DIM), F32)],
        compiler_params=_cparams("parallel", "arbitrary"),
        name="moba_prep",
    )(proj, proj, proj, cos_full, sin_signed)


def _moba_attn_kernel(q_ref, bias_ref, k_ref, oh_ref, vt_ref, o_ref, sa_ref, sb_ref):
    tq_idx = pl.program_id(2)
    TK = MOBA_GROUP * MOBA_BLOCK
    TQ = MOBA_QTILE_KEYS * TK
    nbp = bias_ref.shape[2]
    qt = jnp.concatenate([q_ref[0, 0], bias_ref[0, 0], jnp.zeros((HEAD_DIM - nbp, TQ), BF16)], axis=0)

    def k_aug(j):
        rows = pl.ds(pl.multiple_of(j * TK, TK), TK)
        return jnp.concatenate([k_ref[0, 0, rows, :], oh_ref[rows, :]], axis=1)

    def scores(j):
        return _dot(k_aug(j), qt)

    def attend(s, j, m, l, acc):
        m_new = jnp.maximum(m, jnp.max(s, axis=0, keepdims=True))
        alpha = jnp.exp2(m - m_new)
        p = jnp.exp2(s - m_new)
        l = alpha * l + jnp.sum(p, axis=0, keepdims=True)
        acc = alpha * acc + _dot(vt_ref[0, 0, j], p.astype(BF16))
        return m_new, l, acc

    def step(g, cur_ref, nxt_ref, carry):
        nxt_ref[...] = scores(g + 1)
        m, l, acc = carry
        halves = [attend(cur_ref[:, hl], g, m[:, hl], l[:, hl], acc[:, hl])
                  for hl in (slice(0, TK), slice(TK, TQ))]
        return tuple(jnp.concatenate([a, b], axis=1) for a, b in zip(*halves))

    assert MOBA_QTILE_KEYS == 2
    sa_ref[...] = scores(0)
    carry = (jnp.full((1, TQ), MASK_NEG, F32), jnp.zeros((1, TQ), F32), jnp.zeros((HEAD_DIM, TQ), F32))

    def two_steps(h, carry):
        carry = step(2 * h, sa_ref, sb_ref, carry)
        return step(2 * h + 1, sb_ref, sa_ref, carry)

    carry = lax.fori_loop(0, tq_idx // 2, lambda h, c: two_steps(2 * h + 1, two_steps(2 * h, c)), carry)
    carry = lax.fori_loop(tq_idx // 2 * 2, tq_idx, two_steps, carry)

    own0, own1 = 2 * tq_idx, 2 * tq_idx + 1
    half = slice(TK, TQ)
    tri = (lax.broadcasted_iota(jnp.int32, (TK, TK), 0) <= lax.broadcasted_iota(jnp.int32, (TK, TK), 1))
    sb_ref[:, half] = _dot(k_aug(own1), qt[:, half])
    s0 = sa_ref[...]
    s0 = jnp.concatenate([jnp.where(tri, s0[:, :TK], MASK_NEG), s0[:, half]], axis=1)
    m, l, acc = attend(s0, own0, *carry)
    m2, l2, acc2 = attend(jnp.where(tri, sb_ref[:, half], MASK_NEG), own1, m[:, half], l[:, half], acc[:, half])
    l = jnp.concatenate([l[:, :TK], l2], axis=1)
    acc = jnp.concatenate([acc[:, :TK], acc2], axis=1)
    o_ref[...] = (acc / l).T.astype(BF16)


def _moba_attn(q_t, bias_t, k16, block_onehot, v_t, batch, seq):
    tk = MOBA_GROUP * MOBA_BLOCK
    tq = MOBA_QTILE_KEYS * tk
    nbp = bias_t.shape[2]
    return pl.pallas_call(
        _moba_attn_kernel,
        grid=(batch, HEADS, seq // tq),
        in_specs=[
            pl.BlockSpec((1, 1, HEAD_DIM, tq), lambda b, h, i: (b, h, 0, i)),
            pl.BlockSpec((1, 1, nbp, tq), lambda b, h, i: (b, h, 0, i)),
            pl.BlockSpec((1, 1, seq, HEAD_DIM), lambda b, h, i: (b, h, 0, 0)),
            pl.BlockSpec((seq, HEAD_DIM), lambda b, h, i: (0, 0)),
            pl.BlockSpec((1, 1, seq // tk, HEAD_DIM, tk), lambda b, h, i: (b, h, 0, 0, 0)),
        ],
        out_specs=pl.BlockSpec((tq, HEAD_DIM), lambda b, h, i: (b * (seq // tq) + i, h)),
        out_shape=jax.ShapeDtypeStruct((batch * seq, D_MODEL), BF16),
        scratch_shapes=[pltpu.VMEM((tk, tq), F32), pltpu.VMEM((tk, tq), F32)],
        compiler_params=_cparams("parallel", "parallel", "arbitrary"),
        name="moba_attn",
    )(q_t, bias_t, k16, block_onehot, v_t)


def _merge_kernel(ax_ref, ac_ref, ab_ref, hx_ref, hc_ref, cw_ref, ydn_ref, ymb_ref,
                  ga_ref, gd_ref, gm_ref, x_ref, wa_ref, wd_ref, wm_ref, wo_ref, o_ref, *, seq):
    tm = ax_ref.shape[0]
    f32 = lambda ref: ref[...].astype(F32)
    first = (pl.program_id(0) * tm) % seq == 0
    halo = jnp.where(first, 0.0, (f32(hx_ref) * f32(hc_ref))[HALO_ROWS - SUBLANES:])
    p = f32(ax_ref) * f32(ac_ref)
    y_a = f32(ab_ref) * _causal_conv(p, halo, cw_ref[...], CONV_A_K)
    merged = (jax.nn.sigmoid(f32(ga_ref)) * _dot(y_a.astype(BF16), wa_ref[...])
              + jax.nn.sigmoid(f32(gd_ref)) * _dot(ydn_ref[...], wd_ref[...])
              + jax.nn.sigmoid(f32(gm_ref)) * _dot(ymb_ref[...], wm_ref[...]))
    o_ref[...] = x_ref[...] + _dot(merged.astype(BF16), wo_ref[...])


def _merge(proj_a, proj_b, conv_w, y_dn, y_mb, x, wa, wd, wm, wo, layer, seq, tm):
    t = x.shape[0]
    blk = lambda col: pl.BlockSpec((tm, D_MODEL), lambda i, col=col: (i, col))
    halo = lambda col: pl.BlockSpec(
        (HALO_ROWS, D_MODEL), lambda i, col=col: (jnp.maximum(i * (tm // HALO_ROWS) - 1, 0), col))
    row = pl.BlockSpec((tm, D_MODEL), lambda i: (i, 0))
    wspec = pl.BlockSpec((None, D_MODEL, D_MODEL), lambda i: (layer, 0, 0))
    return pl.pallas_call(
        functools.partial(_merge_kernel, seq=seq),
        grid=(t // tm,),
        in_specs=[blk(COL_AX), blk(COL_AC), blk(COL_AB), halo(COL_AX), halo(COL_AC),
                  pl.BlockSpec((CONV_A_K, D_MODEL), lambda i: (0, 0)),
                  row, row, blk(COL_GATE), blk(COL_GATE + 1), blk(COL_GATE + 2), row,
                  wspec, wspec, wspec, wspec],
        out_specs=row,
        out_shape=jax.ShapeDtypeStruct((t, D_MODEL), F32),
        compiler_params=_cparams("parallel"),
        name="merge",
    )(proj_a, proj_a, proj_a, proj_a, proj_a, conv_w, y_dn, y_mb, proj_b, proj_b, proj_b, x,
      wa, wd, wm, wo)


def _rms_scale(x, gain):
    return x * lax.rsqrt(jnp.mean(x * x, axis=-1, keepdims=True) + NORM_EPS) * gain


def _ffn_kernel(x_ref, g_ref, wg_ref, wu_ref, wd_ref, fg_ref, o_ref, h_ref, acc_ref, *, final_norm):
    j = pl.program_id(1)

    @pl.when(j == 0)
    def _():
        x = x_ref[...]
        h_ref[...] = _rms_scale(x, g_ref[...]).astype(BF16)
        acc_ref[...] = x

    h = h_ref[...]
    act = _silu(_dot(h, wg_ref[...])) * _dot(h, wu_ref[...])
    acc_ref[...] += _dot(act.astype(BF16), wd_ref[...])

    @pl.when(j == pl.num_programs(1) - 1)
    def _():
        y = acc_ref[...]
        o_ref[...] = _rms_scale(y, fg_ref[...]) if final_norm else y


def _ffn(x, gain, w_gate_up, w_down, layer, final_gain, final_norm, tm, th):
    t, d = x.shape
    hid = w_down.shape[1]
    nh = hid // th
    return pl.pallas_call(
        functools.partial(_ffn_kernel, final_norm=final_norm),
        grid=(t // tm, nh),
        in_specs=[
            pl.BlockSpec((tm, d), lambda i, j: (i, 0)),
            pl.BlockSpec((1, d), lambda i, j: (0, 0)),
            pl.BlockSpec((None, d, th), lambda i, j: (layer, 0, j)),
            pl.BlockSpec((None, d, th), lambda i, j: (layer, 0, j + nh)),
            pl.BlockSpec((None, th, d), lambda i, j: (layer, j, 0)),
            pl.BlockSpec((1, d), lambda i, j: (0, 0)),
        ],
        out_specs=pl.BlockSpec((tm, d), lambda i, j: (i, 0)),
        out_shape=jax.ShapeDtypeStruct((t, d), F32),
        scratch_shapes=[pltpu.VMEM((tm, d), BF16), pltpu.VMEM((tm, d), F32)],
        compiler_params=_cparams("parallel", "arbitrary"),
        name="ffn",
    )(x, gain, w_gate_up, w_gate_up, w_down, final_gain)


def _largest_tile(n, cap):
    t = cap
    while n % t:
        t //= 2
    return t


def _split_w_in_kernel(w_ref, tail_ref, oa_ref, ob_ref, os_ref, *, na):
    j = pl.program_id(1)
    shift = 2 * HEADS

    @pl.when(j == 0)
    def _():
        lane = lax.broadcasted_iota(jnp.int32, tail_ref.shape, 1)
        os_ref[...] = jnp.where(lane < shift, tail_ref[...], 0.0).astype(BF16)

    @pl.when(j < na)
    def _():
        oa_ref[...] = w_ref[...].astype(BF16)

    @pl.when(j >= na)
    def _():
        w = jnp.concatenate([w_ref[:, shift:], tail_ref[:, :shift]], axis=1)
        ob_ref[...] = w.astype(BF16)


def _split_w_in(w, tn):
    depth, d, n = w.shape
    na, nb = N_PROJ_A // tn, N_PROJ_B // tn
    per_tn = tn // N_SMALL
    w = w.astype(BF16).reshape(depth * d, n)
    return pl.pallas_call(
        functools.partial(_split_w_in_kernel, na=na),
        grid=(depth, na + nb),
        in_specs=[
            pl.BlockSpec((d, tn), lambda l, j: (l, j)),
            pl.BlockSpec((d, N_SMALL), lambda l, j: (l, (jnp.maximum(j, na - 1) + 1) * per_tn)),
        ],
        out_specs=[pl.BlockSpec((None, d, tn), lambda l, j: (l, 0, jnp.minimum(j, na - 1))),
                   pl.BlockSpec((None, d, tn), lambda l, j: (l, 0, jnp.maximum(j - na, 0))),
                   pl.BlockSpec((None, d, N_SMALL), lambda l, j: (l, 0, 0))],
        out_shape=[jax.ShapeDtypeStruct((depth, d, N_PROJ_A), BF16),
                   jax.ShapeDtypeStruct((depth, d, N_PROJ_B), BF16),
                   jax.ShapeDtypeStruct((depth, d, N_SMALL), BF16)],
        compiler_params=_cparams("parallel", "arbitrary"),
        name="split_w_in",
    )(w, w)


def _lane_row(vals, offset):
    row = jnp.zeros((1, LANES), F32)
    return row.at[0, offset:offset + vals.shape[0]].set(vals.astype(F32))


def kernel(x, attn_norm, w_in, conv_a_w, dn_conv_w, dn_a_log, dn_dt_bias, dn_norm, w_br_a, w_br_dn,
           w_br_moba, w_out, ffn_norm, w_gate_up, w_down, final_norm):
    batch, seq, d = x.shape
    depth = attn_norm.shape[0]
    assert d == D_MODEL and seq // MOBA_BLOCK <= LANES
    assert seq % (MOBA_QTILE_KEYS * MOBA_GROUP * MOBA_BLOCK) == 0
    assert seq % (MOBA_PREP_BLOCKS * MOBA_BLOCK) == 0 and MOBA_PREP_BLOCKS % MOBA_GROUP == 0
    t = batch * seq
    tm_huge = _largest_tile(t, 2048)
    assert seq % (2 * DN_TILE) == 0 and DN_TILE % DN_PAIR == 0 and (seq // DN_PAIR) % SUBLANES == 0

    inv = 1.0 / (ROPE_THETA ** (jnp.arange(0, HEAD_DIM, 2, dtype=F32) / HEAD_DIM))
    ang = jnp.arange(seq, dtype=F32)[:, None] * inv[None, :]
    cos_full = jnp.concatenate([jnp.cos(ang), jnp.cos(ang)], axis=-1)
    sin_signed = jnp.concatenate([-jnp.sin(ang), jnp.sin(ang)], axis=-1)
    block_onehot = (jnp.arange(seq)[:, None] // MOBA_BLOCK == jnp.arange(HEAD_DIM)[None, :]).astype(BF16)

    w_gate_up16, w_down16 = w_gate_up.astype(BF16), w_down.astype(BF16)
    w_br16 = [w.astype(BF16) for w in (w_br_a, w_br_dn, w_br_moba, w_out)]
    w_in16 = _split_w_in(w_in, 1024)

    xf = x.reshape(t, d)
    for l in range(depth):
        proj_a, proj_b, gates = _in_proj(xf, attn_norm[l][None, :], *w_in16, _lane_row(dn_a_log[l], HEADS),
                                         _lane_row(dn_dt_bias[l], HEADS), l, tm_huge, 1024)
        g_row = (gates[:, HEADS:2 * HEADS].reshape(batch, seq, HEADS).transpose(0, 2, 1)
                 .reshape(batch, HEADS, seq // DN_PAIR, DN_PAIR))
        y_dn = _deltanet(proj_a, gates, g_row, dn_conv_w[l], dn_norm[l][None, :], batch, seq)

        q_t, bias_t, k16, v_t = _moba_prep(proj_b, cos_full, sin_signed, batch, seq)
        y_mb = _moba_attn(q_t, bias_t, k16, block_onehot, v_t, batch, seq)

        x1 = _merge(proj_a, proj_b, conv_a_w[l], y_dn, y_mb, xf, *w_br16, l, seq, _largest_tile(t, 512))
        xf = _ffn(x1, ffn_norm[l][None, :], w_gate_up16, w_down16, l, final_norm[None, :],
                  l == depth - 1, tm_huge, 256)

    return xf.reshape(batch, seq, d)
```
